```python
import math
import jax, jax.numpy as jnp
from jax import lax
import numpy as np

D_MODEL = 1024
BATCH = 32
SEQ = 2048
DEPTH = 1
DEC_BATCH = 8
DEC_SEQ = 2048
PAST_LEN = 128

S5_WIDTH = D_MODEL // 2
S5_GROUP_CH = 16
S5_GROUPS = S5_WIDTH // S5_GROUP_CH
S5_STATE = 64
ATT_HEADS = 8
ATT_HEAD_DIM = D_MODEL // (2 * ATT_HEADS)
ATT_WIDTH = ATT_HEADS * 2 * ATT_HEAD_DIM
ROPE_THETA = 10000.0
Q_BLOCK = 128
N_IN = S5_WIDTH + 3 * ATT_WIDTH + 2 * D_MODEL
PEER_HEADS = 8
PEER_NKEYS = 128
PEER_EXPERTS = PEER_NKEYS * PEER_NKEYS
PEER_TOPK = 16
PEER_QDIM = 256
PEER_HALF = PEER_QDIM // 2
PEER_BLOCK = 128
DN_ALPHA = (2 * DEPTH) ** 0.25
DN_BETA = (8 * DEPTH) ** -0.25
LN_EPS = 1e-5

kernel_name = 'hybrid_s5_diffattn_peer_encoder'


def _layernorm(x, gain=None, bias=None):
    xf = x.astype(jnp.float32)
    mu = jnp.mean(xf, axis=-1, keepdims=True)
    var = jnp.mean(jnp.square(xf - mu), axis=-1, keepdims=True)
    y = (xf - mu) * lax.rsqrt(var + LN_EPS)
    if gain is not None:
        y = y * gain.astype(jnp.float32) + bias.astype(jnp.float32)
    return y


def _rope(t, positions):
    half = ATT_HEAD_DIM // 2
    inv_freq = ROPE_THETA ** (-jnp.arange(half, dtype=jnp.float32) * 2.0 / ATT_HEAD_DIM)
    ang = positions[:, None] * inv_freq[None, :]
    cos = jnp.cos(ang)[None, :, None, None, :]
    sin = jnp.sin(ang)[None, :, None, None, :]
    t1, t2 = t[..., :half], t[..., half:]
    return jnp.concatenate([t1 * cos - t2 * sin, t2 * cos + t1 * sin], axis=-1)


def _complex_scan_combine(earlier, later):
    ar1, ai1, br1, bi1 = earlier
    ar2, ai2, br2, bi2 = later
    return (ar2 * ar1 - ai2 * ai1,
            ar2 * ai1 + ai2 * ar1,
            ar2 * br1 - ai2 * bi1 + br2,
            ar2 * bi1 + ai2 * br1 + bi2)


def _s5_bidirectional(u, lam_re, lam_im, log_dt, b_re, b_im, c_re, c_im, d_skip):
    bsz, seq, _ = u.shape
    f32 = jnp.float32
    uf = u.astype(f32)
    ug = uf.reshape(bsz, seq, S5_GROUPS, S5_GROUP_CH)
    y = uf * d_skip.astype(f32)
    for direction in range(2):
        lr = lam_re[direction].astype(f32)
        li = lam_im[direction].astype(f32)
        dt = jnp.exp(log_dt[direction].astype(f32))[:, None]
        mag = jnp.exp(lr * dt)
        lbr = mag * jnp.cos(li * dt)
        lbi = mag * jnp.sin(li * dt)
        den = lr * lr + li * li
        nr = lbr - 1.0
        cr = (nr * lr + lbi * li) / den
        ci = (lbi * lr - nr * li) / den
        br = b_re[direction].astype(f32)
        bi = b_im[direction].astype(f32)
        bbr = cr[..., None] * br - ci[..., None] * bi
        bbi = cr[..., None] * bi + ci[..., None] * br
        xr = jnp.einsum('blgc,gpc->blgp', ug, bbr)
        xi = jnp.einsum('blgc,gpc->blgp', ug, bbi)
        ar = jnp.broadcast_to(lbr[None, None], (1, seq, S5_GROUPS, S5_STATE))
        ai = jnp.broadcast_to(lbi[None, None], (1, seq, S5_GROUPS, S5_STATE))
        _, _, hr, hi = lax.associative_scan(_complex_scan_combine, (ar, ai, xr, xi),
                                            reverse=(direction == 1), axis=1)
        yd = (jnp.einsum('blgp,gcp->blgc', hr, c_re[direction].astype(f32))
              - jnp.einsum('blgp,gcp->blgc', hi, c_im[direction].astype(f32)))
        y = y + yd.reshape(bsz, seq, S5_WIDTH)
    return y


def _diff_attention(q, k, v, lam):
    bsz, seq = q.shape[:2]
    nblk = seq // Q_BLOCK
    scale = ATT_HEAD_DIM ** -0.5
    qb = q.reshape(bsz, nblk, Q_BLOCK, ATT_HEADS, 2, ATT_HEAD_DIM).transpose(1, 0, 2, 3, 4, 5)

    def block(qblk):
        s = jnp.einsum('bqhcd,bkhcd->bchqk', qblk, k, preferred_element_type=jnp.float32) * scale
        p = jax.nn.softmax(s, axis=-1)
        pd = p[:, 0] - lam * p[:, 1]
        return jnp.einsum('bhqk,bkhe->bqhe', pd, v)

    o = lax.map(block, qb)
    return o.transpose(1, 0, 2, 3, 4).reshape(bsz, seq, ATT_HEADS, 2 * ATT_HEAD_DIM)


def _peer(h, w_q, keys, u_tab, v_tab):
    bsz, seq, dm = h.shape
    nblk = (bsz * seq) // PEER_BLOCK
    hb = h.reshape(nblk, PEER_BLOCK, dm)
    keys_f = keys.astype(jnp.float32)

    def block(xb):
        q = jnp.dot(xb, w_q).astype(jnp.float32).reshape(PEER_BLOCK, PEER_HEADS, 2, PEER_HALF)
        s = jnp.einsum('thcd,hcnd->thcn', q, keys_f)
        sv, si = lax.top_k(s, PEER_TOPK)
        cand = (sv[:, :, 0, :, None] + sv[:, :, 1, None, :]).reshape(
            PEER_BLOCK, PEER_HEADS, PEER_TOPK * PEER_TOPK)
        cv, ci = lax.top_k(cand, PEER_TOPK)
        i1 = jnp.take_along_axis(si[:, :, 0], ci // PEER_TOPK, axis=-1)
        i2 = jnp.take_along_axis(si[:, :, 1], ci % PEER_TOPK, axis=-1)
        eidx = (i1 * PEER_NKEYS + i2).reshape(PEER_BLOCK, PEER_HEADS * PEER_TOPK)
        gw = jax.nn.softmax(cv, axis=-1).reshape(PEER_BLOCK, PEER_HEADS * PEER_TOPK)
        ue = jnp.take(u_tab, eidx, axis=0).astype(jnp.float32)
        ve = jnp.take(v_tab, eidx, axis=0).astype(jnp.float32)
        act = jnp.einsum('td,ted->te', xb, ue)
        return jnp.einsum('te,ted->td', gw * jax.nn.gelu(act), ve)

    return lax.map(block, hb).reshape(bsz, seq, dm)


def _encoder_layer(x, c, layer_idx, w_ada, b_ada, w_in, s5_lambda_re, s5_lambda_im, s5_log_dt,
                   s5_b_re, s5_b_im, s5_c_re, s5_c_im, s5_d, s5_w_glu, w_s5_out,
                   attn_lambda_q1, attn_lambda_k1, attn_lambda_q2, attn_lambda_k2, attn_subln_g,
                   w_attn_out, w_o, ln1_g, ln1_b, peer_w_q, peer_keys, peer_u, peer_v, ln2_g, ln2_b):
    f32 = jnp.float32
    dtype = x.dtype
    bsz, seq, _ = x.shape
    mod = jnp.dot(jax.nn.silu(c.astype(f32)), w_ada.astype(f32)) + b_ada.astype(f32)
    sh1, sc1, g1, sh2, sc2, g2 = jnp.split(mod[:, None, :], 6, axis=-1)

    h = _layernorm(x) * (1.0 + sc1) + sh1
    proj = jnp.dot(h, w_in)
    o1 = S5_WIDTH
    o2 = o1 + ATT_WIDTH
    o3 = o2 + ATT_WIDTH
    o4 = o3 + ATT_WIDTH
    o5 = o4 + D_MODEL
    u, q, k, v, gate_a, gate_b = jnp.split(proj, [o1, o2, o3, o4, o5], axis=-1)

    s = jax.nn.gelu(_s5_bidirectional(u, s5_lambda_re, s5_lambda_im, s5_log_dt,
                                      s5_b_re, s5_b_im, s5_c_re, s5_c_im, s5_d))
    s = s * jax.nn.sigmoid(jnp.dot(s, s5_w_glu))
    branch_a = jnp.dot(s, w_s5_out)

    pos = jnp.arange(seq, dtype=f32)
    q = _rope(q.astype(f32).reshape(bsz, seq, ATT_HEADS, 2, ATT_HEAD_DIM), pos)
    k = _rope(k.astype(f32).reshape(bsz, seq, ATT_HEADS, 2, ATT_HEAD_DIM), pos)
    v = v.astype(f32).reshape(bsz, seq, ATT_HEADS, 2 * ATT_HEAD_DIM)
    lam_init = 0.8 - 0.6 * math.exp(-0.3 * layer_idx)
    lam = (jnp.exp(jnp.sum(attn_lambda_q1.astype(f32) * attn_lambda_k1.astype(f32)))
           - jnp.exp(jnp.sum(attn_lambda_q2.astype(f32) * attn_lambda_k2.astype(f32))) + lam_init)
    o = _diff_attention(q, k, v, lam)
    o = o * lax.rsqrt(jnp.mean(o * o, axis=-1, keepdims=True) + LN_EPS)
    o = o * attn_subln_g.astype(f32) * (1.0 - lam_init)
    branch_b = jnp.dot(o.reshape(bsz, seq, ATT_WIDTH), w_attn_out)

    merged = jax.nn.sigmoid(gate_a) * branch_a + jax.nn.sigmoid(gate_b) * branch_b
    x1 = _layernorm(DN_ALPHA * x.astype(f32) + g1 * jnp.dot(merged, w_o), ln1_g, ln1_b)

    h2 = _layernorm(x1) * (1.0 + sc2) + sh2
    ffn = _peer(h2, peer_w_q, peer_keys, peer_u, peer_v)
    x2 = _layernorm(DN_ALPHA * x1 + g2 * ffn, ln2_g, ln2_b)
    return x2.astype(dtype)


def setup_inputs(seed: int = 0) -> dict:
    key = jax.random.key(seed)
    ks = jax.random.split(key, 32)
    f32 = jnp.float32

    def nrm(k, shape, scale):
        return jax.random.normal(k, shape, f32) * scale

    L = DEPTH
    G, P, C = S5_GROUPS, S5_STATE, S5_GROUP_CH
    n_idx = jnp.arange(P, dtype=f32)
    return {
        'x_prompt': nrm(ks[0], (BATCH, SEQ, D_MODEL), 1.0),
        'x_sample': nrm(ks[1], (DEC_BATCH, DEC_SEQ, D_MODEL), 1.0),
        'c_prompt': nrm(ks[2], (BATCH, D_MODEL), 1.0),
        'c_sample': nrm(ks[3], (DEC_BATCH, D_MODEL), 1.0),
        'w_ada': nrm(ks[4], (L, D_MODEL, 6 * D_MODEL), 0.3 * D_MODEL ** -0.5),
        'b_ada': nrm(ks[5], (L, 6 * D_MODEL), 0.01),
        'w_in': nrm(ks[6], (L, D_MODEL, N_IN), D_MODEL ** -0.5),
        's5_lambda_re': -0.5 + nrm(ks[7], (L, 2, G, P), 0.01),
        's5_lambda_im': math.pi * n_idx + nrm(ks[8], (L, 2, G, P), 0.01),
        's5_log_dt': jax.random.uniform(ks[9], (L, 2, G), f32, math.log(1e-3), math.log(1e-1)),
        's5_b_re': nrm(ks[10], (L, 2, G, P, C), (2 * C) ** -0.5),
        's5_b_im': nrm(ks[11], (L, 2, G, P, C), (2 * C) ** -0.5),
        's5_c_re': nrm(ks[12], (L, 2, G, C, P), (2 * P) ** -0.5),
        's5_c_im': nrm(ks[13], (L, 2, G, C, P), (2 * P) ** -0.5),
        's5_d': nrm(ks[14], (L, S5_WIDTH), 1.0),
        's5_w_glu': nrm(ks[15], (L, S5_WIDTH, S5_WIDTH), S5_WIDTH ** -0.5),
        'w_s5_out': nrm(ks[16], (L, S5_WIDTH, D_MODEL), S5_WIDTH ** -0.5),
        'attn_lambda_q1': nrm(ks[17], (L, ATT_HEAD_DIM), 0.1),
        'attn_lambda_k1': nrm(ks[18], (L, ATT_HEAD_DIM), 0.1),
        'attn_lambda_q2': nrm(ks[19], (L, ATT_HEAD_DIM), 0.1),
        'attn_lambda_k2': nrm(ks[20], (L, ATT_HEAD_DIM), 0.1),
        'attn_subln_g': 1.0 + nrm(ks[21], (L, 2 * ATT_HEAD_DIM), 0.01),
        'w_attn_out': nrm(ks[22], (L, ATT_WIDTH, D_MODEL), ATT_WIDTH ** -0.5),
        'w_o': nrm(ks[23], (L, D_MODEL, D_MODEL), DN_BETA * D_MODEL ** -0.5),
        'ln1_g': 1.0 + nrm(ks[24], (L, D_MODEL), 0.01),
        'ln1_b': nrm(ks[25], (L, D_MODEL), 0.01),
        'peer_w_q': nrm(ks[26], (L, D_MODEL, PEER_HEADS * PEER_QDIM), D_MODEL ** -0.5),
        'peer_keys': nrm(ks[27], (L, PEER_HEADS, 2, PEER_NKEYS, PEER_HALF), PEER_HALF ** -0.5),
        'peer_u': nrm(ks[28], (L, PEER_EXPERTS, D_MODEL), D_MODEL ** -0.5),
        'peer_v': nrm(ks[29], (L, PEER_EXPERTS, D_MODEL), DN_BETA),
        'ln2_g': 1.0 + nrm(ks[30], (L, D_MODEL), 0.01),
        'ln2_b': nrm(ks[31], (L, D_MODEL), 0.01),
    }


def reference(x_prompt, x_sample, c_prompt, c_sample, w_ada, b_ada, w_in, s5_lambda_re,
              s5_lambda_im, s5_log_dt, s5_b_re, s5_b_im, s5_c_re, s5_c_im, s5_d, s5_w_glu,
              w_s5_out, attn_lambda_q1, attn_lambda_k1, attn_lambda_q2, attn_lambda_k2,
              attn_subln_g, w_attn_out, w_o, ln1_g, ln1_b, peer_w_q, peer_keys, peer_u,
              peer_v, ln2_g, ln2_b):
    y_prompt = x_prompt
    y_sample = x_sample
    for l in range(DEPTH):
        layer = dict(w_ada=w_ada[l], b_ada=b_ada[l], w_in=w_in[l],
                     s5_lambda_re=s5_lambda_re[l], s5_lambda_im=s5_lambda_im[l],
                     s5_log_dt=s5_log_dt[l], s5_b_re=s5_b_re[l], s5_b_im=s5_b_im[l],
                     s5_c_re=s5_c_re[l], s5_c_im=s5_c_im[l], s5_d=s5_d[l],
                     s5_w_glu=s5_w_glu[l], w_s5_out=w_s5_out[l],
                     attn_lambda_q1=attn_lambda_q1[l], attn_lambda_k1=attn_lambda_k1[l],
                     attn_lambda_q2=attn_lambda_q2[l], attn_lambda_k2=attn_lambda_k2[l],
                     attn_subln_g=attn_subln_g[l], w_attn_out=w_attn_out[l], w_o=w_o[l],
                     ln1_g=ln1_g[l], ln1_b=ln1_b[l], peer_w_q=peer_w_q[l],
                     peer_keys=peer_keys[l], peer_u=peer_u[l], peer_v=peer_v[l],
                     ln2_g=ln2_g[l], ln2_b=ln2_b[l])
        y_prompt = _encoder_layer(y_prompt, c_prompt, l, **layer)
        y_sample = _encoder_layer(y_sample, c_sample, l, **layer)
    return (y_prompt, y_sample)
```

```python
import functools
import math

import jax
import jax.numpy as jnp
from jax import lax
from jax.experimental import pallas as pl
from jax.experimental.pallas import tpu as pltpu

F32 = jnp.float32
BF16 = jnp.bfloat16
HIGHEST = lax.Precision.HIGHEST

D_MODEL = 1024
DEPTH = 1
S5_WIDTH = D_MODEL // 2
S5_GROUP_CH = 16
S5_GROUPS = S5_WIDTH // S5_GROUP_CH
S5_STATE = 64
S5_CHUNK = 16
S5_ROW = S5_CHUNK * S5_GROUP_CH
ATT_HEADS = 8
ATT_HEAD_DIM = D_MODEL // (2 * ATT_HEADS)
ATT_WIDTH = ATT_HEADS * 2 * ATT_HEAD_DIM
ROPE_THETA = 10000.0
N_IN = S5_WIDTH + 3 * ATT_WIDTH + 2 * D_MODEL
PEER_HEADS = 8
PEER_NKEYS = 128
PEER_EXPERTS = PEER_NKEYS * PEER_NKEYS
PEER_TOPK = 16
PEER_QDIM = 256
PEER_HALF = PEER_QDIM // 2
PEER_NQ = PEER_HEADS * PEER_QDIM
DN_ALPHA = (2 * DEPTH) ** 0.25
LN_EPS = 1e-5

LANES = 128
VMEM_LIMIT_BYTES = 56 * 1024 * 1024
NEG_BIG = -3.0e38
POS_BIG = 3.0e38

IN_TILE = 512
ATT_TILE = 256
MERGE_TILE = 256
ROUTE_TILE = 128
EXPERT_TOK_TILE = 512
EXPERT_CHUNK = 512


def _cparams(sem):
    return pltpu.CompilerParams(dimension_semantics=sem, vmem_limit_bytes=VMEM_LIMIT_BYTES)


def _const_spec(shape):
    nd = len(shape)
    return pl.BlockSpec(shape, lambda *_: (0,) * nd, pipeline_mode=pl.Buffered(1))


def _ln(x):
    mu = jnp.mean(x, axis=-1, keepdims=True)
    xc = x - mu
    var = jnp.mean(xc * xc, axis=-1, keepdims=True)
    return xc * lax.rsqrt(var + LN_EPS)


def _gelu_tanh(x):
    return 0.5 * x * (1.0 + jnp.tanh(math.sqrt(2.0 / math.pi) * (x + 0.044715 * (x * x * x))))


def _sigmoid(x):
    return 1.0 / (1.0 + jnp.exp(-x))


def _ada_kernel(c_ref, w_ref, b_ref, o_ref):
    c = c_ref[...]
    s = c * _sigmoid(c)
    o_ref[...] = jnp.dot(s, w_ref[...], precision=HIGHEST, preferred_element_type=F32) + b_ref[...]


def _ada(c, w_ada, b_ada):
    bsz = c.shape[0]
    nblk = w_ada.shape[1] // D_MODEL
    return pl.pallas_call(
        _ada_kernel,
        grid=(nblk,),
        in_specs=[
            pl.BlockSpec((bsz, D_MODEL), lambda j: (0, 0)),
            pl.BlockSpec((D_MODEL, D_MODEL), lambda j: (0, j)),
            pl.BlockSpec((1, D_MODEL), lambda j: (0, j)),
        ],
        out_specs=pl.BlockSpec((bsz, D_MODEL), lambda j: (0, j)),
        out_shape=jax.ShapeDtypeStruct((bsz, nblk * D_MODEL), F32),
        compiler_params=_cparams(("arbitrary",)),
        name="ada",
    )(c, w_ada, b_ada.reshape(1, -1))


def _in_kernel(x_ref, mod_ref, w_ref, cos_ref, sa_ref, sb_ref,
               u_ref, q_ref, k_ref, v_ref, ga_ref, gb_ref):
    mod = mod_ref[...]
    h = _ln(x_ref[...]) * (1.0 + mod[1:2]) + mod[0:1]
    hb = h.astype(BF16)

    def proj(a, b):
        return jnp.dot(hb, w_ref[:, a:b], preferred_element_type=F32)

    o1 = S5_WIDTH
    o2 = o1 + ATT_WIDTH
    o3 = o2 + ATT_WIDTH
    o4 = o3 + ATT_WIDTH
    o5 = o4 + D_MODEL
    u_ref[...] = proj(0, o1)
    cos, sa, sb = cos_ref[...], sa_ref[...], sb_ref[...]
    half = ATT_HEAD_DIM // 2

    def rope_store(t, o_ref, scale):
        for j in range(ATT_WIDTH // LANES):
            tj = t[:, j * LANES:(j + 1) * LANES]
            r = (tj * cos + pltpu.roll(tj, LANES - half, 1) * sa + pltpu.roll(tj, half, 1) * sb)
            o_ref[:, j * LANES:(j + 1) * LANES] = (r * scale).astype(BF16)

    rope_store(proj(o1, o2), q_ref, ATT_HEAD_DIM ** -0.5)
    rope_store(proj(o2, o3), k_ref, 1.0)
    v_ref[...] = proj(o3, o4).astype(BF16)
    ga_ref[...] = _sigmoid(proj(o4, o5)).astype(BF16)
    gb_ref[...] = _sigmoid(proj(o5, N_IN)).astype(BF16)


def _inproj(x, mod3, w_in_b, cos, sa, sb):
    bsz, seq, _ = x.shape
    tm = min(IN_TILE, seq)
    tok = lambda n: pl.BlockSpec((None, tm, n), lambda b, i: (b, i, 0))
    rope = pl.BlockSpec((tm, LANES), lambda b, i: (i, 0))
    outs = [(S5_WIDTH, F32), (ATT_WIDTH, BF16), (ATT_WIDTH, BF16), (ATT_WIDTH, BF16),
            (D_MODEL, BF16), (D_MODEL, BF16)]
    return pl.pallas_call(
        _in_kernel,
        grid=(bsz, seq // tm),
        in_specs=[tok(D_MODEL),
                  pl.BlockSpec((None, 6, D_MODEL), lambda b, i: (b, 0, 0)),
                  _const_spec((D_MODEL, N_IN)), rope, rope, rope],
        out_specs=[tok(n) for n, _ in outs],
        out_shape=[jax.ShapeDtypeStruct((bsz, seq, n), dt) for n, dt in outs],
        compiler_params=_cparams(("parallel", "parallel")),
        name="inproj",
    )(x, mod3, w_in_b, cos, sa, sb)


def _s5_weights(lam_re, lam_im, log_dt, b_re, b_im, c_re, c_im):
    n = S5_CHUNK
    per_dir = []
    for d in range(2):
        lr = lam_re[d].astype(F32)
        li = lam_im[d].astype(F32)
        dt = jnp.exp(log_dt[d].astype(F32))[:, None]
        mag = jnp.exp(lr * dt)
        lbr = mag * jnp.cos(li * dt)
        lbi = mag * jnp.sin(li * dt)
        den = lr * lr + li * li
        nr = lbr - 1.0
        cr = (nr * lr + lbi * li) / den
        ci = (lbi * lr - nr * li) / den
        br = b_re[d].astype(F32)
        bi = b_im[d].astype(F32)
        bbr = cr[..., None] * br - ci[..., None] * bi
        bbi = cr[..., None] * bi + ci[..., None] * br
        pr = [jnp.ones_like(lbr)]
        pi = [jnp.zeros_like(lbi)]
        for _ in range(n):
            pr.append(pr[-1] * lbr - pi[-1] * lbi)
            pi.append(pr[-2] * lbi + pi[-1] * lbr)
        pr = jnp.stack(pr)
        pi = jnp.stack(pi)
        cre = c_re[d].astype(F32)
        cim = c_im[d].astype(F32)
        cpr = cre[None] * pr[:, :, None, :] - cim[None] * pi[:, :, None, :]
        cpi = cre[None] * pi[:, :, None, :] + cim[None] * pr[:, :, None, :]
        kern = (jnp.einsum('ngcp,gpd->ngcd', cpr, bbr, precision=HIGHEST)
                - jnp.einsum('ngcp,gpd->ngcd', cpi, bbi, precision=HIGHEST))
        pbr = pr[..., None] * bbr[None] - pi[..., None] * bbi[None]
        pbi = pr[..., None] * bbi[None] + pi[..., None] * bbr[None]
        per_dir.append(dict(pr=pr, pi=pi, cpr=cpr, cpi=cpi, kern=kern, pbr=pbr, pbi=pbi))

    g, c, p = S5_GROUPS, S5_GROUP_CH, S5_STATE
    s_idx = jnp.arange(n)[:, None]
    t_idx = jnp.arange(n)[None, :]
    tau_f = t_idx - s_idx
    tau_b = s_idx - t_idx
    kf = per_dir[0]['kern'][jnp.clip(tau_f, 0, n)] * (tau_f >= 0)[..., None, None, None]
    kb = per_dir[1]['kern'][jnp.clip(tau_b, 0, n)] * (tau_b >= 0)[..., None, None, None]
    m = (kf + kb).transpose(2, 0, 4, 1, 3).reshape(g, S5_ROW, S5_ROW)

    pow_f = n - 1 - jnp.arange(n)
    pow_b = jnp.arange(n)

    def ws_part(key, d, pows):
        return per_dir[d][key][pows].transpose(1, 0, 3, 2).reshape(g, S5_ROW, p)

    ws = jnp.concatenate([ws_part('pbr', 0, pow_f), ws_part('pbr', 1, pow_b),
                          ws_part('pbi', 0, pow_f), ws_part('pbi', 1, pow_b)], axis=-1)

    out_f = jnp.arange(n) + 1
    out_b = n - jnp.arange(n)

    def wo_part(key, d, pows):
        return per_dir[d][key][pows].transpose(1, 3, 0, 2).reshape(g, p, S5_ROW)

    zero = jnp.zeros((g, p, S5_ROW), F32)
    wof = jnp.concatenate([wo_part('cpr', 0, out_f), zero, -wo_part('cpi', 0, out_f), zero], axis=1)
    wob = jnp.concatenate([zero, wo_part('cpr', 1, out_b), zero, -wo_part('cpi', 1, out_b)], axis=1)
    ar = jnp.concatenate([per_dir[0]['pr'][n], per_dir[1]['pr'][n]], axis=-1)[:, None, :]
    ai = jnp.concatenate([per_dir[0]['pi'][n], per_dir[1]['pi'][n]], axis=-1)[:, None, :]
    return m.astype(BF16), ws.astype(BF16), wof.astype(BF16), wob.astype(BF16), ar, ai


def _s5_kernel(u_ref, m_ref, ws_ref, wof_ref, wob_ref, ar_ref, ai_ref, y_ref,
               s_ref, hf_ref, hb_ref, *, nb, nchunk):
    u = u_ref[...]
    s_ref[...] = jnp.dot(u, ws_ref[...], preferred_element_type=F32)
    ar = ar_ref[...]
    ai = ai_ref[...]
    is_fwd = lax.broadcasted_iota(jnp.int32, (nb, LANES), 1) < S5_STATE

    def step(i, carry):
        h_re, h_im = carry
        rf = pl.multiple_of(i * nb, 8)
        rb = pl.multiple_of((nchunk - 1 - i) * nb, 8)
        hf_ref[pl.ds(rf, nb), 0:LANES] = h_re
        hf_ref[pl.ds(rf, nb), LANES:2 * LANES] = h_im
        hb_ref[pl.ds(rb, nb), 0:LANES] = h_re
        hb_ref[pl.ds(rb, nb), LANES:2 * LANES] = h_im
        s_re = jnp.where(is_fwd, s_ref[pl.ds(rf, nb), 0:LANES], s_ref[pl.ds(rb, nb), 0:LANES])
        s_im = jnp.where(is_fwd, s_ref[pl.ds(rf, nb), LANES:2 * LANES],
                         s_ref[pl.ds(rb, nb), LANES:2 * LANES])
        return (ar * h_re - ai * h_im + s_re, ar * h_im + ai * h_re + s_im)

    zero = jnp.zeros((nb, LANES), F32)
    lax.fori_loop(0, nchunk, step, (zero, zero))
    y = jnp.dot(u, m_ref[...], preferred_element_type=F32)
    y = y + jnp.dot(hf_ref[...].astype(BF16), wof_ref[...], preferred_element_type=F32)
    y = y + jnp.dot(hb_ref[...].astype(BF16), wob_ref[...], preferred_element_type=F32)
    y_ref[...] = y


def _s5(u, s5w):
    m, ws, wof, wob, ar, ai = s5w
    bsz, seq, _ = u.shape
    nchunk = seq // S5_CHUNK
    rows = nchunk * bsz
    u2 = (u.reshape(bsz, nchunk, S5_CHUNK, S5_GROUPS, S5_GROUP_CH)
          .transpose(3, 1, 0, 2, 4).reshape(S5_GROUPS, rows, S5_ROW).astype(BF16))
    grp = lambda r, c: pl.BlockSpec((None, r, c), lambda g: (g, 0, 0))
    y2 = pl.pallas_call(
        functools.partial(_s5_kernel, nb=bsz, nchunk=nchunk),
        grid=(S5_GROUPS,),
        in_specs=[grp(rows, S5_ROW), grp(S5_ROW, S5_ROW), grp(S5_ROW, S5_ROW),
                  grp(S5_ROW, S5_ROW), grp(S5_ROW, S5_ROW), grp(1, LANES), grp(1, LANES)],
        out_specs=grp(rows, S5_ROW),
        out_shape=jax.ShapeDtypeStruct((S5_GROUPS, rows, S5_ROW), F32),
        scratch_shapes=[pltpu.VMEM((rows, S5_ROW), F32)] * 3,
        compiler_params=_cparams(("parallel",)),
        name="s5",
    )(u2, m, ws, wof, wob, ar, ai)
    return (y2.reshape(S5_GROUPS, nchunk, bsz, S5_CHUNK, S5_GROUP_CH)
            .transpose(2, 1, 3, 0, 4).reshape(bsz, seq, S5_WIDTH))


def _attn_kernel(lam_ref, q_ref, k_ref, v_ref, g_ref, o_ref):
    q = q_ref[...]
    k = k_ref[...]
    v = v_ref[...]
    first = lax.broadcasted_iota(jnp.int32, q.shape, 1) < ATT_HEAD_DIM
    zero = jnp.zeros_like(q)

    def softmax_pv(qm):
        s = lax.dot_general(qm, k, (((1,), (1,)), ((), ())), preferred_element_type=F32)
        e = jnp.exp(s - jnp.max(s, axis=-1, keepdims=True))
        l = jnp.sum(e, axis=-1, keepdims=True)
        return jnp.dot(e.astype(BF16), v, preferred_element_type=F32) / l

    o = softmax_pv(jnp.where(first, q, zero)) - lam_ref[0] * softmax_pv(jnp.where(first, zero, q))
    o = o * lax.rsqrt(jnp.mean(o * o, axis=-1, keepdims=True) + LN_EPS) * g_ref[...]
    o_ref[...] = o.astype(BF16)


def _attention(lam, q, k, v, g_row):
    bsz, seq, _ = q.shape
    tq = min(ATT_TILE, seq)
    return pl.pallas_call(
        _attn_kernel,
        grid=(bsz, ATT_HEADS, seq // tq),
        in_specs=[pl.BlockSpec(memory_space=pltpu.SMEM),
                  pl.BlockSpec((None, tq, LANES), lambda b, h, i: (b, i, h)),
                  pl.BlockSpec((None, seq, LANES), lambda b, h, i: (b, 0, h)),
                  pl.BlockSpec((None, seq, LANES), lambda b, h, i: (b, 0, h)),
                  pl.BlockSpec((1, LANES), lambda b, h, i: (0, 0))],
        out_specs=pl.BlockSpec((None, tq, LANES), lambda b, h, i: (b, i, h)),
        out_shape=jax.ShapeDtypeStruct((bsz, seq, ATT_WIDTH), BF16),
        compiler_params=_cparams(("parallel", "parallel", "arbitrary")),
        name="attn",
    )(lam, q, k, v, g_row)


def _merge_kernel(x_ref, u_ref, y_ref, o_ref, ga_ref, gb_ref, mod_ref, d_ref, wglu_ref, ws5_ref,
                  watt_ref, wo_ref, ln1g_ref, ln1b_ref, wq_ref, x1_ref, h2_ref, pq_ref):
    mod = mod_ref[...]
    s = _gelu_tanh(u_ref[...] * d_ref[...] + y_ref[...])
    s = s * _sigmoid(jnp.dot(s.astype(BF16), wglu_ref[...], preferred_element_type=F32))
    br_a = jnp.dot(s.astype(BF16), ws5_ref[...], preferred_element_type=F32)
    br_b = jnp.dot(o_ref[...], watt_ref[...], preferred_element_type=F32)
    merged = ga_ref[...].astype(F32) * br_a + gb_ref[...].astype(F32) * br_b
    z = jnp.dot(merged.astype(BF16), wo_ref[...], preferred_element_type=F32)
    x1 = _ln(DN_ALPHA * x_ref[...] + mod[2:3] * z) * ln1g_ref[...] + ln1b_ref[...]
    x1_ref[...] = x1
    h2 = (_ln(x1) * (1.0 + mod[4:5]) + mod[3:4]).astype(BF16)
    h2_ref[...] = h2
    pq_ref[...] = jnp.dot(h2, wq_ref[...], preferred_element_type=F32).astype(BF16)


def _merge(x, u, y, o, ga, gb, mod3, wts):
    bsz, seq, _ = x.shape
    tm = min(MERGE_TILE, seq)
    tok = lambda n: pl.BlockSpec((None, tm, n), lambda b, i: (b, i, 0))
    row = lambda n: _const_spec((1, n))
    return pl.pallas_call(
        _merge_kernel,
        grid=(bsz, seq // tm),
        in_specs=[tok(D_MODEL), tok(S5_WIDTH), tok(S5_WIDTH), tok(ATT_WIDTH), tok(D_MODEL), tok(D_MODEL),
                  pl.BlockSpec((None, 6, D_MODEL), lambda b, i: (b, 0, 0)),
                  row(S5_WIDTH), _const_spec((S5_WIDTH, S5_WIDTH)), _const_spec((S5_WIDTH, D_MODEL)),
                  _const_spec((ATT_WIDTH, D_MODEL)), _const_spec((D_MODEL, D_MODEL)),
                  row(D_MODEL), row(D_MODEL), _const_spec((D_MODEL, PEER_NQ))],
        out_specs=[tok(D_MODEL), tok(D_MODEL), tok(PEER_NQ)],
        out_shape=[jax.ShapeDtypeStruct((bsz, seq, D_MODEL), F32),
                   jax.ShapeDtypeStruct((bsz, seq, D_MODEL), BF16),
                   jax.ShapeDtypeStruct((bsz, seq, PEER_NQ), BF16)],
        compiler_params=_cparams(("parallel", "parallel")),
        name="merge",
    )(x, u, y, o, ga, gb, mod3, wts['s5_d'], wts['w_glu'], wts['w_s5_out'], wts['w_attn_out'],
      wts['w_o'], wts['ln1_g'], wts['ln1_b'], wts['w_q'])


def _route_kernel(pq_ref, keys_ref, w_ref, a_sc, b_sc, th_sc, wt_sc, ix_sc, wr_sc, ir_sc, s2_sc, e2_sc):
    tn = pq_ref.shape[0]
    sub_f = lax.broadcasted_iota(jnp.int32, (PEER_NKEYS, tn), 0).astype(F32)
    row8 = lax.broadcasted_iota(jnp.int32, (8, tn), 0)
    nt_dims = (((1,), (1,)), ((), ()))

    for h in range(PEER_HEADS):
        q1 = pq_ref[:, (2 * h) * PEER_HALF:(2 * h + 1) * PEER_HALF]
        q2 = pq_ref[:, (2 * h + 1) * PEER_HALF:(2 * h + 2) * PEER_HALF]
        st1 = lax.dot_general(keys_ref[2 * h], q1, nt_dims, preferred_element_type=F32)
        st2 = lax.dot_general(keys_ref[2 * h + 1], q2, nt_dims, preferred_element_type=F32)
        s2 = st2.T
        s2_sc[h] = s2
        e2_sc[h] = jnp.exp(s2 - jnp.max(s2, axis=-1, keepdims=True))

        a_vals = []
        for k in range(PEER_TOPK):
            m = jnp.max(st1, axis=0, keepdims=True)
            ix = jnp.min(jnp.where(st1 == m, sub_f, float(PEER_NKEYS)), axis=0, keepdims=True)
            st1 = jnp.where(sub_f == ix, NEG_BIG, st1)
            a_vals.append(m)
            ix_sc[h * PEER_TOPK + k:h * PEER_TOPK + k + 1, :] = ix
        for k in range(PEER_TOPK):
            m = jnp.max(st2, axis=0, keepdims=True)
            st2 = jnp.where(st2 == m, NEG_BIG, st2)
            b_sc[k:k + 1, :] = m
        b_lo = b_sc[0:8, :]
        b_hi = b_sc[8:16, :]

        cands = [a_vals[0] + b_lo, a_vals[0] + b_hi]
        for k in range(1, PEER_TOPK):
            cands.append(jnp.where(row8 < PEER_TOPK // (k + 1), a_vals[k] + b_lo, NEG_BIG))
        m0 = a_vals[0] + b_sc[0:1, :]
        z = jnp.zeros_like(m0)
        tau = m0
        for _ in range(PEER_TOPK):
            tau = jnp.max(functools.reduce(jnp.maximum, cands), axis=0, keepdims=True)
            cands = [jnp.where(c == tau, NEG_BIG, c) for c in cands]
            z = z + jnp.exp(tau - m0)
        inv_z = 1.0 / z
        for k in range(PEER_TOPK):
            th = jnp.where(a_vals[k] + b_lo >= tau, b_lo, POS_BIG)
            if PEER_TOPK // (k + 1) > 8:
                th = jnp.minimum(th, jnp.where(a_vals[k] + b_hi >= tau, b_hi, POS_BIG))
            j = h * PEER_TOPK + k
            th_sc[j:j + 1, :] = jnp.min(th, axis=0, keepdims=True)
            wt_sc[j:j + 1, :] = jnp.exp(a_vals[k] - a_vals[0]) * inv_z

    wr_sc[...] = wt_sc[...].T
    ir_sc[...] = ix_sc[...].T
    lane_i = lax.broadcasted_iota(jnp.int32, (PEER_NKEYS, tn), 1)
    sub_j = lax.broadcasted_iota(jnp.int32, (PEER_NKEYS, PEER_NKEYS), 0).astype(F32)

    def token(t, carry):
        th_col = jnp.sum(jnp.where(lane_i == t, th_sc[...], 0.0), axis=1, keepdims=True)
        s2b = jnp.concatenate([jnp.broadcast_to(s2_sc[h, pl.ds(t, 1), :], (PEER_TOPK, PEER_NKEYS))
                               for h in range(PEER_HEADS)], axis=0)
        e2b = jnp.concatenate([jnp.broadcast_to(e2_sc[h, pl.ds(t, 1), :], (PEER_TOPK, PEER_NKEYS))
                               for h in range(PEER_HEADS)], axis=0)
        r = jnp.where(s2b >= th_col, e2b, 0.0).astype(BF16)
        pt = jnp.where(sub_j == ir_sc[pl.ds(t, 1), :], wr_sc[pl.ds(t, 1), :], 0.0).astype(BF16)
        w_ref[t] = jnp.dot(pt, r, preferred_element_type=F32).astype(BF16)
        return carry

    lax.fori_loop(0, tn, token, 0)


def _route(pq, keys_b):
    ntok = pq.shape[0]
    tn = ROUTE_TILE
    nj = PEER_HEADS * PEER_TOPK
    return pl.pallas_call(
        _route_kernel,
        grid=(ntok // tn,),
        in_specs=[pl.BlockSpec((tn, PEER_NQ), lambda i: (i, 0)),
                  _const_spec((2 * PEER_HEADS, PEER_NKEYS, PEER_HALF))],
        out_specs=pl.BlockSpec((tn, PEER_NKEYS, PEER_NKEYS), lambda i: (i, 0, 0)),
        out_shape=jax.ShapeDtypeStruct((ntok, PEER_NKEYS, PEER_NKEYS), BF16),
        scratch_shapes=[pltpu.VMEM((PEER_TOPK, tn), F32), pltpu.VMEM((PEER_TOPK, tn), F32),
                        pltpu.VMEM((nj, tn), F32), pltpu.VMEM((nj, tn), F32), pltpu.VMEM((nj, tn), F32),
                        pltpu.VMEM((tn, nj), F32), pltpu.VMEM((tn, nj), F32),
                        pltpu.VMEM((PEER_HEADS, tn, PEER_NKEYS), F32),
                        pltpu.VMEM((PEER_HEADS, tn, PEER_NKEYS), F32)],
        compiler_params=_cparams(("parallel",)),
        name="route",
    )(pq, keys_b)


def _expert_kernel(h_ref, ut_ref, v_ref, w_ref, x1_ref, mod_ref, g_ref, b_ref, o_ref, acc_ref):
    e = pl.program_id(2)

    @pl.when(e == 0)
    def _():
        acc_ref[...] = jnp.zeros_like(acc_ref)

    act = jnp.dot(h_ref[...], ut_ref[...], preferred_element_type=F32)
    gated = (_gelu_tanh(act) * w_ref[...].astype(F32)).astype(BF16)
    acc_ref[...] += jnp.dot(gated, v_ref[...], preferred_element_type=F32)

    @pl.when(e == pl.num_programs(2) - 1)
    def _():
        mod = mod_ref[...]
        o_ref[...] = _ln(DN_ALPHA * x1_ref[...] + mod[5:6] * acc_ref[...]) * g_ref[...] + b_ref[...]


def _experts(h2, w2d, x1, mod3, ut_b, v_b, ln2_g, ln2_b):
    bsz, seq, _ = x1.shape
    tt = min(EXPERT_TOK_TILE, seq)
    ce = EXPERT_CHUNK
    tok = lambda n: pl.BlockSpec((None, tt, n), lambda b, i, e: (b, i, 0))
    return pl.pallas_call(
        _expert_kernel,
        grid=(bsz, seq // tt, PEER_EXPERTS // ce),
        in_specs=[tok(D_MODEL),
                  pl.BlockSpec((D_MODEL, ce), lambda b, i, e: (0, e)),
                  pl.BlockSpec((ce, D_MODEL), lambda b, i, e: (e, 0)),
                  pl.BlockSpec((None, tt, ce), lambda b, i, e: (b, i, e)),
                  tok(D_MODEL),
                  pl.BlockSpec((None, 6, D_MODEL), lambda b, i, e: (b, 0, 0)),
                  pl.BlockSpec((1, D_MODEL), lambda b, i, e: (0, 0)),
                  pl.BlockSpec((1, D_MODEL), lambda b, i, e: (0, 0))],
        out_specs=tok(D_MODEL),
        out_shape=jax.ShapeDtypeStruct((bsz, seq, D_MODEL), F32),
        scratch_shapes=[pltpu.VMEM((tt, D_MODEL), F32)],
        compiler_params=_cparams(("parallel", "parallel", "arbitrary")),
        name="expert",
    )(h2, ut_b, v_b, w2d, x1, mod3, ln2_g, ln2_b)


def _rope_tables(seq):
    half = ATT_HEAD_DIM // 2
    inv_freq = ROPE_THETA ** (-jnp.arange(half, dtype=F32) * 2.0 / ATT_HEAD_DIM)
    ang = jnp.arange(seq, dtype=F32)[:, None] * inv_freq[None, :]
    cos = jnp.cos(ang)
    sin = jnp.sin(ang)
    zero = jnp.zeros_like(sin)
    reps = LANES // ATT_HEAD_DIM
    cos_t = jnp.tile(jnp.concatenate([cos, cos], axis=1), (1, reps))
    sa_t = jnp.tile(jnp.concatenate([-sin, zero], axis=1), (1, reps))
    sb_t = jnp.tile(jnp.concatenate([zero, sin], axis=1), (1, reps))
    return cos_t, sa_t, sb_t


def _prepare(layer_idx, p):
    lam_init = 0.8 - 0.6 * math.exp(-0.3 * layer_idx)
    lam = (jnp.exp(jnp.sum(p['attn_lambda_q1'].astype(F32) * p['attn_lambda_k1'].astype(F32)))
           - jnp.exp(jnp.sum(p['attn_lambda_q2'].astype(F32) * p['attn_lambda_k2'].astype(F32))) + lam_init)
    row = lambda a: a.astype(F32).reshape(1, -1)
    return dict(
        w_ada=p['w_ada'].astype(F32), b_ada=p['b_ada'].astype(F32),
        w_in=p['w_in'].astype(BF16),
        s5w=_s5_weights(p['s5_lambda_re'], p['s5_lambda_im'], p['s5_log_dt'], p['s5_b_re'],
                        p['s5_b_im'], p['s5_c_re'], p['s5_c_im']),
        s5_d=row(p['s5_d']), w_glu=p['s5_w_glu'].astype(BF16), w_s5_out=p['w_s5_out'].astype(BF16),
        lam=lam.reshape(1).astype(F32),
        subln=row(p['attn_subln_g']) * (1.0 - lam_init),
        w_attn_out=p['w_attn_out'].astype(BF16), w_o=p['w_o'].astype(BF16),
        ln1_g=row(p['ln1_g']), ln1_b=row(p['ln1_b']),
        w_q=p['peer_w_q'].astype(BF16),
        keys=p['peer_keys'].astype(BF16).reshape(2 * PEER_HEADS, PEER_NKEYS, PEER_HALF),
        ut=p['peer_u'].astype(BF16).T, v=p['peer_v'].astype(BF16),
        ln2_g=row(p['ln2_g']), ln2_b=row(p['ln2_b']),
    )


def _encoder_layer(x, c, wts, rope):
    bsz, seq, _ = x.shape
    mod3 = _ada(c.astype(F32), wts['w_ada'], wts['b_ada']).reshape(bsz, 6, D_MODEL)
    u, q, k, v, ga, gb = _inproj(x.astype(F32), mod3, wts['w_in'], *rope)
    y = _s5(u, wts['s5w'])
    o = _attention(wts['lam'], q, k, v, wts['subln'])
    x1, h2, pq = _merge(x.astype(F32), u, y, o, ga, gb, mod3, wts)
    w = _route(pq.reshape(bsz * seq, PEER_NQ), wts['keys'])
    w2d = w.reshape(bsz, seq, PEER_EXPERTS)
    x2 = _experts(h2, w2d, x1, mod3, wts['ut'], wts['v'], wts['ln2_g'], wts['ln2_b'])
    return x2.astype(x.dtype)


def kernel(x_prompt, x_sample, c_prompt, c_sample, w_ada, b_ada, w_in, s5_lambda_re, s5_lambda_im, s5_log_dt, s5_b_re, s5_b_im, s5_c_re, s5_c_im, s5_d, s5_w_glu, w_s5_out, attn_lambda_q1, attn_lambda_k1, attn_lambda_q2, attn_lambda_k2, attn_subln_g, w_attn_out, w_o, ln1_g, ln1_b, peer_w_q, peer_keys, peer_u, peer_v, ln2_g, ln2_b):
    params = dict(w_ada=w_ada, b_ada=b_ada, w_in=w_in, s5_lambda_re=s5_lambda_re,
                  s5_lambda_im=s5_lambda_im, s5_log_dt=s5_log_dt, s5_b_re=s5_b_re, s5_b_im=s5_b_im,
                  s5_c_re=s5_c_re, s5_c_im=s5_c_im, s5_d=s5_d, s5_w_glu=s5_w_glu, w_s5_out=w_s5_out,
                  attn_lambda_q1=attn_lambda_q1, attn_lambda_k1=attn_lambda_k1,
                  attn_lambda_q2=attn_lambda_q2, attn_lambda_k2=attn_lambda_k2,
                  attn_subln_g=attn_subln_g, w_attn_out=w_attn_out, w_o=w_o, ln1_g=ln1_g, ln1_b=ln1_b,
                  peer_w_q=peer_w_q, peer_keys=peer_keys, peer_u=peer_u, peer_v=peer_v,
                  ln2_g=ln2_g, ln2_b=ln2_b)
    y_prompt, y_sample = x_prompt, x_sample
    rope_p = _rope_tables(x_prompt.shape[1])
    rope_s = _rope_tables(x_sample.shape[1])
    for l in range(DEPTH):
        wts = _prepare(l, {name: a[l] for name, a in params.items()})
        y_prompt = _encoder_layer(y_prompt, c_prompt, wts, rope_p)
        y_sample = _encoder_layer(y_sample, c_sample, wts, rope_s)
    return (y_prompt, y_sample)
```

```python
import functools
import math

import jax
import jax.numpy as jnp
from jax import lax
from jax.experimental import pallas as pl
from jax.experimental.pallas import tpu as pltpu

F32 = jnp.float32
BF16 = jnp.bfloat16
HIGHEST = lax.Precision.HIGHEST

D_MODEL = 1024
DEPTH = 1
S5_WIDTH = D_MODEL // 2
S5_GROUP_CH = 16
S5_GROUPS = S5_WIDTH // S5_GROUP_CH
S5_STATE = 64
S5_CHUNK = 16
S5_ROW = S5_CHUNK * S5_GROUP_CH
ATT_HEADS = 8
ATT_HEAD_DIM = D_MODEL // (2 * ATT_HEADS)
ATT_WIDTH = ATT_HEADS * 2 * ATT_HEAD_DIM
ROPE_THETA = 10000.0
N_IN = S5_WIDTH + 3 * ATT_WIDTH + 2 * D_MODEL
PEER_HEADS = 8
PEER_NKEYS = 128
PEER_EXPERTS = PEER_NKEYS * PEER_NKEYS
PEER_TOPK = 16
PEER_QDIM = 256
PEER_HALF = PEER_QDIM // 2
PEER_NQ = PEER_HEADS * PEER_QDIM
DN_ALPHA = (2 * DEPTH) ** 0.25
LN_EPS = 1e-5

LANES = 128
VMEM_LIMIT_BYTES = 56 * 1024 * 1024
NEG_BIG = -3.0e38
POS_BIG = 3.0e38

IN_TILE = 512
ATT_TILE = 256
MERGE_TILE = 256
ROUTE_TILE = 128
EXPERT_TOK_TILE = 512
EXPERT_CHUNK = 512


def _cparams(sem):
    return pltpu.CompilerParams(dimension_semantics=sem, vmem_limit_bytes=VMEM_LIMIT_BYTES)


def _const_spec(shape):
    nd = len(shape)
    return pl.BlockSpec(shape, lambda *_: (0,) * nd, pipeline_mode=pl.Buffered(1))


def _ln(x):
    mu = jnp.mean(x, axis=-1, keepdims=True)
    xc = x - mu
    var = jnp.mean(xc * xc, axis=-1, keepdims=True)
    return xc * lax.rsqrt(var + LN_EPS)


def _gelu_tanh(x):
    return 0.5 * x * (1.0 + jnp.tanh(math.sqrt(2.0 / math.pi) * (x + 0.044715 * (x * x * x))))


def _sigmoid(x):
    return 1.0 / (1.0 + jnp.exp(-x))


def _ada_kernel(c_ref, w_ref, b_ref, o_ref):
    c = c_ref[...]
    s = c * _sigmoid(c)
    o_ref[...] = jnp.dot(s, w_ref[...], precision=HIGHEST, preferred_element_type=F32) + b_ref[...]


def _ada(c, w_ada, b_ada):
    bsz = c.shape[0]
    nblk = w_ada.shape[1] // D_MODEL
    return pl.pallas_call(
        _ada_kernel,
        grid=(nblk,),
        in_specs=[
            pl.BlockSpec((bsz, D_MODEL), lambda j: (0, 0)),
            pl.BlockSpec((D_MODEL, D_MODEL), lambda j: (0, j)),
            pl.BlockSpec((1, D_MODEL), lambda j: (0, j)),
        ],
        out_specs=pl.BlockSpec((bsz, D_MODEL), lambda j: (0, j)),
        out_shape=jax.ShapeDtypeStruct((bsz, nblk * D_MODEL), F32),
        compiler_params=_cparams(("arbitrary",)),
        name="ada",
    )(c, w_ada, b_ada.reshape(1, -1))


def _in_kernel(x_ref, mod_ref, w_ref, cos_ref, sa_ref, sb_ref,
               u_ref, q_ref, k_ref, v_ref, ga_ref, gb_ref):
    mod = mod_ref[...]
    h = _ln(x_ref[...]) * (1.0 + mod[1:2]) + mod[0:1]
    hb = h.astype(BF16)

    def proj(a, b):
        return jnp.dot(hb, w_ref[:, a:b], preferred_element_type=F32)

    o1 = S5_WIDTH
    o2 = o1 + ATT_WIDTH
    o3 = o2 + ATT_WIDTH
    o4 = o3 + ATT_WIDTH
    o5 = o4 + D_MODEL
    u_ref[...] = proj(0, o1)
    cos, sa, sb = cos_ref[...], sa_ref[...], sb_ref[...]
    half = ATT_HEAD_DIM // 2

    def rope_store(t, o_ref, scale):
        for j in range(ATT_WIDTH // LANES):
            tj = t[:, j * LANES:(j + 1) * LANES]
            r = (tj * cos + pltpu.roll(tj, LANES - half, 1) * sa + pltpu.roll(tj, half, 1) * sb)
            o_ref[:, j * LANES:(j + 1) * LANES] = (r * scale).astype(BF16)

    rope_store(proj(o1, o2), q_ref, ATT_HEAD_DIM ** -0.5)
    rope_store(proj(o2, o3), k_ref, 1.0)
    v_ref[...] = proj(o3, o4).astype(BF16)
    ga_ref[...] = _sigmoid(proj(o4, o5)).astype(BF16)
    gb_ref[...] = _sigmoid(proj(o5, N_IN)).astype(BF16)


def _inproj(x, mod3, w_in_b, cos, sa, sb):
    bsz, seq, _ = x.shape
    tm = min(IN_TILE, seq)
    tok = lambda n: pl.BlockSpec((None, tm, n), lambda b, i: (b, i, 0))
    rope = pl.BlockSpec((tm, LANES), lambda b, i: (i, 0))
    outs = [(S5_WIDTH, F32), (ATT_WIDTH, BF16), (ATT_WIDTH, BF16), (ATT_WIDTH, BF16),
            (D_MODEL, BF16), (D_MODEL, BF16)]
    return pl.pallas_call(
        _in_kernel,
        grid=(bsz, seq // tm),
        in_specs=[tok(D_MODEL),
                  pl.BlockSpec((None, 6, D_MODEL), lambda b, i: (b, 0, 0)),
                  _const_spec((D_MODEL, N_IN)), rope, rope, rope],
        out_specs=[tok(n) for n, _ in outs],
        out_shape=[jax.ShapeDtypeStruct((bsz, seq, n), dt) for n, dt in outs],
        compiler_params=_cparams(("parallel", "parallel")),
        name="inproj",
    )(x, mod3, w_in_b, cos, sa, sb)


def _s5_weights(lam_re, lam_im, log_dt, b_re, b_im, c_re, c_im):
    n = S5_CHUNK
    per_dir = []
    for d in range(2):
        lr = lam_re[d].astype(F32)
        li = lam_im[d].astype(F32)
        dt = jnp.exp(log_dt[d].astype(F32))[:, None]
        mag = jnp.exp(lr * dt)
        lbr = mag * jnp.cos(li * dt)
        lbi = mag * jnp.sin(li * dt)
        den = lr * lr + li * li
        nr = lbr - 1.0
        cr = (nr * lr + lbi * li) / den
        ci = (lbi * lr - nr * li) / den
        br = b_re[d].astype(F32)
        bi = b_im[d].astype(F32)
        bbr = cr[..., None] * br - ci[..., None] * bi
        bbi = cr[..., None] * bi + ci[..., None] * br
        pr = [jnp.ones_like(lbr)]
        pi = [jnp.zeros_like(lbi)]
        for _ in range(n):
            pr.append(pr[-1] * lbr - pi[-1] * lbi)
            pi.append(pr[-2] * lbi + pi[-1] * lbr)
        pr = jnp.stack(pr)
        pi = jnp.stack(pi)
        cre = c_re[d].astype(F32)
        cim = c_im[d].astype(F32)
        cpr = cre[None] * pr[:, :, None, :] - cim[None] * pi[:, :, None, :]
        cpi = cre[None] * pi[:, :, None, :] + cim[None] * pr[:, :, None, :]
        kern = (jnp.einsum('ngcp,gpd->ngcd', cpr, bbr, precision=HIGHEST)
                - jnp.einsum('ngcp,gpd->ngcd', cpi, bbi, precision=HIGHEST))
        pbr = pr[..., None] * bbr[None] - pi[..., None] * bbi[None]
        pbi = pr[..., None] * bbi[None] + pi[..., None] * bbr[None]
        per_dir.append(dict(pr=pr, pi=pi, cpr=cpr, cpi=cpi, kern=kern, pbr=pbr, pbi=pbi))

    g, c, p = S5_GROUPS, S5_GROUP_CH, S5_STATE
    s_idx = jnp.arange(n)[:, None]
    t_idx = jnp.arange(n)[None, :]
    tau_f = t_idx - s_idx
    tau_b = s_idx - t_idx
    kf = per_dir[0]['kern'][jnp.clip(tau_f, 0, n)] * (tau_f >= 0)[..., None, None, None]
    kb = per_dir[1]['kern'][jnp.clip(tau_b, 0, n)] * (tau_b >= 0)[..., None, None, None]
    m = (kf + kb).transpose(2, 0, 4, 1, 3).reshape(g, S5_ROW, S5_ROW)

    pow_f = n - 1 - jnp.arange(n)
    pow_b = jnp.arange(n)

    def ws_part(key, d, pows):
        return per_dir[d][key][pows].transpose(1, 0, 3, 2).reshape(g, S5_ROW, p)

    ws = jnp.concatenate([ws_part('pbr', 0, pow_f), ws_part('pbr', 1, pow_b),
                          ws_part('pbi', 0, pow_f), ws_part('pbi', 1, pow_b)], axis=-1)

    out_f = jnp.arange(n) + 1
    out_b = n - jnp.arange(n)

    def wo_part(key, d, pows):
        return per_dir[d][key][pows].transpose(1, 3, 0, 2).reshape(g, p, S5_ROW)

    zero = jnp.zeros((g, p, S5_ROW), F32)
    wof = jnp.concatenate([wo_part('cpr', 0, out_f), zero, -wo_part('cpi', 0, out_f), zero], axis=1)
    wob = jnp.concatenate([zero, wo_part('cpr', 1, out_b), zero, -wo_part('cpi', 1, out_b)], axis=1)
    ar = jnp.concatenate([per_dir[0]['pr'][n], per_dir[1]['pr'][n]], axis=-1)[:, None, :]
    ai = jnp.concatenate([per_dir[0]['pi'][n], per_dir[1]['pi'][n]], axis=-1)[:, None, :]
    return m.astype(BF16), ws.astype(BF16), wof.astype(BF16), wob.astype(BF16), ar, ai


def _s5_kernel(u_ref, m_ref, ws_ref, wof_ref, wob_ref, ar_ref, ai_ref, y_ref,
               s_ref, hf_ref, hb_ref, *, nb, nchunk):
    u = u_ref[...]
    s_ref[...] = jnp.dot(u, ws_ref[...], preferred_element_type=F32)
    ar = ar_ref[...]
    ai = ai_ref[...]
    is_fwd = lax.broadcasted_iota(jnp.int32, (nb, LANES), 1) < S5_STATE

    def step(i, carry):
        h_re, h_im = carry
        rf = pl.multiple_of(i * nb, 8)
        rb = pl.multiple_of((nchunk - 1 - i) * nb, 8)
        hf_ref[pl.ds(rf, nb), 0:LANES] = h_re
        hf_ref[pl.ds(rf, nb), LANES:2 * LANES] = h_im
        hb_ref[pl.ds(rb, nb), 0:LANES] = h_re
        hb_ref[pl.ds(rb, nb), LANES:2 * LANES] = h_im
        s_re = jnp.where(is_fwd, s_ref[pl.ds(rf, nb), 0:LANES], s_ref[pl.ds(rb, nb), 0:LANES])
        s_im = jnp.where(is_fwd, s_ref[pl.ds(rf, nb), LANES:2 * LANES],
                         s_ref[pl.ds(rb, nb), LANES:2 * LANES])
        return (ar * h_re - ai * h_im + s_re, ar * h_im + ai * h_re + s_im)

    zero = jnp.zeros((nb, LANES), F32)
    lax.fori_loop(0, nchunk, step, (zero, zero))
    y = jnp.dot(u, m_ref[...], preferred_element_type=F32)
    y = y + jnp.dot(hf_ref[...].astype(BF16), wof_ref[...], preferred_element_type=F32)
    y = y + jnp.dot(hb_ref[...].astype(BF16), wob_ref[...], preferred_element_type=F32)
    y_ref[...] = y


def _s5(u, s5w):
    m, ws, wof, wob, ar, ai = s5w
    bsz, seq, _ = u.shape
    nchunk = seq // S5_CHUNK
    rows = nchunk * bsz
    u2 = (u.reshape(bsz, nchunk, S5_CHUNK, S5_GROUPS, S5_GROUP_CH)
          .transpose(3, 1, 0, 2, 4).reshape(S5_GROUPS, rows, S5_ROW).astype(BF16))
    grp = lambda r, c: pl.BlockSpec((None, r, c), lambda g: (g, 0, 0))
    y2 = pl.pallas_call(
        functools.partial(_s5_kernel, nb=bsz, nchunk=nchunk),
        grid=(S5_GROUPS,),
        in_specs=[grp(rows, S5_ROW), grp(S5_ROW, S5_ROW), grp(S5_ROW, S5_ROW),
                  grp(S5_ROW, S5_ROW), grp(S5_ROW, S5_ROW), grp(1, LANES), grp(1, LANES)],
        out_specs=grp(rows, S5_ROW),
        out_shape=jax.ShapeDtypeStruct((S5_GROUPS, rows, S5_ROW), F32),
        scratch_shapes=[pltpu.VMEM((rows, S5_ROW), F32)] * 3,
        compiler_params=_cparams(("parallel",)),
        name="s5",
    )(u2, m, ws, wof, wob, ar, ai)
    return (y2.reshape(S5_GROUPS, nchunk, bsz, S5_CHUNK, S5_GROUP_CH)
            .transpose(2, 1, 3, 0, 4).reshape(bsz, seq, S5_WIDTH))


def _attn_kernel(lam_ref, q_ref, k_ref, v_ref, g_ref, o_ref):
    q = q_ref[...]
    k = k_ref[...]
    v = v_ref[...]
    first = lax.broadcasted_iota(jnp.int32, q.shape, 1) < ATT_HEAD_DIM
    zero = jnp.zeros_like(q)

    def softmax_pv(qm):
        s = lax.dot_general(qm, k, (((1,), (1,)), ((), ())), preferred_element_type=F32)
        e = jnp.exp(s - jnp.max(s, axis=-1, keepdims=True))
        l = jnp.sum(e, axis=-1, keepdims=True)
        return jnp.dot(e.astype(BF16), v, preferred_element_type=F32) / l

    o = softmax_pv(jnp.where(first, q, zero)) - lam_ref[0] * softmax_pv(jnp.where(first, zero, q))
    o = o * lax.rsqrt(jnp.mean(o * o, axis=-1, keepdims=True) + LN_EPS) * g_ref[...]
    o_ref[...] = o.astype(BF16)


def _attention(lam, q, k, v, g_row):
    bsz, seq, _ = q.shape
    tq = min(ATT_TILE, seq)
    return pl.pallas_call(
        _attn_kernel,
        grid=(bsz, ATT_HEADS, seq // tq),
        in_specs=[pl.BlockSpec(memory_space=pltpu.SMEM),
                  pl.BlockSpec((None, tq, LANES), lambda b, h, i: (b, i, h)),
                  pl.BlockSpec((None, seq, LANES), lambda b, h, i: (b, 0, h)),
                  pl.BlockSpec((None, seq, LANES), lambda b, h, i: (b, 0, h)),
                  pl.BlockSpec((1, LANES), lambda b, h, i: (0, 0))],
        out_specs=pl.BlockSpec((None, tq, LANES), lambda b, h, i: (b, i, h)),
        out_shape=jax.ShapeDtypeStruct((bsz, seq, ATT_WIDTH), BF16),
        compiler_params=_cparams(("parallel", "parallel", "arbitrary")),
        name="attn",
    )(lam, q, k, v, g_row)


def _merge_kernel(x_ref, u_ref, y_ref, o_ref, ga_ref, gb_ref, mod_ref, d_ref, wglu_ref, ws5_ref,
                  watt_ref, wo_ref, ln1g_ref, ln1b_ref, wq_ref, x1_ref, h2_ref, pq_ref):
    mod = mod_ref[...]
    s = _gelu_tanh(u_ref[...] * d_ref[...] + y_ref[...])
    s = s * _sigmoid(jnp.dot(s.astype(BF16), wglu_ref[...], preferred_element_type=F32))
    br_a = jnp.dot(s.astype(BF16), ws5_ref[...], preferred_element_type=F32)
    br_b = jnp.dot(o_ref[...], watt_ref[...], preferred_element_type=F32)
    merged = ga_ref[...].astype(F32) * br_a + gb_ref[...].astype(F32) * br_b
    z = jnp.dot(merged.astype(BF16), wo_ref[...], preferred_element_type=F32)
    x1 = _ln(DN_ALPHA * x_ref[...] + mod[2:3] * z) * ln1g_ref[...] + ln1b_ref[...]
    x1_ref[...] = x1
    h2 = (_ln(x1) * (1.0 + mod[4:5]) + mod[3:4]).astype(BF16)
    h2_ref[...] = h2
    pq_ref[...] = jnp.dot(h2, wq_ref[...], preferred_element_type=F32).astype(BF16)


def _merge(x, u, y, o, ga, gb, mod3, wts):
    bsz, seq, _ = x.shape
    tm = min(MERGE_TILE, seq)
    tok = lambda n: pl.BlockSpec((None, tm, n), lambda b, i: (b, i, 0))
    row = lambda n: _const_spec((1, n))
    return pl.pallas_call(
        _merge_kernel,
        grid=(bsz, seq // tm),
        in_specs=[tok(D_MODEL), tok(S5_WIDTH), tok(S5_WIDTH), tok(ATT_WIDTH), tok(D_MODEL), tok(D_MODEL),
                  pl.BlockSpec((None, 6, D_MODEL), lambda b, i: (b, 0, 0)),
                  row(S5_WIDTH), _const_spec((S5_WIDTH, S5_WIDTH)), _const_spec((S5_WIDTH, D_MODEL)),
                  _const_spec((ATT_WIDTH, D_MODEL)), _const_spec((D_MODEL, D_MODEL)),
                  row(D_MODEL), row(D_MODEL), _const_spec((D_MODEL, PEER_NQ))],
        out_specs=[tok(D_MODEL), tok(D_MODEL), tok(PEER_NQ)],
        out_shape=[jax.ShapeDtypeStruct((bsz, seq, D_MODEL), F32),
                   jax.ShapeDtypeStruct((bsz, seq, D_MODEL), BF16),
                   jax.ShapeDtypeStruct((bsz, seq, PEER_NQ), BF16)],
        compiler_params=_cparams(("parallel", "parallel")),
        name="merge",
    )(x, u, y, o, ga, gb, mod3, wts['s5_d'], wts['w_glu'], wts['w_s5_out'], wts['w_attn_out'],
      wts['w_o'], wts['ln1_g'], wts['ln1_b'], wts['w_q'])


def _route_kernel(pq_ref, keys_ref, w_ref, a_sc, b_sc, th_sc, wt_sc, ix_sc, wr_sc, ir_sc, s2_sc, e2_sc):
    tn = pq_ref.shape[0]
    sub_f = lax.broadcasted_iota(jnp.int32, (PEER_NKEYS, tn), 0).astype(F32)
    row8 = lax.broadcasted_iota(jnp.int32, (8, tn), 0)
    nt_dims = (((1,), (1,)), ((), ()))

    for h in range(PEER_HEADS):
        q1 = pq_ref[:, (2 * h) * PEER_HALF:(2 * h + 1) * PEER_HALF]
        q2 = pq_ref[:, (2 * h + 1) * PEER_HALF:(2 * h + 2) * PEER_HALF]
        st1 = lax.dot_general(keys_ref[2 * h], q1, nt_dims, preferred_element_type=F32)
        st2 = lax.dot_general(keys_ref[2 * h + 1], q2, nt_dims, preferred_element_type=F32)
        s2 = st2.T
        s2_sc[h] = s2
        e2_sc[h] = jnp.exp(s2 - jnp.max(s2, axis=-1, keepdims=True))

        a_vals = []
        for k in range(PEER_TOPK):
            m = jnp.max(st1, axis=0, keepdims=True)
            ix = jnp.min(jnp.where(st1 == m, sub_f, float(PEER_NKEYS)), axis=0, keepdims=True)
            st1 = jnp.where(sub_f == ix, NEG_BIG, st1)
            a_vals.append(m)
            ix_sc[h * PEER_TOPK + k:h * PEER_TOPK + k + 1, :] = ix
        for k in range(PEER_TOPK):
            m = jnp.max(st2, axis=0, keepdims=True)
            st2 = jnp.where(st2 == m, NEG_BIG, st2)
            b_sc[k:k + 1, :] = m
        b_lo = b_sc[0:8, :]
        b_hi = b_sc[8:16, :]

        cands = [a_vals[0] + b_lo, a_vals[0] + b_hi]
        for k in range(1, PEER_TOPK):
            cands.append(jnp.where(row8 < PEER_TOPK // (k + 1), a_vals[k] + b_lo, NEG_BIG))
        m0 = a_vals[0] + b_sc[0:1, :]
        z = jnp.zeros_like(m0)
        tau = m0
        for _ in range(PEER_TOPK):
            tau = jnp.max(functools.reduce(jnp.maximum, cands), axis=0, keepdims=True)
            cands = [jnp.where(c == tau, NEG_BIG, c) for c in cands]
            z = z + jnp.exp(tau - m0)
        inv_z = 1.0 / z
        for k in range(PEER_TOPK):
            th = jnp.where(a_vals[k] + b_lo >= tau, b_lo, POS_BIG)
            if PEER_TOPK // (k + 1) > 8:
                th = jnp.minimum(th, jnp.where(a_vals[k] + b_hi >= tau, b_hi, POS_BIG))
            j = h * PEER_TOPK + k
            th_sc[j:j + 1, :] = jnp.min(th, axis=0, keepdims=True)
            wt_sc[j:j + 1, :] = jnp.exp(a_vals[k] - a_vals[0]) * inv_z

    wr_sc[...] = wt_sc[...].T
    ir_sc[...] = ix_sc[...].T
    lane_i = lax.broadcasted_iota(jnp.int32, (PEER_NKEYS, tn), 1)
    sub_j = lax.broadcasted_iota(jnp.int32, (PEER_NKEYS, PEER_NKEYS), 0).astype(F32)

    def token(t):
        th_col = jnp.sum(jnp.where(lane_i == t, th_sc[...], 0.0), axis=1, keepdims=True)
        s2b = jnp.concatenate([jnp.broadcast_to(s2_sc[h, pl.ds(t, 1), :], (PEER_TOPK, PEER_NKEYS))
                               for h in range(PEER_HEADS)], axis=0)
        e2b = jnp.concatenate([jnp.broadcast_to(e2_sc[h, pl.ds(t, 1), :], (PEER_TOPK, PEER_NKEYS))
                               for h in range(PEER_HEADS)], axis=0)
        r = jnp.where(s2b >= th_col, e2b, 0.0).astype(BF16)
        pt = jnp.where(sub_j == ir_sc[pl.ds(t, 1), :], wr_sc[pl.ds(t, 1), :], 0.0).astype(BF16)
        return jnp.dot(pt, r, preferred_element_type=F32)

    def token_group(g, carry):
        t0 = pl.multiple_of(g * 8, 8)
        tiles = jnp.stack([token(t0 + u) for u in range(8)])
        w_ref[:, pl.ds(t0, 8), :] = jnp.swapaxes(tiles, 0, 1)
        return carry

    lax.fori_loop(0, tn // 8, token_group, 0)


def _route(pq, keys_b):
    ntok = pq.shape[0]
    tn = ROUTE_TILE
    nj = PEER_HEADS * PEER_TOPK
    return pl.pallas_call(
        _route_kernel,
        grid=(ntok // tn,),
        in_specs=[pl.BlockSpec((tn, PEER_NQ), lambda i: (i, 0)),
                  _const_spec((2 * PEER_HEADS, PEER_NKEYS, PEER_HALF))],
        out_specs=pl.BlockSpec((None, PEER_NKEYS, tn, PEER_NKEYS), lambda i: (i, 0, 0, 0)),
        out_shape=jax.ShapeDtypeStruct((ntok // tn, PEER_NKEYS, tn, PEER_NKEYS), F32),
        scratch_shapes=[pltpu.VMEM((PEER_TOPK, tn), F32), pltpu.VMEM((PEER_TOPK, tn), F32),
                        pltpu.VMEM((nj, tn), F32), pltpu.VMEM((nj, tn), F32), pltpu.VMEM((nj, tn), F32),
                        pltpu.VMEM((tn, nj), F32), pltpu.VMEM((tn, nj), F32),
                        pltpu.VMEM((PEER_HEADS, tn, PEER_NKEYS), F32),
                        pltpu.VMEM((PEER_HEADS, tn, PEER_NKEYS), F32)],
        compiler_params=_cparams(("parallel",)),
        name="route",
    )(pq, keys_b)


def _expert_kernel(h_ref, ut_ref, v_ref, w_ref, x1_ref, mod_ref, g_ref, b_ref, o_ref, acc_ref, gate_sc):
    e = pl.program_id(2)

    @pl.when(e == 0)
    def _():
        acc_ref[...] = jnp.zeros_like(acc_ref)

    act = jnp.dot(h_ref[...], ut_ref[...], preferred_element_type=F32)
    rt = w_ref.shape[2]
    for j in range(w_ref.shape[0]):
        for k in range(w_ref.shape[1]):
            a = act[j * rt:(j + 1) * rt, k * PEER_NKEYS:(k + 1) * PEER_NKEYS]
            gate_sc[j * rt:(j + 1) * rt, k * PEER_NKEYS:(k + 1) * PEER_NKEYS] = (
                _gelu_tanh(a) * w_ref[j, k]).astype(BF16)
    acc_ref[...] += jnp.dot(gate_sc[...], v_ref[...], preferred_element_type=F32)

    @pl.when(e == pl.num_programs(2) - 1)
    def _():
        mod = mod_ref[...]
        o_ref[...] = _ln(DN_ALPHA * x1_ref[...] + mod[5:6] * acc_ref[...]) * g_ref[...] + b_ref[...]


def _experts(h2, w4, x1, mod3, ut_b, v_b, ln2_g, ln2_b):
    bsz, seq, _ = x1.shape
    tt = min(EXPERT_TOK_TILE, seq)
    ce = EXPERT_CHUNK
    rt = w4.shape[2]
    nblk = seq // tt
    tok = lambda n: pl.BlockSpec((None, tt, n), lambda b, i, e: (b, i, 0))
    return pl.pallas_call(
        _expert_kernel,
        grid=(bsz, nblk, PEER_EXPERTS // ce),
        in_specs=[tok(D_MODEL),
                  pl.BlockSpec((D_MODEL, ce), lambda b, i, e: (0, e)),
                  pl.BlockSpec((ce, D_MODEL), lambda b, i, e: (e, 0)),
                  pl.BlockSpec((tt // rt, ce // PEER_NKEYS, rt, PEER_NKEYS),
                               lambda b, i, e: (b * nblk + i, e, 0, 0)),
                  tok(D_MODEL),
                  pl.BlockSpec((None, 6, D_MODEL), lambda b, i, e: (b, 0, 0)),
                  pl.BlockSpec((1, D_MODEL), lambda b, i, e: (0, 0)),
                  pl.BlockSpec((1, D_MODEL), lambda b, i, e: (0, 0))],
        out_specs=tok(D_MODEL),
        out_shape=jax.ShapeDtypeStruct((bsz, seq, D_MODEL), F32),
        scratch_shapes=[pltpu.VMEM((tt, D_MODEL), F32), pltpu.VMEM((tt, ce), BF16)],
        compiler_params=_cparams(("parallel", "parallel", "arbitrary")),
        name="expert",
    )(h2, ut_b, v_b, w4, x1, mod3, ln2_g, ln2_b)


def _rope_tables(seq):
    half = ATT_HEAD_DIM // 2
    inv_freq = ROPE_THETA ** (-jnp.arange(half, dtype=F32) * 2.0 / ATT_HEAD_DIM)
    ang = jnp.arange(seq, dtype=F32)[:, None] * inv_freq[None, :]
    cos = jnp.cos(ang)
    sin = jnp.sin(ang)
    zero = jnp.zeros_like(sin)
    reps = LANES // ATT_HEAD_DIM
    cos_t = jnp.tile(jnp.concatenate([cos, cos], axis=1), (1, reps))
    sa_t = jnp.tile(jnp.concatenate([-sin, zero], axis=1), (1, reps))
    sb_t = jnp.tile(jnp.concatenate([zero, sin], axis=1), (1, reps))
    return cos_t, sa_t, sb_t


def _prepare(layer_idx, p):
    lam_init = 0.8 - 0.6 * math.exp(-0.3 * layer_idx)
    lam = (jnp.exp(jnp.sum(p['attn_lambda_q1'].astype(F32) * p['attn_lambda_k1'].astype(F32)))
           - jnp.exp(jnp.sum(p['attn_lambda_q2'].astype(F32) * p['attn_lambda_k2'].astype(F32))) + lam_init)
    row = lambda a: a.astype(F32).reshape(1, -1)
    return dict(
        w_ada=p['w_ada'].astype(F32), b_ada=p['b_ada'].astype(F32),
        w_in=p['w_in'].astype(BF16),
        s5w=_s5_weights(p['s5_lambda_re'], p['s5_lambda_im'], p['s5_log_dt'], p['s5_b_re'],
                        p['s5_b_im'], p['s5_c_re'], p['s5_c_im']),
        s5_d=row(p['s5_d']), w_glu=p['s5_w_glu'].astype(BF16), w_s5_out=p['w_s5_out'].astype(BF16),
        lam=lam.reshape(1).astype(F32),
        subln=row(p['attn_subln_g']) * (1.0 - lam_init),
        w_attn_out=p['w_attn_out'].astype(BF16), w_o=p['w_o'].astype(BF16),
        ln1_g=row(p['ln1_g']), ln1_b=row(p['ln1_b']),
        w_q=p['peer_w_q'].astype(BF16),
        keys=p['peer_keys'].astype(BF16).reshape(2 * PEER_HEADS, PEER_NKEYS, PEER_HALF),
        ut=p['peer_u'].astype(BF16).T, v=p['peer_v'].astype(BF16),
        ln2_g=row(p['ln2_g']), ln2_b=row(p['ln2_b']),
    )


def _encoder_layer(x, c, wts, rope):
    bsz, seq, _ = x.shape
    mod3 = _ada(c.astype(F32), wts['w_ada'], wts['b_ada']).reshape(bsz, 6, D_MODEL)
    u, q, k, v, ga, gb = _inproj(x.astype(F32), mod3, wts['w_in'], *rope)
    y = _s5(u, wts['s5w'])
    o = _attention(wts['lam'], q, k, v, wts['subln'])
    x1, h2, pq = _merge(x.astype(F32), u, y, o, ga, gb, mod3, wts)
    w4 = _route(pq.reshape(bsz * seq, PEER_NQ), wts['keys'])
    x2 = _experts(h2, w4, x1, mod3, wts['ut'], wts['v'], wts['ln2_g'], wts['ln2_b'])
    return x2.astype(x.dtype)


def kernel(x_prompt, x_sample, c_prompt, c_sample, w_ada, b_ada, w_in, s5_lambda_re, s5_lambda_im, s5_log_dt, s5_b_re, s5_b_im, s5_c_re, s5_c_im, s5_d, s5_w_glu, w_s5_out, attn_lambda_q1, attn_lambda_k1, attn_lambda_q2, attn_lambda_k2, attn_subln_g, w_attn_out, w_o, ln1_g, ln1_b, peer_w_q, peer_keys, peer_u, peer_v, ln2_g, ln2_b):
    params = dict(w_ada=w_ada, b_ada=b_ada, w_in=w_in, s5_lambda_re=s5_lambda_re,
                  s5_lambda_im=s5_lambda_im, s5_log_dt=s5_log_dt, s5_b_re=s5_b_re, s5_b_im=s5_b_im,
                  s5_c_re=s5_c_re, s5_c_im=s5_c_im, s5_d=s5_d, s5_w_glu=s5_w_glu, w_s5_out=w_s5_out,
                  attn_lambda_q1=attn_lambda_q1, attn_lambda_k1=attn_lambda_k1,
                  attn_lambda_q2=attn_lambda_q2, attn_lambda_k2=attn_lambda_k2,
                  attn_subln_g=attn_subln_g, w_attn_out=w_attn_out, w_o=w_o, ln1_g=ln1_g, ln1_b=ln1_b,
                  peer_w_q=peer_w_q, peer_keys=peer_keys, peer_u=peer_u, peer_v=peer_v,
                  ln2_g=ln2_g, ln2_b=ln2_b)
    y_prompt, y_sample = x_prompt, x_sample
    rope_p = _rope_tables(x_prompt.shape[1])
    rope_s = _rope_tables(x_sample.shape[1])
    for l in range(DEPTH):
        wts = _prepare(l, {name: a[l] for name, a in params.items()})
        y_prompt = _encoder_layer(y_prompt, c_prompt, wts, rope_p)
        y_sample = _encoder_layer(y_sample, c_sample, wts, rope_s)
    return (y_prompt, y_sample)
```

```python
import functools
import math

import jax
import jax.numpy as jnp
from jax import lax
from jax.experimental import pallas as pl
from jax.experimental.pallas import tpu as pltpu

F32 = jnp.float32
BF16 = jnp.bfloat16
HIGHEST = lax.Precision.HIGHEST

D_MODEL = 1024
DEPTH = 1
S5_WIDTH = D_MODEL // 2
S5_GROUP_CH = 16
S5_GROUPS = S5_WIDTH // S5_GROUP_CH
S5_STATE = 64
S5_CHUNK = 16
S5_ROW = S5_CHUNK * S5_GROUP_CH
ATT_HEADS = 8
ATT_HEAD_DIM = D_MODEL // (2 * ATT_HEADS)
ATT_WIDTH = ATT_HEADS * 2 * ATT_HEAD_DIM
ROPE_THETA = 10000.0
N_IN = S5_WIDTH + 3 * ATT_WIDTH + 2 * D_MODEL
PEER_HEADS = 8
PEER_NKEYS = 128
PEER_EXPERTS = PEER_NKEYS * PEER_NKEYS
PEER_TOPK = 16
PEER_QDIM = 256
PEER_HALF = PEER_QDIM // 2
PEER_NQ = PEER_HEADS * PEER_QDIM
DN_ALPHA = (2 * DEPTH) ** 0.25
LN_EPS = 1e-5

LANES = 128
VMEM_LIMIT_BYTES = 56 * 1024 * 1024
NEG_BIG = -3.0e38
POS_BIG = 3.0e38

IN_TILE = 512
ATT_TILE = 256
MERGE_TILE = 256
ROUTE_TILE = 128
ROUTE_GROUPS_PER_TRIP = 2
EXPERT_TOK_TILE = 1024
EXPERT_CHUNK = 1024


def _cparams(sem):
    return pltpu.CompilerParams(dimension_semantics=sem, vmem_limit_bytes=VMEM_LIMIT_BYTES)


def _const_spec(shape):
    nd = len(shape)
    return pl.BlockSpec(shape, lambda *_: (0,) * nd, pipeline_mode=pl.Buffered(1))


def _ln(x):
    mu = jnp.mean(x, axis=-1, keepdims=True)
    xc = x - mu
    var = jnp.mean(xc * xc, axis=-1, keepdims=True)
    return xc * lax.rsqrt(var + LN_EPS)


def _gelu_tanh(x):
    return 0.5 * x * (1.0 + jnp.tanh(math.sqrt(2.0 / math.pi) * (x + 0.044715 * (x * x * x))))


def _sigmoid(x):
    return 1.0 / (1.0 + jnp.exp(-x))


def _ada_kernel(c_ref, w_ref, b_ref, o_ref):
    c = c_ref[...]
    s = c * _sigmoid(c)
    o_ref[...] = jnp.dot(s, w_ref[...], precision=HIGHEST, preferred_element_type=F32) + b_ref[...]


def _ada(c, w_ada, b_ada):
    bsz = c.shape[0]
    nblk = w_ada.shape[1] // D_MODEL
    return pl.pallas_call(
        _ada_kernel,
        grid=(nblk,),
        in_specs=[
            pl.BlockSpec((bsz, D_MODEL), lambda j: (0, 0)),
            pl.BlockSpec((D_MODEL, D_MODEL), lambda j: (0, j)),
            pl.BlockSpec((1, D_MODEL), lambda j: (0, j)),
        ],
        out_specs=pl.BlockSpec((bsz, D_MODEL), lambda j: (0, j)),
        out_shape=jax.ShapeDtypeStruct((bsz, nblk * D_MODEL), F32),
        compiler_params=_cparams(("arbitrary",)),
        name="ada",
    )(c, w_ada, b_ada.reshape(1, -1))


def _in_kernel(x_ref, mod_ref, w_ref, cos_ref, sa_ref, sb_ref,
               u_ref, q_ref, k_ref, v_ref, ga_ref, gb_ref):
    mod = mod_ref[...]
    h = _ln(x_ref[...]) * (1.0 + mod[1:2]) + mod[0:1]
    hb = h.astype(BF16)

    def proj(a, b):
        return jnp.dot(hb, w_ref[:, a:b], preferred_element_type=F32)

    o1 = S5_WIDTH
    o2 = o1 + ATT_WIDTH
    o3 = o2 + ATT_WIDTH
    o4 = o3 + ATT_WIDTH
    o5 = o4 + D_MODEL
    u_ref[...] = proj(0, o1)
    cos, sa, sb = cos_ref[...], sa_ref[...], sb_ref[...]
    half = ATT_HEAD_DIM // 2

    def rope_store(t, o_ref, scale):
        for j in range(ATT_WIDTH // LANES):
            tj = t[:, j * LANES:(j + 1) * LANES]
            r = (tj * cos + pltpu.roll(tj, LANES - half, 1) * sa + pltpu.roll(tj, half, 1) * sb)
            o_ref[:, j * LANES:(j + 1) * LANES] = (r * scale).astype(BF16)

    rope_store(proj(o1, o2), q_ref, ATT_HEAD_DIM ** -0.5)
    rope_store(proj(o2, o3), k_ref, 1.0)
    v_ref[...] = proj(o3, o4).astype(BF16)
    ga_ref[...] = _sigmoid(proj(o4, o5)).astype(BF16)
    gb_ref[...] = _sigmoid(proj(o5, N_IN)).astype(BF16)


def _inproj(x, mod3, w_in_b, cos, sa, sb):
    bsz, seq, _ = x.shape
    tm = min(IN_TILE, seq)
    tok = lambda n: pl.BlockSpec((None, tm, n), lambda b, i: (b, i, 0))
    rope = pl.BlockSpec((tm, LANES), lambda b, i: (i, 0))
    outs = [(S5_WIDTH, F32), (ATT_WIDTH, BF16), (ATT_WIDTH, BF16), (ATT_WIDTH, BF16),
            (D_MODEL, BF16), (D_MODEL, BF16)]
    return pl.pallas_call(
        _in_kernel,
        grid=(bsz, seq // tm),
        in_specs=[tok(D_MODEL),
                  pl.BlockSpec((None, 6, D_MODEL), lambda b, i: (b, 0, 0)),
                  _const_spec((D_MODEL, N_IN)), rope, rope, rope],
        out_specs=[tok(n) for n, _ in outs],
        out_shape=[jax.ShapeDtypeStruct((bsz, seq, n), dt) for n, dt in outs],
        compiler_params=_cparams(("parallel", "parallel")),
        name="inproj",
    )(x, mod3, w_in_b, cos, sa, sb)


def _s5_weights(lam_re, lam_im, log_dt, b_re, b_im, c_re, c_im):
    n = S5_CHUNK
    per_dir = []
    for d in range(2):
        lr = lam_re[d].astype(F32)
        li = lam_im[d].astype(F32)
        dt = jnp.exp(log_dt[d].astype(F32))[:, None]
        mag = jnp.exp(lr * dt)
        lbr = mag * jnp.cos(li * dt)
        lbi = mag * jnp.sin(li * dt)
        den = lr * lr + li * li
        nr = lbr - 1.0
        cr = (nr * lr + lbi * li) / den
        ci = (lbi * lr - nr * li) / den
        br = b_re[d].astype(F32)
        bi = b_im[d].astype(F32)
        bbr = cr[..., None] * br - ci[..., None] * bi
        bbi = cr[..., None] * bi + ci[..., None] * br
        pr = [jnp.ones_like(lbr)]
        pi = [jnp.zeros_like(lbi)]
        for _ in range(n):
            pr.append(pr[-1] * lbr - pi[-1] * lbi)
            pi.append(pr[-2] * lbi + pi[-1] * lbr)
        pr = jnp.stack(pr)
        pi = jnp.stack(pi)
        cre = c_re[d].astype(F32)
        cim = c_im[d].astype(F32)
        cpr = cre[None] * pr[:, :, None, :] - cim[None] * pi[:, :, None, :]
        cpi = cre[None] * pi[:, :, None, :] + cim[None] * pr[:, :, None, :]
        kern = (jnp.einsum('ngcp,gpd->ngcd', cpr, bbr, precision=HIGHEST)
                - jnp.einsum('ngcp,gpd->ngcd', cpi, bbi, precision=HIGHEST))
        pbr = pr[..., None] * bbr[None] - pi[..., None] * bbi[None]
        pbi = pr[..., None] * bbi[None] + pi[..., None] * bbr[None]
        per_dir.append(dict(pr=pr, pi=pi, cpr=cpr, cpi=cpi, kern=kern, pbr=pbr, pbi=pbi))

    g, c, p = S5_GROUPS, S5_GROUP_CH, S5_STATE
    s_idx = jnp.arange(n)[:, None]
    t_idx = jnp.arange(n)[None, :]
    tau_f = t_idx - s_idx
    tau_b = s_idx - t_idx
    kf = per_dir[0]['kern'][jnp.clip(tau_f, 0, n)] * (tau_f >= 0)[..., None, None, None]
    kb = per_dir[1]['kern'][jnp.clip(tau_b, 0, n)] * (tau_b >= 0)[..., None, None, None]
    m = (kf + kb).transpose(2, 0, 4, 1, 3).reshape(g, S5_ROW, S5_ROW)

    pow_f = n - 1 - jnp.arange(n)
    pow_b = jnp.arange(n)

    def ws_part(key, d, pows):
        return per_dir[d][key][pows].transpose(1, 0, 3, 2).reshape(g, S5_ROW, p)

    ws = jnp.concatenate([ws_part('pbr', 0, pow_f), ws_part('pbr', 1, pow_b),
                          ws_part('pbi', 0, pow_f), ws_part('pbi', 1, pow_b)], axis=-1)

    out_f = jnp.arange(n) + 1
    out_b = n - jnp.arange(n)

    def wo_part(key, d, pows):
        return per_dir[d][key][pows].transpose(1, 3, 0, 2).reshape(g, p, S5_ROW)

    zero = jnp.zeros((g, p, S5_ROW), F32)
    wof = jnp.concatenate([wo_part('cpr', 0, out_f), zero, -wo_part('cpi', 0, out_f), zero], axis=1)
    wob = jnp.concatenate([zero, wo_part('cpr', 1, out_b), zero, -wo_part('cpi', 1, out_b)], axis=1)
    ar = jnp.concatenate([per_dir[0]['pr'][n], per_dir[1]['pr'][n]], axis=-1)[:, None, :]
    ai = jnp.concatenate([per_dir[0]['pi'][n], per_dir[1]['pi'][n]], axis=-1)[:, None, :]
    return m.astype(BF16), ws.astype(BF16), wof.astype(BF16), wob.astype(BF16), ar, ai


def _s5_kernel(u_ref, m_ref, ws_ref, wof_ref, wob_ref, ar_ref, ai_ref, y_ref,
               s_ref, hf_ref, hb_ref, *, nb, nchunk):
    u = u_ref[...]
    s_ref[...] = jnp.dot(u, ws_ref[...], preferred_element_type=F32)
    ar = ar_ref[...]
    ai = ai_ref[...]
    is_fwd = lax.broadcasted_iota(jnp.int32, (nb, LANES), 1) < S5_STATE

    def step(i, carry):
        h_re, h_im = carry
        rf = pl.multiple_of(i * nb, 8)
        rb = pl.multiple_of((nchunk - 1 - i) * nb, 8)
        hf_ref[pl.ds(rf, nb), 0:LANES] = h_re
        hf_ref[pl.ds(rf, nb), LANES:2 * LANES] = h_im
        hb_ref[pl.ds(rb, nb), 0:LANES] = h_re
        hb_ref[pl.ds(rb, nb), LANES:2 * LANES] = h_im
        s_re = jnp.where(is_fwd, s_ref[pl.ds(rf, nb), 0:LANES], s_ref[pl.ds(rb, nb), 0:LANES])
        s_im = jnp.where(is_fwd, s_ref[pl.ds(rf, nb), LANES:2 * LANES],
                         s_ref[pl.ds(rb, nb), LANES:2 * LANES])
        return (ar * h_re - ai * h_im + s_re, ar * h_im + ai * h_re + s_im)

    zero = jnp.zeros((nb, LANES), F32)
    lax.fori_loop(0, nchunk, step, (zero, zero))
    y = jnp.dot(u, m_ref[...], preferred_element_type=F32)
    y = y + jnp.dot(hf_ref[...].astype(BF16), wof_ref[...], preferred_element_type=F32)
    y = y + jnp.dot(hb_ref[...].astype(BF16), wob_ref[...], preferred_element_type=F32)
    y_ref[...] = y


def _s5(u, s5w):
    m, ws, wof, wob, ar, ai = s5w
    bsz, seq, _ = u.shape
    nchunk = seq // S5_CHUNK
    rows = nchunk * bsz
    u2 = (u.reshape(bsz, nchunk, S5_CHUNK, S5_GROUPS, S5_GROUP_CH)
          .transpose(3, 1, 0, 2, 4).reshape(S5_GROUPS, rows, S5_ROW).astype(BF16))
    grp = lambda r, c: pl.BlockSpec((None, r, c), lambda g: (g, 0, 0))
    y2 = pl.pallas_call(
        functools.partial(_s5_kernel, nb=bsz, nchunk=nchunk),
        grid=(S5_GROUPS,),
        in_specs=[grp(rows, S5_ROW), grp(S5_ROW, S5_ROW), grp(S5_ROW, S5_ROW),
                  grp(S5_ROW, S5_ROW), grp(S5_ROW, S5_ROW), grp(1, LANES), grp(1, LANES)],
        out_specs=grp(rows, S5_ROW),
        out_shape=jax.ShapeDtypeStruct((S5_GROUPS, rows, S5_ROW), F32),
        scratch_shapes=[pltpu.VMEM((rows, S5_ROW), F32)] * 3,
        compiler_params=_cparams(("parallel",)),
        name="s5",
    )(u2, m, ws, wof, wob, ar, ai)
    return (y2.reshape(S5_GROUPS, nchunk, bsz, S5_CHUNK, S5_GROUP_CH)
            .transpose(2, 1, 3, 0, 4).reshape(bsz, seq, S5_WIDTH))


def _attn_kernel(lam_ref, q_ref, k_ref, v_ref, g_ref, o_ref):
    q = q_ref[...]
    k = k_ref[...]
    v = v_ref[...]
    first = lax.broadcasted_iota(jnp.int32, q.shape, 1) < ATT_HEAD_DIM
    zero = jnp.zeros_like(q)

    def softmax_pv(qm):
        s = lax.dot_general(qm, k, (((1,), (1,)), ((), ())), preferred_element_type=F32)
        e = jnp.exp(s - jnp.max(s, axis=-1, keepdims=True))
        l = jnp.sum(e, axis=-1, keepdims=True)
        return jnp.dot(e.astype(BF16), v, preferred_element_type=F32) / l

    o = softmax_pv(jnp.where(first, q, zero)) - lam_ref[0] * softmax_pv(jnp.where(first, zero, q))
    o = o * lax.rsqrt(jnp.mean(o * o, axis=-1, keepdims=True) + LN_EPS) * g_ref[...]
    o_ref[...] = o.astype(BF16)


def _attention(lam, q, k, v, g_row):
    bsz, seq, _ = q.shape
    tq = min(ATT_TILE, seq)
    return pl.pallas_call(
        _attn_kernel,
        grid=(bsz, ATT_HEADS, seq // tq),
        in_specs=[pl.BlockSpec(memory_space=pltpu.SMEM),
                  pl.BlockSpec((None, tq, LANES), lambda b, h, i: (b, i, h)),
                  pl.BlockSpec((None, seq, LANES), lambda b, h, i: (b, 0, h)),
                  pl.BlockSpec((None, seq, LANES), lambda b, h, i: (b, 0, h)),
                  pl.BlockSpec((1, LANES), lambda b, h, i: (0, 0))],
        out_specs=pl.BlockSpec((None, tq, LANES), lambda b, h, i: (b, i, h)),
        out_shape=jax.ShapeDtypeStruct((bsz, seq, ATT_WIDTH), BF16),
        compiler_params=_cparams(("parallel", "parallel", "arbitrary")),
        name="attn",
    )(lam, q, k, v, g_row)


def _merge_kernel(x_ref, u_ref, y_ref, o_ref, ga_ref, gb_ref, mod_ref, d_ref, wglu_ref, ws5_ref,
                  watt_ref, wo_ref, ln1g_ref, ln1b_ref, wq_ref, x1_ref, h2_ref, pq_ref):
    mod = mod_ref[...]
    s = _gelu_tanh(u_ref[...] * d_ref[...] + y_ref[...])
    s = s * _sigmoid(jnp.dot(s.astype(BF16), wglu_ref[...], preferred_element_type=F32))
    br_a = jnp.dot(s.astype(BF16), ws5_ref[...], preferred_element_type=F32)
    br_b = jnp.dot(o_ref[...], watt_ref[...], preferred_element_type=F32)
    merged = ga_ref[...].astype(F32) * br_a + gb_ref[...].astype(F32) * br_b
    z = jnp.dot(merged.astype(BF16), wo_ref[...], preferred_element_type=F32)
    x1 = _ln(DN_ALPHA * x_ref[...] + mod[2:3] * z) * ln1g_ref[...] + ln1b_ref[...]
    x1_ref[...] = x1
    h2 = (_ln(x1) * (1.0 + mod[4:5]) + mod[3:4]).astype(BF16)
    h2_ref[...] = h2
    pq_ref[...] = jnp.dot(h2, wq_ref[...], preferred_element_type=F32).astype(BF16)


def _merge(x, u, y, o, ga, gb, mod3, wts):
    bsz, seq, _ = x.shape
    tm = min(MERGE_TILE, seq)
    tok = lambda n: pl.BlockSpec((None, tm, n), lambda b, i: (b, i, 0))
    row = lambda n: _const_spec((1, n))
    return pl.pallas_call(
        _merge_kernel,
        grid=(bsz, seq // tm),
        in_specs=[tok(D_MODEL), tok(S5_WIDTH), tok(S5_WIDTH), tok(ATT_WIDTH), tok(D_MODEL), tok(D_MODEL),
                  pl.BlockSpec((None, 6, D_MODEL), lambda b, i: (b, 0, 0)),
                  row(S5_WIDTH), _const_spec((S5_WIDTH, S5_WIDTH)), _const_spec((S5_WIDTH, D_MODEL)),
                  _const_spec((ATT_WIDTH, D_MODEL)), _const_spec((D_MODEL, D_MODEL)),
                  row(D_MODEL), row(D_MODEL), _const_spec((D_MODEL, PEER_NQ))],
        out_specs=[tok(D_MODEL), tok(D_MODEL), tok(PEER_NQ)],
        out_shape=[jax.ShapeDtypeStruct((bsz, seq, D_MODEL), F32),
                   jax.ShapeDtypeStruct((bsz, seq, D_MODEL), BF16),
                   jax.ShapeDtypeStruct((bsz, seq, PEER_NQ), BF16)],
        compiler_params=_cparams(("parallel", "parallel")),
        name="merge",
    )(x, u, y, o, ga, gb, mod3, wts['s5_d'], wts['w_glu'], wts['w_s5_out'], wts['w_attn_out'],
      wts['w_o'], wts['ln1_g'], wts['ln1_b'], wts['w_q'])


def _route_kernel(pq_ref, keys_ref, w_ref, a_sc, b_sc, th_sc, wt_sc, ix_sc, wr_sc, ir_sc, s2_sc, e2_sc):
    tn = pq_ref.shape[0]
    sub_f = lax.broadcasted_iota(jnp.int32, (PEER_NKEYS, tn), 0).astype(F32)
    row8 = lax.broadcasted_iota(jnp.int32, (8, tn), 0)
    nt_dims = (((1,), (1,)), ((), ()))

    for h in range(PEER_HEADS):
        q1 = pq_ref[:, (2 * h) * PEER_HALF:(2 * h + 1) * PEER_HALF]
        q2 = pq_ref[:, (2 * h + 1) * PEER_HALF:(2 * h + 2) * PEER_HALF]
        st1 = lax.dot_general(keys_ref[2 * h], q1, nt_dims, preferred_element_type=F32)
        st2 = lax.dot_general(keys_ref[2 * h + 1], q2, nt_dims, preferred_element_type=F32)
        s2 = st2.T
        s2_sc[h] = s2
        e2_sc[h] = jnp.exp(s2 - jnp.max(s2, axis=-1, keepdims=True))

        a_vals = []
        for k in range(PEER_TOPK):
            m = jnp.max(st1, axis=0, keepdims=True)
            ix = jnp.min(jnp.where(st1 == m, sub_f, float(PEER_NKEYS)), axis=0, keepdims=True)
            st1 = jnp.where(sub_f == ix, NEG_BIG, st1)
            a_vals.append(m)
            ix_sc[h * PEER_TOPK + k:h * PEER_TOPK + k + 1, :] = ix
        for k in range(PEER_TOPK):
            m = jnp.max(st2, axis=0, keepdims=True)
            st2 = jnp.where(st2 == m, NEG_BIG, st2)
            b_sc[k:k + 1, :] = m
        b_lo = b_sc[0:8, :]
        b_hi = b_sc[8:16, :]

        cands = [a_vals[0] + b_lo, a_vals[0] + b_hi]
        for k in range(1, PEER_TOPK):
            cands.append(jnp.where(row8 < PEER_TOPK // (k + 1), a_vals[k] + b_lo, NEG_BIG))
        m0 = a_vals[0] + b_sc[0:1, :]
        z = jnp.zeros_like(m0)
        tau = m0
        for _ in range(PEER_TOPK):
            tau = jnp.max(functools.reduce(jnp.maximum, cands), axis=0, keepdims=True)
            cands = [jnp.where(c == tau, NEG_BIG, c) for c in cands]
            z = z + jnp.exp(tau - m0)
        inv_z = 1.0 / z
        for k in range(PEER_TOPK):
            th = jnp.where(a_vals[k] + b_lo >= tau, b_lo, POS_BIG)
            if PEER_TOPK // (k + 1) > 8:
                th = jnp.minimum(th, jnp.where(a_vals[k] + b_hi >= tau, b_hi, POS_BIG))
            j = h * PEER_TOPK + k
            th_sc[j:j + 1, :] = jnp.min(th, axis=0, keepdims=True)
            wt_sc[j:j + 1, :] = jnp.exp(a_vals[k] - a_vals[0]) * inv_z

    wr_sc[...] = wt_sc[...].T
    ir_sc[...] = ix_sc[...].T
    lane_i = lax.broadcasted_iota(jnp.int32, (PEER_NKEYS, tn), 1)
    sub_j = lax.broadcasted_iota(jnp.int32, (PEER_NKEYS, PEER_NKEYS), 0).astype(F32)

    def token(t):
        th_col = jnp.sum(jnp.where(lane_i == t, th_sc[...], 0.0), axis=1, keepdims=True)
        s2b = jnp.concatenate([jnp.broadcast_to(s2_sc[h, pl.ds(t, 1), :], (PEER_TOPK, PEER_NKEYS))
                               for h in range(PEER_HEADS)], axis=0)
        e2b = jnp.concatenate([jnp.broadcast_to(e2_sc[h, pl.ds(t, 1), :], (PEER_TOPK, PEER_NKEYS))
                               for h in range(PEER_HEADS)], axis=0)
        r = jnp.where(s2b >= th_col, e2b, 0.0).astype(BF16)
        pt = jnp.where(sub_j == ir_sc[pl.ds(t, 1), :], wr_sc[pl.ds(t, 1), :], 0.0).astype(BF16)
        return jnp.dot(pt, r, preferred_element_type=F32)

    def token_group(g, carry):
        for half in range(ROUTE_GROUPS_PER_TRIP):
            t0 = pl.multiple_of((g * ROUTE_GROUPS_PER_TRIP + half) * 8, 8)
            tiles = jnp.stack([token(t0 + u) for u in range(8)])
            w_ref[:, pl.ds(t0, 8), :] = jnp.swapaxes(tiles, 0, 1)
        return carry

    lax.fori_loop(0, tn // (8 * ROUTE_GROUPS_PER_TRIP), token_group, 0)


def _route(pq, keys_b):
    ntok = pq.shape[0]
    tn = ROUTE_TILE
    nj = PEER_HEADS * PEER_TOPK
    return pl.pallas_call(
        _route_kernel,
        grid=(ntok // tn,),
        in_specs=[pl.BlockSpec((tn, PEER_NQ), lambda i: (i, 0)),
                  _const_spec((2 * PEER_HEADS, PEER_NKEYS, PEER_HALF))],
        out_specs=pl.BlockSpec((None, PEER_NKEYS, tn, PEER_NKEYS), lambda i: (i, 0, 0, 0)),
        out_shape=jax.ShapeDtypeStruct((ntok // tn, PEER_NKEYS, tn, PEER_NKEYS), F32),
        scratch_shapes=[pltpu.VMEM((PEER_TOPK, tn), F32), pltpu.VMEM((PEER_TOPK, tn), F32),
                        pltpu.VMEM((nj, tn), F32), pltpu.VMEM((nj, tn), F32), pltpu.VMEM((nj, tn), F32),
                        pltpu.VMEM((tn, nj), F32), pltpu.VMEM((tn, nj), F32),
                        pltpu.VMEM((PEER_HEADS, tn, PEER_NKEYS), F32),
                        pltpu.VMEM((PEER_HEADS, tn, PEER_NKEYS), F32)],
        compiler_params=_cparams(("parallel",)),
        name="route",
    )(pq, keys_b)


def _expert_kernel(h_ref, ut_ref, v_ref, w_ref, x1_ref, mod_ref, g_ref, b_ref, o_ref, acc_ref, gate_sc):
    e = pl.program_id(2)

    @pl.when(e == 0)
    def _():
        acc_ref[...] = jnp.zeros_like(acc_ref)

    act = jnp.dot(h_ref[...], ut_ref[...], preferred_element_type=F32)
    rt = w_ref.shape[2]
    for j in range(w_ref.shape[0]):
        for k in range(w_ref.shape[1]):
            a = act[j * rt:(j + 1) * rt, k * PEER_NKEYS:(k + 1) * PEER_NKEYS]
            gate_sc[j * rt:(j + 1) * rt, k * PEER_NKEYS:(k + 1) * PEER_NKEYS] = (
                _gelu_tanh(a) * w_ref[j, k]).astype(BF16)
    acc_ref[...] += jnp.dot(gate_sc[...], v_ref[...], preferred_element_type=F32)

    @pl.when(e == pl.num_programs(2) - 1)
    def _():
        mod = mod_ref[...]
        o_ref[...] = _ln(DN_ALPHA * x1_ref[...] + mod[5:6] * acc_ref[...]) * g_ref[...] + b_ref[...]


def _experts(h2, w4, x1, mod3, ut_b, v_b, ln2_g, ln2_b):
    bsz, seq, _ = x1.shape
    tt = min(EXPERT_TOK_TILE, seq)
    ce = EXPERT_CHUNK
    rt = w4.shape[2]
    nblk = seq // tt
    tok = lambda n: pl.BlockSpec((None, tt, n), lambda b, i, e: (b, i, 0))
    return pl.pallas_call(
        _expert_kernel,
        grid=(bsz, nblk, PEER_EXPERTS // ce),
        in_specs=[tok(D_MODEL),
                  pl.BlockSpec((None, D_MODEL, ce), lambda b, i, e: (e, 0, 0)),
                  pl.BlockSpec((ce, D_MODEL), lambda b, i, e: (e, 0)),
                  pl.BlockSpec((tt // rt, ce // PEER_NKEYS, rt, PEER_NKEYS),
                               lambda b, i, e: (b * nblk + i, e, 0, 0)),
                  pl.BlockSpec((None, tt, D_MODEL), lambda b, i, e: (b, i, 0),
                               pipeline_mode=pl.Buffered(1)),
                  pl.BlockSpec((None, 6, D_MODEL), lambda b, i, e: (b, 0, 0)),
                  pl.BlockSpec((1, D_MODEL), lambda b, i, e: (0, 0)),
                  pl.BlockSpec((1, D_MODEL), lambda b, i, e: (0, 0))],
        out_specs=tok(D_MODEL),
        out_shape=jax.ShapeDtypeStruct((bsz, seq, D_MODEL), F32),
        scratch_shapes=[pltpu.VMEM((tt, D_MODEL), F32), pltpu.VMEM((tt, ce), BF16)],
        compiler_params=_cparams(("parallel", "parallel", "arbitrary")),
        name="expert",
    )(h2, ut_b, v_b, w4, x1, mod3, ln2_g, ln2_b)


def _rope_tables(seq):
    half = ATT_HEAD_DIM // 2
    inv_freq = ROPE_THETA ** (-jnp.arange(half, dtype=F32) * 2.0 / ATT_HEAD_DIM)
    ang = jnp.arange(seq, dtype=F32)[:, None] * inv_freq[None, :]
    cos = jnp.cos(ang)
    sin = jnp.sin(ang)
    zero = jnp.zeros_like(sin)
    reps = LANES // ATT_HEAD_DIM
    cos_t = jnp.tile(jnp.concatenate([cos, cos], axis=1), (1, reps))
    sa_t = jnp.tile(jnp.concatenate([-sin, zero], axis=1), (1, reps))
    sb_t = jnp.tile(jnp.concatenate([zero, sin], axis=1), (1, reps))
    return cos_t, sa_t, sb_t


def _prepare(layer_idx, p):
    lam_init = 0.8 - 0.6 * math.exp(-0.3 * layer_idx)
    lam = (jnp.exp(jnp.sum(p['attn_lambda_q1'].astype(F32) * p['attn_lambda_k1'].astype(F32)))
           - jnp.exp(jnp.sum(p['attn_lambda_q2'].astype(F32) * p['attn_lambda_k2'].astype(F32))) + lam_init)
    row = lambda a: a.astype(F32).reshape(1, -1)
    return dict(
        w_ada=p['w_ada'].astype(F32), b_ada=p['b_ada'].astype(F32),
        w_in=p['w_in'].astype(BF16),
        s5w=_s5_weights(p['s5_lambda_re'], p['s5_lambda_im'], p['s5_log_dt'], p['s5_b_re'],
                        p['s5_b_im'], p['s5_c_re'], p['s5_c_im']),
        s5_d=row(p['s5_d']), w_glu=p['s5_w_glu'].astype(BF16), w_s5_out=p['w_s5_out'].astype(BF16),
        lam=lam.reshape(1).astype(F32),
        subln=row(p['attn_subln_g']) * (1.0 - lam_init),
        w_attn_out=p['w_attn_out'].astype(BF16), w_o=p['w_o'].astype(BF16),
        ln1_g=row(p['ln1_g']), ln1_b=row(p['ln1_b']),
        w_q=p['peer_w_q'].astype(BF16),
        keys=p['peer_keys'].astype(BF16).reshape(2 * PEER_HEADS, PEER_NKEYS, PEER_HALF),
        ut=(p['peer_u'].astype(BF16).reshape(PEER_EXPERTS // EXPERT_CHUNK, EXPERT_CHUNK, D_MODEL)
            .transpose(0, 2, 1)),
        v=p['peer_v'].astype(BF16),
        ln2_g=row(p['ln2_g']), ln2_b=row(p['ln2_b']),
    )


def _encoder_layer(x, c, wts, rope):
    bsz, seq, _ = x.shape
    mod3 = _ada(c.astype(F32), wts['w_ada'], wts['b_ada']).reshape(bsz, 6, D_MODEL)
    u, q, k, v, ga, gb = _inproj(x.astype(F32), mod3, wts['w_in'], *rope)
    y = _s5(u, wts['s5w'])
    o = _attention(wts['lam'], q, k, v, wts['subln'])
    x1, h2, pq = _merge(x.astype(F32), u, y, o, ga, gb, mod3, wts)
    w4 = _route(pq.reshape(bsz * seq, PEER_NQ), wts['keys'])
    x2 = _experts(h2, w4, x1, mod3, wts['ut'], wts['v'], wts['ln2_g'], wts['ln2_b'])
    return x2.astype(x.dtype)


def kernel(x_prompt, x_sample, c_prompt, c_sample, w_ada, b_ada, w_in, s5_lambda_re, s5_lambda_im, s5_log_dt, s5_b_re, s5_b_im, s5_c_re, s5_c_im, s5_d, s5_w_glu, w_s5_out, attn_lambda_q1, attn_lambda_k1, attn_lambda_q2, attn_lambda_k2, attn_subln_g, w_attn_out, w_o, ln1_g, ln1_b, peer_w_q, peer_keys, peer_u, peer_v, ln2_g, ln2_b):
    params = dict(w_ada=w_ada, b_ada=b_ada, w_in=w_in, s5_lambda_re=s5_lambda_re,
                  s5_lambda_im=s5_lambda_im, s5_log_dt=s5_log_dt, s5_b_re=s5_b_re, s5_b_im=s5_b_im,
                  s5_c_re=s5_c_re, s5_c_im=s5_c_im, s5_d=s5_d, s5_w_glu=s5_w_glu, w_s5_out=w_s5_out,
                  attn_lambda_q1=attn_lambda_q1, attn_lambda_k1=attn_lambda_k1,
                  attn_lambda_q2=attn_lambda_q2, attn_lambda_k2=attn_lambda_k2,
                  attn_subln_g=attn_subln_g, w_attn_out=w_attn_out, w_o=w_o, ln1_g=ln1_g, ln1_b=ln1_b,
                  peer_w_q=peer_w_q, peer_keys=peer_keys, peer_u=peer_u, peer_v=peer_v,
                  ln2_g=ln2_g, ln2_b=ln2_b)
    y_prompt, y_sample = x_prompt, x_sample
    rope_p = _rope_tables(x_prompt.shape[1])
    rope_s = _rope_tables(x_sample.shape[1])
    for l in range(DEPTH):
        wts = _prepare(l, {name: a[l] for name, a in params.items()})
        y_prompt = _encoder_layer(y_prompt, c_prompt, wts, rope_p)
        y_sample = _encoder_layer(y_sample, c_sample, wts, rope_s)
    return (y_prompt, y_sample)
```

```python
import functools
import math

import jax
import jax.numpy as jnp
from jax import lax
from jax.experimental import pallas as pl
from jax.experimental.pallas import tpu as pltpu

F32 = jnp.float32
BF16 = jnp.bfloat16
HIGHEST = lax.Precision.HIGHEST

D_MODEL = 1024
DEPTH = 1
S5_WIDTH = D_MODEL // 2
S5_GROUP_CH = 16
S5_GROUPS = S5_WIDTH // S5_GROUP_CH
S5_STATE = 64
S5_CHUNK = 16
S5_ROW = S5_CHUNK * S5_GROUP_CH
ATT_HEADS = 8
ATT_HEAD_DIM = D_MODEL // (2 * ATT_HEADS)
ATT_WIDTH = ATT_HEADS * 2 * ATT_HEAD_DIM
ROPE_THETA = 10000.0
N_IN = S5_WIDTH + 3 * ATT_WIDTH + 2 * D_MODEL
PEER_HEADS = 8
PEER_NKEYS = 128
PEER_EXPERTS = PEER_NKEYS * PEER_NKEYS
PEER_TOPK = 16
PEER_QDIM = 256
PEER_HALF = PEER_QDIM // 2
PEER_NQ = PEER_HEADS * PEER_QDIM
DN_ALPHA = (2 * DEPTH) ** 0.25
LN_EPS = 1e-5

LANES = 128
VMEM_LIMIT_BYTES = 56 * 1024 * 1024
NEG_BIG = -3.0e38
POS_BIG = 3.0e38

IN_TILE = 512
ATT_TILE = 1024
ATT_ROW_CHUNKS = 2
MERGE_TILE = 256
ROUTE_TILE = 128
ROUTE_GROUPS_PER_TRIP = 2
EXPERT_TOK_TILE = 1024
EXPERT_CHUNK = 1024


def _cparams(sem):
    return pltpu.CompilerParams(dimension_semantics=sem, vmem_limit_bytes=VMEM_LIMIT_BYTES)


def _const_spec(shape):
    nd = len(shape)
    return pl.BlockSpec(shape, lambda *_: (0,) * nd, pipeline_mode=pl.Buffered(1))


def _ln(x):
    mu = jnp.mean(x, axis=-1, keepdims=True)
    xc = x - mu
    var = jnp.mean(xc * xc, axis=-1, keepdims=True)
    return xc * lax.rsqrt(var + LN_EPS)


def _gelu_tanh(x):
    return 0.5 * x * (1.0 + jnp.tanh(math.sqrt(2.0 / math.pi) * (x + 0.044715 * (x * x * x))))


def _sigmoid(x):
    return 1.0 / (1.0 + jnp.exp(-x))


def _ada_kernel(c_ref, w_ref, b_ref, o_ref):
    c = c_ref[...]
    s = c * _sigmoid(c)
    o_ref[...] = jnp.dot(s, w_ref[...], precision=HIGHEST, preferred_element_type=F32) + b_ref[...]


def _ada(c, w_ada, b_ada):
    bsz = c.shape[0]
    nblk = w_ada.shape[1] // D_MODEL
    return pl.pallas_call(
        _ada_kernel,
        grid=(nblk,),
        in_specs=[
            pl.BlockSpec((bsz, D_MODEL), lambda j: (0, 0)),
            pl.BlockSpec((D_MODEL, D_MODEL), lambda j: (0, j)),
            pl.BlockSpec((1, D_MODEL), lambda j: (0, j)),
        ],
        out_specs=pl.BlockSpec((bsz, D_MODEL), lambda j: (0, j)),
        out_shape=jax.ShapeDtypeStruct((bsz, nblk * D_MODEL), F32),
        compiler_params=_cparams(("arbitrary",)),
        name="ada",
    )(c, w_ada, b_ada.reshape(1, -1))


def _in_kernel(x_ref, mod_ref, w_ref, cos_ref, sa_ref, sb_ref,
               u_ref, q_ref, k_ref, v_ref, ga_ref, gb_ref):
    mod = mod_ref[...]
    h = _ln(x_ref[...]) * (1.0 + mod[1:2]) + mod[0:1]
    hb = h.astype(BF16)

    def proj(a, b):
        return jnp.dot(hb, w_ref[:, a:b], preferred_element_type=F32)

    o1 = S5_WIDTH
    o2 = o1 + ATT_WIDTH
    o3 = o2 + ATT_WIDTH
    o4 = o3 + ATT_WIDTH
    o5 = o4 + D_MODEL
    u_ref[...] = proj(0, o1).astype(BF16)
    cos, sa, sb = cos_ref[...], sa_ref[...], sb_ref[...]
    half = ATT_HEAD_DIM // 2

    def rope_store(t, o_ref, scale):
        for j in range(ATT_WIDTH // LANES):
            tj = t[:, j * LANES:(j + 1) * LANES]
            r = (tj * cos + pltpu.roll(tj, LANES - half, 1) * sa + pltpu.roll(tj, half, 1) * sb)
            o_ref[:, j * LANES:(j + 1) * LANES] = (r * scale).astype(BF16)

    rope_store(proj(o1, o2), q_ref, ATT_HEAD_DIM ** -0.5 * math.log2(math.e))
    rope_store(proj(o2, o3), k_ref, 1.0)
    v_ref[...] = proj(o3, o4).astype(BF16)
    ga_ref[...] = _sigmoid(proj(o4, o5)).astype(BF16)
    gb_ref[...] = _sigmoid(proj(o5, N_IN)).astype(BF16)


def _inproj(x, mod3, w_in_b, cos, sa, sb):
    bsz, seq, _ = x.shape
    tm = min(IN_TILE, seq)
    tok = lambda n: pl.BlockSpec((None, tm, n), lambda b, i: (b, i, 0))
    rope = pl.BlockSpec((tm, LANES), lambda b, i: (i, 0))
    outs = [(S5_WIDTH, BF16), (ATT_WIDTH, BF16), (ATT_WIDTH, BF16), (ATT_WIDTH, BF16),
            (D_MODEL, BF16), (D_MODEL, BF16)]
    return pl.pallas_call(
        _in_kernel,
        grid=(bsz, seq // tm),
        in_specs=[tok(D_MODEL),
                  pl.BlockSpec((None, 6, D_MODEL), lambda b, i: (b, 0, 0)),
                  _const_spec((D_MODEL, N_IN)), rope, rope, rope],
        out_specs=[tok(n) for n, _ in outs],
        out_shape=[jax.ShapeDtypeStruct((bsz, seq, n), dt) for n, dt in outs],
        compiler_params=_cparams(("parallel", "parallel")),
        name="inproj",
    )(x, mod3, w_in_b, cos, sa, sb)


def _s5_weights(lam_re, lam_im, log_dt, b_re, b_im, c_re, c_im):
    n = S5_CHUNK
    per_dir = []
    for d in range(2):
        lr = lam_re[d].astype(F32)
        li = lam_im[d].astype(F32)
        dt = jnp.exp(log_dt[d].astype(F32))[:, None]
        mag = jnp.exp(lr * dt)
        lbr = mag * jnp.cos(li * dt)
        lbi = mag * jnp.sin(li * dt)
        den = lr * lr + li * li
        nr = lbr - 1.0
        cr = (nr * lr + lbi * li) / den
        ci = (lbi * lr - nr * li) / den
        br = b_re[d].astype(F32)
        bi = b_im[d].astype(F32)
        bbr = cr[..., None] * br - ci[..., None] * bi
        bbi = cr[..., None] * bi + ci[..., None] * br
        pr = [jnp.ones_like(lbr)]
        pi = [jnp.zeros_like(lbi)]
        for _ in range(n):
            pr.append(pr[-1] * lbr - pi[-1] * lbi)
            pi.append(pr[-2] * lbi + pi[-1] * lbr)
        pr = jnp.stack(pr)
        pi = jnp.stack(pi)
        cre = c_re[d].astype(F32)
        cim = c_im[d].astype(F32)
        cpr = cre[None] * pr[:, :, None, :] - cim[None] * pi[:, :, None, :]
        cpi = cre[None] * pi[:, :, None, :] + cim[None] * pr[:, :, None, :]
        kern = (jnp.einsum('ngcp,gpd->ngcd', cpr, bbr, precision=HIGHEST)
                - jnp.einsum('ngcp,gpd->ngcd', cpi, bbi, precision=HIGHEST))
        pbr = pr[..., None] * bbr[None] - pi[..., None] * bbi[None]
        pbi = pr[..., None] * bbi[None] + pi[..., None] * bbr[None]
        per_dir.append(dict(pr=pr, pi=pi, cpr=cpr, cpi=cpi, kern=kern, pbr=pbr, pbi=pbi))

    g, c, p = S5_GROUPS, S5_GROUP_CH, S5_STATE
    s_idx = jnp.arange(n)[:, None]
    t_idx = jnp.arange(n)[None, :]
    tau_f = t_idx - s_idx
    tau_b = s_idx - t_idx
    kf = per_dir[0]['kern'][jnp.clip(tau_f, 0, n)] * (tau_f >= 0)[..., None, None, None]
    kb = per_dir[1]['kern'][jnp.clip(tau_b, 0, n)] * (tau_b >= 0)[..., None, None, None]
    m = (kf + kb).transpose(2, 0, 4, 1, 3).reshape(g, S5_ROW, S5_ROW)

    pow_f = n - 1 - jnp.arange(n)
    pow_b = jnp.arange(n)

    def ws_part(key, d, pows):
        return per_dir[d][key][pows].transpose(1, 0, 3, 2).reshape(g, S5_ROW, p)

    ws = jnp.concatenate([ws_part('pbr', 0, pow_f), ws_part('pbr', 1, pow_b),
                          ws_part('pbi', 0, pow_f), ws_part('pbi', 1, pow_b)], axis=-1)

    out_f = jnp.arange(n) + 1
    out_b = n - jnp.arange(n)

    def wo_part(key, d, pows):
        return per_dir[d][key][pows].transpose(1, 3, 0, 2).reshape(g, p, S5_ROW)

    zero = jnp.zeros((g, p, S5_ROW), F32)
    wof = jnp.concatenate([wo_part('cpr', 0, out_f), zero, -wo_part('cpi', 0, out_f), zero], axis=1)
    wob = jnp.concatenate([zero, wo_part('cpr', 1, out_b), zero, -wo_part('cpi', 1, out_b)], axis=1)
    ar = jnp.concatenate([per_dir[0]['pr'][n], per_dir[1]['pr'][n]], axis=-1)[:, None, :]
    ai = jnp.concatenate([per_dir[0]['pi'][n], per_dir[1]['pi'][n]], axis=-1)[:, None, :]
    return m.astype(BF16), ws.astype(BF16), wof.astype(BF16), wob.astype(BF16), ar, ai


def _s5_kernel(u_ref, m_ref, ws_ref, wof_ref, wob_ref, ar_ref, ai_ref, y_ref,
               s_ref, hf_ref, hb_ref, *, nb, nchunk):
    u = u_ref[...]
    s_ref[...] = jnp.dot(u, ws_ref[...], preferred_element_type=F32)
    ar = ar_ref[...]
    ai = ai_ref[...]
    is_fwd = lax.broadcasted_iota(jnp.int32, (nb, LANES), 1) < S5_STATE

    def step(i, carry):
        h_re, h_im = carry
        rf = pl.multiple_of(i * nb, 8)
        rb = pl.multiple_of((nchunk - 1 - i) * nb, 8)
        hf_ref[pl.ds(rf, nb), 0:LANES] = h_re
        hf_ref[pl.ds(rf, nb), LANES:2 * LANES] = h_im
        hb_ref[pl.ds(rb, nb), 0:LANES] = h_re
        hb_ref[pl.ds(rb, nb), LANES:2 * LANES] = h_im
        s_re = jnp.where(is_fwd, s_ref[pl.ds(rf, nb), 0:LANES], s_ref[pl.ds(rb, nb), 0:LANES])
        s_im = jnp.where(is_fwd, s_ref[pl.ds(rf, nb), LANES:2 * LANES],
                         s_ref[pl.ds(rb, nb), LANES:2 * LANES])
        return (ar * h_re - ai * h_im + s_re, ar * h_im + ai * h_re + s_im)

    zero = jnp.zeros((nb, LANES), F32)
    lax.fori_loop(0, nchunk, step, (zero, zero))
    y = jnp.dot(u, m_ref[...], preferred_element_type=F32)
    y = y + jnp.dot(hf_ref[...].astype(BF16), wof_ref[...], preferred_element_type=F32)
    y = y + jnp.dot(hb_ref[...].astype(BF16), wob_ref[...], preferred_element_type=F32)
    y_ref[...] = y.astype(BF16)


def _s5(u, s5w):
    m, ws, wof, wob, ar, ai = s5w
    bsz, seq, _ = u.shape
    nchunk = seq // S5_CHUNK
    rows = nchunk * bsz
    u2 = (u.reshape(bsz, nchunk, S5_CHUNK, S5_GROUPS, S5_GROUP_CH)
          .transpose(3, 1, 0, 2, 4).reshape(S5_GROUPS, rows, S5_ROW))
    grp = lambda r, c: pl.BlockSpec((None, r, c), lambda g: (g, 0, 0))
    y2 = pl.pallas_call(
        functools.partial(_s5_kernel, nb=bsz, nchunk=nchunk),
        grid=(S5_GROUPS,),
        in_specs=[grp(rows, S5_ROW), grp(S5_ROW, S5_ROW), grp(S5_ROW, S5_ROW),
                  grp(S5_ROW, S5_ROW), grp(S5_ROW, S5_ROW), grp(1, LANES), grp(1, LANES)],
        out_specs=grp(rows, S5_ROW),
        out_shape=jax.ShapeDtypeStruct((S5_GROUPS, rows, S5_ROW), BF16),
        scratch_shapes=[pltpu.VMEM((rows, S5_ROW), F32)] * 3,
        compiler_params=_cparams(("parallel",)),
        name="s5",
    )(u2, m, ws, wof, wob, ar, ai)
    return (y2.reshape(S5_GROUPS, nchunk, bsz, S5_CHUNK, S5_GROUP_CH)
            .transpose(2, 1, 3, 0, 4).reshape(bsz, seq, S5_WIDTH))


def _attn_kernel(lam_ref, q_ref, k_ref, v_ref, g_ref, o_ref, vx_ref):
    @pl.when(pl.program_id(2) == 0)
    def _():
        vx_ref[:, 0:LANES] = v_ref[...]
        vx_ref[:, LANES:2 * LANES] = jnp.ones(v_ref.shape, BF16)

    k = k_ref[...]
    vx = vx_ref[...]
    rows_per_chunk = q_ref.shape[0] // ATT_ROW_CHUNKS

    def scores(qm):
        return lax.dot_general(qm, k, (((1,), (1,)), ((), ())), preferred_element_type=F32)

    def probs(s):
        return jnp.exp2(s - jnp.max(s, axis=-1, keepdims=True)).astype(BF16)

    def weighted(e):
        ol = jnp.dot(e, vx, preferred_element_type=F32)
        return ol[:, 0:LANES] / ol[:, LANES:2 * LANES]

    chains = []
    for c in range(ATT_ROW_CHUNKS):
        rows = slice(c * rows_per_chunk, (c + 1) * rows_per_chunk)
        q = q_ref[rows, :]
        first = lax.broadcasted_iota(jnp.int32, q.shape, 1) < ATT_HEAD_DIM
        zero = jnp.zeros_like(q)
        chains.append((rows, jnp.where(first, q, zero)))
        chains.append((rows, jnp.where(first, zero, q)))

    s_val, e_val, o_val = {}, {}, {}
    for t in range(len(chains) + 2):
        if t < len(chains):
            s_val[t] = scores(chains[t][1])
        if 0 <= t - 1 < len(chains):
            e_val[t - 1] = probs(s_val.pop(t - 1))
        if 0 <= t - 2 < len(chains):
            o_val[t - 2] = weighted(e_val.pop(t - 2))
            if (t - 2) % 2 == 1:
                o = o_val.pop(t - 3) - lam_ref[0] * o_val.pop(t - 2)
                o = o * lax.rsqrt(jnp.mean(o * o, axis=-1, keepdims=True) + LN_EPS) * g_ref[...]
                o_ref[chains[t - 2][0], :] = o.astype(BF16)


def _attention(lam, q, k, v, g_row):
    bsz, seq, _ = q.shape
    tq = min(ATT_TILE, seq)
    return pl.pallas_call(
        _attn_kernel,
        grid=(bsz, ATT_HEADS, seq // tq),
        in_specs=[pl.BlockSpec(memory_space=pltpu.SMEM),
                  pl.BlockSpec((None, tq, LANES), lambda b, h, i: (b, i, h)),
                  pl.BlockSpec((None, seq, LANES), lambda b, h, i: (b, 0, h)),
                  pl.BlockSpec((None, seq, LANES), lambda b, h, i: (b, 0, h)),
                  pl.BlockSpec((1, LANES), lambda b, h, i: (0, 0))],
        out_specs=pl.BlockSpec((None, tq, LANES), lambda b, h, i: (b, i, h)),
        out_shape=jax.ShapeDtypeStruct((bsz, seq, ATT_WIDTH), BF16),
        scratch_shapes=[pltpu.VMEM((seq, 2 * LANES), BF16)],
        compiler_params=_cparams(("parallel", "parallel", "arbitrary")),
        name="attn",
    )(lam, q, k, v, g_row)


def _merge_kernel(x_ref, u_ref, y_ref, o_ref, ga_ref, gb_ref, mod_ref, d_ref, wglu_ref, ws5_ref,
                  watt_ref, wo_ref, ln1g_ref, ln1b_ref, wq_ref, x1_ref, h2_ref, pq_ref):
    mod = mod_ref[...]
    s = _gelu_tanh(u_ref[...].astype(F32) * d_ref[...] + y_ref[...].astype(F32))
    s = s * _sigmoid(jnp.dot(s.astype(BF16), wglu_ref[...], preferred_element_type=F32))
    br_a = jnp.dot(s.astype(BF16), ws5_ref[...], preferred_element_type=F32)
    br_b = jnp.dot(o_ref[...], watt_ref[...], preferred_element_type=F32)
    merged = ga_ref[...].astype(F32) * br_a + gb_ref[...].astype(F32) * br_b
    z = jnp.dot(merged.astype(BF16), wo_ref[...], preferred_element_type=F32)
    x1 = _ln(DN_ALPHA * x_ref[...] + mod[2:3] * z) * ln1g_ref[...] + ln1b_ref[...]
    x1_ref[...] = x1
    h2 = (_ln(x1) * (1.0 + mod[4:5]) + mod[3:4]).astype(BF16)
    h2_ref[...] = h2
    pq_ref[...] = jnp.dot(h2, wq_ref[...], preferred_element_type=F32).astype(BF16)


def _merge(x, u, y, o, ga, gb, mod3, wts):
    bsz, seq, _ = x.shape
    tm = min(MERGE_TILE, seq)
    tok = lambda n: pl.BlockSpec((None, tm, n), lambda b, i: (b, i, 0))
    row = lambda n: _const_spec((1, n))
    return pl.pallas_call(
        _merge_kernel,
        grid=(bsz, seq // tm),
        in_specs=[tok(D_MODEL), tok(S5_WIDTH), tok(S5_WIDTH), tok(ATT_WIDTH), tok(D_MODEL), tok(D_MODEL),
                  pl.BlockSpec((None, 6, D_MODEL), lambda b, i: (b, 0, 0)),
                  row(S5_WIDTH), _const_spec((S5_WIDTH, S5_WIDTH)), _const_spec((S5_WIDTH, D_MODEL)),
                  _const_spec((ATT_WIDTH, D_MODEL)), _const_spec((D_MODEL, D_MODEL)),
                  row(D_MODEL), row(D_MODEL), _const_spec((D_MODEL, PEER_NQ))],
        out_specs=[tok(D_MODEL), tok(D_MODEL), tok(PEER_NQ)],
        out_shape=[jax.ShapeDtypeStruct((bsz, seq, D_MODEL), F32),
                   jax.ShapeDtypeStruct((bsz, seq, D_MODEL), BF16),
                   jax.ShapeDtypeStruct((bsz, seq, PEER_NQ), BF16)],
        compiler_params=_cparams(("parallel", "parallel")),
        name="merge",
    )(x, u, y, o, ga, gb, mod3, wts['s5_d'], wts['w_glu'], wts['w_s5_out'], wts['w_attn_out'],
      wts['w_o'], wts['ln1_g'], wts['ln1_b'], wts['w_q'])


def _route_kernel(pq_ref, keys_ref, w_ref, a_sc, b_sc, th_sc, wt_sc, ix_sc, wr_sc, ir_sc, s2_sc, e2_sc):
    tn = pq_ref.shape[0]
    sub_f = lax.broadcasted_iota(jnp.int32, (PEER_NKEYS, tn), 0).astype(F32)
    row8 = lax.broadcasted_iota(jnp.int32, (8, tn), 0)
    nt_dims = (((1,), (1,)), ((), ()))

    for h in range(PEER_HEADS):
        q1 = pq_ref[:, (2 * h) * PEER_HALF:(2 * h + 1) * PEER_HALF]
        q2 = pq_ref[:, (2 * h + 1) * PEER_HALF:(2 * h + 2) * PEER_HALF]
        st1 = lax.dot_general(keys_ref[2 * h], q1, nt_dims, preferred_element_type=F32)
        st2 = lax.dot_general(keys_ref[2 * h + 1], q2, nt_dims, preferred_element_type=F32)
        s2 = st2.T
        s2_sc[h] = s2
        e2_sc[h] = jnp.exp(s2 - jnp.max(s2, axis=-1, keepdims=True))

        a_vals = []
        for k in range(PEER_TOPK):
            m = jnp.max(st1, axis=0, keepdims=True)
            ix = jnp.min(jnp.where(st1 == m, sub_f, float(PEER_NKEYS)), axis=0, keepdims=True)
            st1 = jnp.where(sub_f == ix, NEG_BIG, st1)
            a_vals.append(m)
            ix_sc[h * PEER_TOPK + k:h * PEER_TOPK + k + 1, :] = ix
        for k in range(PEER_TOPK):
            m = jnp.max(st2, axis=0, keepdims=True)
            st2 = jnp.where(st2 == m, NEG_BIG, st2)
            b_sc[k:k + 1, :] = m
        b_lo = b_sc[0:8, :]
        b_hi = b_sc[8:16, :]

        cands = [a_vals[0] + b_lo, a_vals[0] + b_hi]
        for k in range(1, PEER_TOPK):
            cands.append(jnp.where(row8 < PEER_TOPK // (k + 1), a_vals[k] + b_lo, NEG_BIG))
        m0 = a_vals[0] + b_sc[0:1, :]
        z = jnp.zeros_like(m0)
        tau = m0
        for _ in range(PEER_TOPK):
            tau = jnp.max(functools.reduce(jnp.maximum, cands), axis=0, keepdims=True)
            cands = [jnp.where(c == tau, NEG_BIG, c) for c in cands]
            z = z + jnp.exp(tau - m0)
        inv_z = 1.0 / z
        for k in range(PEER_TOPK):
            th = jnp.where(a_vals[k] + b_lo >= tau, b_lo, POS_BIG)
            if PEER_TOPK // (k + 1) > 8:
                th = jnp.minimum(th, jnp.where(a_vals[k] + b_hi >= tau, b_hi, POS_BIG))
            j = h * PEER_TOPK + k
            th_sc[j:j + 1, :] = jnp.min(th, axis=0, keepdims=True)
            wt_sc[j:j + 1, :] = jnp.exp(a_vals[k] - a_vals[0]) * inv_z

    wr_sc[...] = wt_sc[...].T
    ir_sc[...] = ix_sc[...].T
    lane_i = lax.broadcasted_iota(jnp.int32, (PEER_NKEYS, tn), 1)
    sub_j = lax.broadcasted_iota(jnp.int32, (PEER_NKEYS, PEER_NKEYS), 0).astype(F32)

    def token(t):
        th_col = jnp.sum(jnp.where(lane_i == t, th_sc[...], 0.0), axis=1, keepdims=True)
        s2b = jnp.concatenate([jnp.broadcast_to(s2_sc[h, pl.ds(t, 1), :], (PEER_TOPK, PEER_NKEYS))
                               for h in range(PEER_HEADS)], axis=0)
        e2b = jnp.concatenate([jnp.broadcast_to(e2_sc[h, pl.ds(t, 1), :], (PEER_TOPK, PEER_NKEYS))
                               for h in range(PEER_HEADS)], axis=0)
        r = jnp.where(s2b >= th_col, e2b, 0.0).astype(BF16)
        pt = jnp.where(sub_j == ir_sc[pl.ds(t, 1), :], wr_sc[pl.ds(t, 1), :], 0.0).astype(BF16)
        return jnp.dot(pt, r, preferred_element_type=F32)

    def token_group(g, carry):
        for half in range(ROUTE_GROUPS_PER_TRIP):
            t0 = pl.multiple_of((g * ROUTE_GROUPS_PER_TRIP + half) * 8, 8)
            tiles = jnp.stack([token(t0 + u) for u in range(8)])
            w_ref[:, pl.ds(t0, 8), :] = jnp.swapaxes(tiles, 0, 1)
        return carry

    lax.fori_loop(0, tn // (8 * ROUTE_GROUPS_PER_TRIP), token_group, 0)


def _route(pq, keys_b):
    ntok = pq.shape[0]
    tn = ROUTE_TILE
    nj = PEER_HEADS * PEER_TOPK
    return pl.pallas_call(
        _route_kernel,
        grid=(ntok // tn,),
        in_specs=[pl.BlockSpec((tn, PEER_NQ), lambda i: (i, 0)),
                  _const_spec((2 * PEER_HEADS, PEER_NKEYS, PEER_HALF))],
        out_specs=pl.BlockSpec((None, PEER_NKEYS, tn, PEER_NKEYS), lambda i: (i, 0, 0, 0)),
        out_shape=jax.ShapeDtypeStruct((ntok // tn, PEER_NKEYS, tn, PEER_NKEYS), F32),
        scratch_shapes=[pltpu.VMEM((PEER_TOPK, tn), F32), pltpu.VMEM((PEER_TOPK, tn), F32),
                        pltpu.VMEM((nj, tn), F32), pltpu.VMEM((nj, tn), F32), pltpu.VMEM((nj, tn), F32),
                        pltpu.VMEM((tn, nj), F32), pltpu.VMEM((tn, nj), F32),
                        pltpu.VMEM((PEER_HEADS, tn, PEER_NKEYS), F32),
                        pltpu.VMEM((PEER_HEADS, tn, PEER_NKEYS), F32)],
        compiler_params=_cparams(("parallel",)),
        name="route",
    )(pq, keys_b)


def _expert_kernel(h_ref, ut_ref, v_ref, w_ref, x1_ref, mod_ref, g_ref, b_ref, o_ref, acc_ref, gate_sc):
    e = pl.program_id(2)

    @pl.when(e == 0)
    def _():
        acc_ref[...] = jnp.zeros_like(acc_ref)

    act = jnp.dot(h_ref[...], ut_ref[...], preferred_element_type=F32)
    rt = w_ref.shape[2]
    for j in range(w_ref.shape[0]):
        for k in range(w_ref.shape[1]):
            a = act[j * rt:(j + 1) * rt, k * PEER_NKEYS:(k + 1) * PEER_NKEYS]
            gate_sc[j * rt:(j + 1) * rt, k * PEER_NKEYS:(k + 1) * PEER_NKEYS] = (
                _gelu_tanh(a) * w_ref[j, k]).astype(BF16)
    acc_ref[...] += jnp.dot(gate_sc[...], v_ref[...], preferred_element_type=F32)

    @pl.when(e == pl.num_programs(2) - 1)
    def _():
        mod = mod_ref[...]
        o_ref[...] = _ln(DN_ALPHA * x1_ref[...] + mod[5:6] * acc_ref[...]) * g_ref[...] + b_ref[...]


def _experts(h2, w4, x1, mod3, ut_b, v_b, ln2_g, ln2_b):
    bsz, seq, _ = x1.shape
    tt = min(EXPERT_TOK_TILE, seq)
    ce = EXPERT_CHUNK
    rt = w4.shape[2]
    nblk = seq // tt
    tok = lambda n: pl.BlockSpec((None, tt, n), lambda b, i, e: (b, i, 0))
    return pl.pallas_call(
        _expert_kernel,
        grid=(bsz, nblk, PEER_EXPERTS // ce),
        in_specs=[tok(D_MODEL),
                  pl.BlockSpec((None, D_MODEL, ce), lambda b, i, e: (e, 0, 0)),
                  pl.BlockSpec((ce, D_MODEL), lambda b, i, e: (e, 0)),
                  pl.BlockSpec((tt // rt, ce // PEER_NKEYS, rt, PEER_NKEYS),
                               lambda b, i, e: (b * nblk + i, e, 0, 0)),
                  pl.BlockSpec((None, tt, D_MODEL), lambda b, i, e: (b, i, 0),
                               pipeline_mode=pl.Buffered(1)),
                  pl.BlockSpec((None, 6, D_MODEL), lambda b, i, e: (b, 0, 0)),
                  pl.BlockSpec((1, D_MODEL), lambda b, i, e: (0, 0)),
                  pl.BlockSpec((1, D_MODEL), lambda b, i, e: (0, 0))],
        out_specs=tok(D_MODEL),
        out_shape=jax.ShapeDtypeStruct((bsz, seq, D_MODEL), F32),
        scratch_shapes=[pltpu.VMEM((tt, D_MODEL), F32), pltpu.VMEM((tt, ce), BF16)],
        compiler_params=_cparams(("parallel", "parallel", "arbitrary")),
        name="expert",
    )(h2, ut_b, v_b, w4, x1, mod3, ln2_g, ln2_b)


def _rope_tables(seq):
    half = ATT_HEAD_DIM // 2
    inv_freq = ROPE_THETA ** (-jnp.arange(half, dtype=F32) * 2.0 / ATT_HEAD_DIM)
    ang = jnp.arange(seq, dtype=F32)[:, None] * inv_freq[None, :]
    cos = jnp.cos(ang)
    sin = jnp.sin(ang)
    zero = jnp.zeros_like(sin)
    reps = LANES // ATT_HEAD_DIM
    cos_t = jnp.tile(jnp.concatenate([cos, cos], axis=1), (1, reps))
    sa_t = jnp.tile(jnp.concatenate([-sin, zero], axis=1), (1, reps))
    sb_t = jnp.tile(jnp.concatenate([zero, sin], axis=1), (1, reps))
    return cos_t, sa_t, sb_t


def _prepare(layer_idx, p):
    lam_init = 0.8 - 0.6 * math.exp(-0.3 * layer_idx)
    lam = (jnp.exp(jnp.sum(p['attn_lambda_q1'].astype(F32) * p['attn_lambda_k1'].astype(F32)))
           - jnp.exp(jnp.sum(p['attn_lambda_q2'].astype(F32) * p['attn_lambda_k2'].astype(F32))) + lam_init)
    row = lambda a: a.astype(F32).reshape(1, -1)
    return dict(
        w_ada=p['w_ada'].astype(F32), b_ada=p['b_ada'].astype(F32),
        w_in=p['w_in'].astype(BF16),
        s5w=_s5_weights(p['s5_lambda_re'], p['s5_lambda_im'], p['s5_log_dt'], p['s5_b_re'],
                        p['s5_b_im'], p['s5_c_re'], p['s5_c_im']),
        s5_d=row(p['s5_d']), w_glu=p['s5_w_glu'].astype(BF16), w_s5_out=p['w_s5_out'].astype(BF16),
        lam=lam.reshape(1).astype(F32),
        subln=row(p['attn_subln_g']) * (1.0 - lam_init),
        w_attn_out=p['w_attn_out'].astype(BF16), w_o=p['w_o'].astype(BF16),
        ln1_g=row(p['ln1_g']), ln1_b=row(p['ln1_b']),
        w_q=p['peer_w_q'].astype(BF16),
        keys=p['peer_keys'].astype(BF16).reshape(2 * PEER_HEADS, PEER_NKEYS, PEER_HALF),
        ut=(p['peer_u'].astype(BF16).reshape(PEER_EXPERTS // EXPERT_CHUNK, EXPERT_CHUNK, D_MODEL)
            .transpose(0, 2, 1)),
        v=p['peer_v'].astype(BF16),
        ln2_g=row(p['ln2_g']), ln2_b=row(p['ln2_b']),
    )


def _encoder_layer(x, c, wts, rope):
    bsz, seq, _ = x.shape
    mod3 = _ada(c.astype(F32), wts['w_ada'], wts['b_ada']).reshape(bsz, 6, D_MODEL)
    u, q, k, v, ga, gb = _inproj(x.astype(F32), mod3, wts['w_in'], *rope)
    y = _s5(u, wts['s5w'])
    o = _attention(wts['lam'], q, k, v, wts['subln'])
    x1, h2, pq = _merge(x.astype(F32), u, y, o, ga, gb, mod3, wts)
    w4 = _route(pq.reshape(bsz * seq, PEER_NQ), wts['keys'])
    x2 = _experts(h2, w4, x1, mod3, wts['ut'], wts['v'], wts['ln2_g'], wts['ln2_b'])
    return x2.astype(x.dtype)


def kernel(x_prompt, x_sample, c_prompt, c_sample, w_ada, b_ada, w_in, s5_lambda_re, s5_lambda_im, s5_log_dt, s5_b_re, s5_b_im, s5_c_re, s5_c_im, s5_d, s5_w_glu, w_s5_out, attn_lambda_q1, attn_lambda_k1, attn_lambda_q2, attn_lambda_k2, attn_subln_g, w_attn_out, w_o, ln1_g, ln1_b, peer_w_q, peer_keys, peer_u, peer_v, ln2_g, ln2_b):
    params = dict(w_ada=w_ada, b_ada=b_ada, w_in=w_in, s5_lambda_re=s5_lambda_re,
                  s5_lambda_im=s5_lambda_im, s5_log_dt=s5_log_dt, s5_b_re=s5_b_re, s5_b_im=s5_b_im,
                  s5_c_re=s5_c_re, s5_c_im=s5_c_im, s5_d=s5_d, s5_w_glu=s5_w_glu, w_s5_out=w_s5_out,
                  attn_lambda_q1=attn_lambda_q1, attn_lambda_k1=attn_lambda_k1,
                  attn_lambda_q2=attn_lambda_q2, attn_lambda_k2=attn_lambda_k2,
                  attn_subln_g=attn_subln_g, w_attn_out=w_attn_out, w_o=w_o, ln1_g=ln1_g, ln1_b=ln1_b,
                  peer_w_q=peer_w_q, peer_keys=peer_keys, peer_u=peer_u, peer_v=peer_v,
                  ln2_g=ln2_g, ln2_b=ln2_b)
    y_prompt, y_sample = x_prompt, x_sample
    rope_p = _rope_tables(x_prompt.shape[1])
    rope_s = _rope_tables(x_sample.shape[1])
    for l in range(DEPTH):
        wts = _prepare(l, {name: a[l] for name, a in params.items()})
        y_prompt = _encoder_layer(y_prompt, c_prompt, wts, rope_p)
        y_sample = _encoder_layer(y_sample, c_sample, wts, rope_s)
    return (y_prompt, y_sample)
```

```python
import functools
import math

import jax
import jax.numpy as jnp
from jax import lax
from jax.experimental import pallas as pl
from jax.experimental.pallas import tpu as pltpu

F32 = jnp.float32
BF16 = jnp.bfloat16
HIGHEST = lax.Precision.HIGHEST

D_MODEL = 1024
DEPTH = 1
S5_WIDTH = D_MODEL // 2
S5_GROUP_CH = 16
S5_GROUPS = S5_WIDTH // S5_GROUP_CH
S5_STATE = 64
S5_CHUNK = 16
S5_ROW = S5_CHUNK * S5_GROUP_CH
ATT_HEADS = 8
ATT_HEAD_DIM = D_MODEL // (2 * ATT_HEADS)
ATT_WIDTH = ATT_HEADS * 2 * ATT_HEAD_DIM
ROPE_THETA = 10000.0
N_IN = S5_WIDTH + 3 * ATT_WIDTH + 2 * D_MODEL
PEER_HEADS = 8
PEER_NKEYS = 128
PEER_EXPERTS = PEER_NKEYS * PEER_NKEYS
PEER_TOPK = 16
PEER_QDIM = 256
PEER_HALF = PEER_QDIM // 2
PEER_NQ = PEER_HEADS * PEER_QDIM
DN_ALPHA = (2 * DEPTH) ** 0.25
LN_EPS = 1e-5

LANES = 128
VMEM_LIMIT_BYTES = 56 * 1024 * 1024
NEG_BIG = -3.0e38
POS_BIG = 3.0e38

IN_TILE = 512
ATT_TILE = 1024
ATT_ROW_CHUNKS = 2
MERGE_TILE = 256
ROUTE_TILE = 128
ROUTE_GROUPS_PER_TRIP = 4
EXPERT_TOK_TILE = 1024
EXPERT_CHUNK = 1024


def _merge_exchange_pairs(n):
    pairs = []
    p = 1
    while p < n:
        k = p
        while k >= 1:
            for j in range(k % p, n - k, 2 * k):
                for i in range(min(k, n - j - k)):
                    if (i + j) // (2 * p) == (i + j + k) // (2 * p):
                        pairs.append((i + j, i + j + k))
            k //= 2
        p *= 2
    return pairs


_SORT16_PAIRS = _merge_exchange_pairs(PEER_NKEYS // 8)


def _cparams(sem):
    return pltpu.CompilerParams(dimension_semantics=sem, vmem_limit_bytes=VMEM_LIMIT_BYTES)


def _const_spec(shape):
    nd = len(shape)
    return pl.BlockSpec(shape, lambda *_: (0,) * nd, pipeline_mode=pl.Buffered(1))


def _ln(x):
    mu = jnp.mean(x, axis=-1, keepdims=True)
    xc = x - mu
    var = jnp.mean(xc * xc, axis=-1, keepdims=True)
    return xc * lax.rsqrt(var + LN_EPS)


def _gelu_tanh(x):
    return 0.5 * x * (1.0 + jnp.tanh(math.sqrt(2.0 / math.pi) * (x + 0.044715 * (x * x * x))))


def _sigmoid(x):
    return 1.0 / (1.0 + jnp.exp(-x))


def _ada_kernel(c_ref, w_ref, b_ref, o_ref):
    c = c_ref[...]
    s = c * _sigmoid(c)
    o_ref[...] = jnp.dot(s, w_ref[...], precision=HIGHEST, preferred_element_type=F32) + b_ref[...]


def _ada(c, w_ada, b_ada):
    bsz = c.shape[0]
    nblk = w_ada.shape[1] // D_MODEL
    return pl.pallas_call(
        _ada_kernel,
        grid=(nblk,),
        in_specs=[
            pl.BlockSpec((bsz, D_MODEL), lambda j: (0, 0)),
            pl.BlockSpec((D_MODEL, D_MODEL), lambda j: (0, j)),
            pl.BlockSpec((1, D_MODEL), lambda j: (0, j)),
        ],
        out_specs=pl.BlockSpec((bsz, D_MODEL), lambda j: (0, j)),
        out_shape=jax.ShapeDtypeStruct((bsz, nblk * D_MODEL), F32),
        compiler_params=_cparams(("arbitrary",)),
        name="ada",
    )(c, w_ada, b_ada.reshape(1, -1))


def _in_kernel(x_ref, mod_ref, w_ref, cos_ref, sa_ref, sb_ref,
               u_ref, q_ref, k_ref, v_ref, ga_ref, gb_ref):
    mod = mod_ref[...]
    h = _ln(x_ref[...]) * (1.0 + mod[1:2]) + mod[0:1]
    hb = h.astype(BF16)

    def proj(a, b):
        return jnp.dot(hb, w_ref[:, a:b], preferred_element_type=F32)

    o1 = S5_WIDTH
    o2 = o1 + ATT_WIDTH
    o3 = o2 + ATT_WIDTH
    o4 = o3 + ATT_WIDTH
    o5 = o4 + D_MODEL
    u_ref[...] = proj(0, o1).astype(BF16)
    cos, sa, sb = cos_ref[...], sa_ref[...], sb_ref[...]
    half = ATT_HEAD_DIM // 2

    def rope_store(t, o_ref, scale):
        for j in range(ATT_WIDTH // LANES):
            tj = t[:, j * LANES:(j + 1) * LANES]
            r = (tj * cos + pltpu.roll(tj, LANES - half, 1) * sa + pltpu.roll(tj, half, 1) * sb)
            o_ref[:, j * LANES:(j + 1) * LANES] = (r * scale).astype(BF16)

    rope_store(proj(o1, o2), q_ref, ATT_HEAD_DIM ** -0.5 * math.log2(math.e))
    rope_store(proj(o2, o3), k_ref, 1.0)
    v_ref[...] = proj(o3, o4).astype(BF16)
    ga_ref[...] = _sigmoid(proj(o4, o5)).astype(BF16)
    gb_ref[...] = _sigmoid(proj(o5, N_IN)).astype(BF16)


def _inproj(x, mod3, w_in_b, cos, sa, sb):
    bsz, seq, _ = x.shape
    tm = min(IN_TILE, seq)
    tok = lambda n: pl.BlockSpec((None, tm, n), lambda b, i: (b, i, 0))
    rope = pl.BlockSpec((tm, LANES), lambda b, i: (i, 0))
    outs = [(S5_WIDTH, BF16), (ATT_WIDTH, BF16), (ATT_WIDTH, BF16), (ATT_WIDTH, BF16),
            (D_MODEL, BF16), (D_MODEL, BF16)]
    return pl.pallas_call(
        _in_kernel,
        grid=(bsz, seq // tm),
        in_specs=[tok(D_MODEL),
                  pl.BlockSpec((None, 6, D_MODEL), lambda b, i: (b, 0, 0)),
                  _const_spec((D_MODEL, N_IN)), rope, rope, rope],
        out_specs=[tok(n) for n, _ in outs],
        out_shape=[jax.ShapeDtypeStruct((bsz, seq, n), dt) for n, dt in outs],
        compiler_params=_cparams(("parallel", "parallel")),
        name="inproj",
    )(x, mod3, w_in_b, cos, sa, sb)


def _s5_weights(lam_re, lam_im, log_dt, b_re, b_im, c_re, c_im):
    n = S5_CHUNK
    per_dir = []
    for d in range(2):
        lr = lam_re[d].astype(F32)
        li = lam_im[d].astype(F32)
        dt = jnp.exp(log_dt[d].astype(F32))[:, None]
        mag = jnp.exp(lr * dt)
        lbr = mag * jnp.cos(li * dt)
        lbi = mag * jnp.sin(li * dt)
        den = lr * lr + li * li
        nr = lbr - 1.0
        cr = (nr * lr + lbi * li) / den
        ci = (lbi * lr - nr * li) / den
        br = b_re[d].astype(F32)
        bi = b_im[d].astype(F32)
        bbr = cr[..., None] * br - ci[..., None] * bi
        bbi = cr[..., None] * bi + ci[..., None] * br
        pr = [jnp.ones_like(lbr)]
        pi = [jnp.zeros_like(lbi)]
        for _ in range(n):
            pr.append(pr[-1] * lbr - pi[-1] * lbi)
            pi.append(pr[-2] * lbi + pi[-1] * lbr)
        pr = jnp.stack(pr)
        pi = jnp.stack(pi)
        cre = c_re[d].astype(F32)
        cim = c_im[d].astype(F32)
        cpr = cre[None] * pr[:, :, None, :] - cim[None] * pi[:, :, None, :]
        cpi = cre[None] * pi[:, :, None, :] + cim[None] * pr[:, :, None, :]
        kern = (jnp.einsum('ngcp,gpd->ngcd', cpr, bbr, precision=HIGHEST)
                - jnp.einsum('ngcp,gpd->ngcd', cpi, bbi, precision=HIGHEST))
        pbr = pr[..., None] * bbr[None] - pi[..., None] * bbi[None]
        pbi = pr[..., None] * bbi[None] + pi[..., None] * bbr[None]
        per_dir.append(dict(pr=pr, pi=pi, cpr=cpr, cpi=cpi, kern=kern, pbr=pbr, pbi=pbi))

    g, c, p = S5_GROUPS, S5_GROUP_CH, S5_STATE
    s_idx = jnp.arange(n)[:, None]
    t_idx = jnp.arange(n)[None, :]
    tau_f = t_idx - s_idx
    tau_b = s_idx - t_idx
    kf = per_dir[0]['kern'][jnp.clip(tau_f, 0, n)] * (tau_f >= 0)[..., None, None, None]
    kb = per_dir[1]['kern'][jnp.clip(tau_b, 0, n)] * (tau_b >= 0)[..., None, None, None]
    m = (kf + kb).transpose(2, 0, 4, 1, 3).reshape(g, S5_ROW, S5_ROW)

    pow_f = n - 1 - jnp.arange(n)
    pow_b = jnp.arange(n)

    def ws_part(key, d, pows):
        return per_dir[d][key][pows].transpose(1, 0, 3, 2).reshape(g, S5_ROW, p)

    ws = jnp.concatenate([ws_part('pbr', 0, pow_f), ws_part('pbr', 1, pow_b),
                          ws_part('pbi', 0, pow_f), ws_part('pbi', 1, pow_b)], axis=-1)

    out_f = jnp.arange(n) + 1
    out_b = n - jnp.arange(n)

    def wo_part(key, d, pows):
        return per_dir[d][key][pows].transpose(1, 3, 0, 2).reshape(g, p, S5_ROW)

    zero = jnp.zeros((g, p, S5_ROW), F32)
    wof = jnp.concatenate([wo_part('cpr', 0, out_f), zero, -wo_part('cpi', 0, out_f), zero], axis=1)
    wob = jnp.concatenate([zero, wo_part('cpr', 1, out_b), zero, -wo_part('cpi', 1, out_b)], axis=1)
    ar = jnp.concatenate([per_dir[0]['pr'][n], per_dir[1]['pr'][n]], axis=-1)[:, None, :]
    ai = jnp.concatenate([per_dir[0]['pi'][n], per_dir[1]['pi'][n]], axis=-1)[:, None, :]
    return m.astype(BF16), ws.astype(BF16), wof.astype(BF16), wob.astype(BF16), ar, ai


def _s5_kernel(u_ref, m_ref, ws_ref, wof_ref, wob_ref, ar_ref, ai_ref, y_ref,
               s_ref, hf_ref, hb_ref, *, nb, nchunk):
    u = u_ref[...]
    s_ref[...] = jnp.dot(u, ws_ref[...], preferred_element_type=F32)
    ar = ar_ref[...]
    ai = ai_ref[...]
    is_fwd = lax.broadcasted_iota(jnp.int32, (nb, LANES), 1) < S5_STATE

    def step(i, carry):
        h_re, h_im = carry
        rf = pl.multiple_of(i * nb, 8)
        rb = pl.multiple_of((nchunk - 1 - i) * nb, 8)
        hf_ref[pl.ds(rf, nb), 0:LANES] = h_re
        hf_ref[pl.ds(rf, nb), LANES:2 * LANES] = h_im
        hb_ref[pl.ds(rb, nb), 0:LANES] = h_re
        hb_ref[pl.ds(rb, nb), LANES:2 * LANES] = h_im
        s_re = jnp.where(is_fwd, s_ref[pl.ds(rf, nb), 0:LANES], s_ref[pl.ds(rb, nb), 0:LANES])
        s_im = jnp.where(is_fwd, s_ref[pl.ds(rf, nb), LANES:2 * LANES],
                         s_ref[pl.ds(rb, nb), LANES:2 * LANES])
        return (ar * h_re - ai * h_im + s_re, ar * h_im + ai * h_re + s_im)

    zero = jnp.zeros((nb, LANES), F32)
    lax.fori_loop(0, nchunk, step, (zero, zero))
    y = jnp.dot(u, m_ref[...], preferred_element_type=F32)
    y = y + jnp.dot(hf_ref[...].astype(BF16), wof_ref[...], preferred_element_type=F32)
    y = y + jnp.dot(hb_ref[...].astype(BF16), wob_ref[...], preferred_element_type=F32)
    y_ref[...] = y.astype(BF16)


def _s5(u, s5w):
    m, ws, wof, wob, ar, ai = s5w
    bsz, seq, _ = u.shape
    nchunk = seq // S5_CHUNK
    rows = nchunk * bsz
    u2 = (u.reshape(bsz, nchunk, S5_CHUNK, S5_GROUPS, S5_GROUP_CH)
          .transpose(3, 1, 0, 2, 4).reshape(S5_GROUPS, rows, S5_ROW))
    grp = lambda r, c: pl.BlockSpec((None, r, c), lambda g: (g, 0, 0))
    y2 = pl.pallas_call(
        functools.partial(_s5_kernel, nb=bsz, nchunk=nchunk),
        grid=(S5_GROUPS,),
        in_specs=[grp(rows, S5_ROW), grp(S5_ROW, S5_ROW), grp(S5_ROW, S5_ROW),
                  grp(S5_ROW, S5_ROW), grp(S5_ROW, S5_ROW), grp(1, LANES), grp(1, LANES)],
        out_specs=grp(rows, S5_ROW),
        out_shape=jax.ShapeDtypeStruct((S5_GROUPS, rows, S5_ROW), BF16),
        scratch_shapes=[pltpu.VMEM((rows, S5_ROW), F32)] * 3,
        compiler_params=_cparams(("parallel",)),
        name="s5",
    )(u2, m, ws, wof, wob, ar, ai)
    return (y2.reshape(S5_GROUPS, nchunk, bsz, S5_CHUNK, S5_GROUP_CH)
            .transpose(2, 1, 3, 0, 4).reshape(bsz, seq, S5_WIDTH))


def _attn_kernel(lam_ref, q_ref, k_ref, v_ref, g_ref, o_ref, vx_ref):
    @pl.when(pl.program_id(2) == 0)
    def _():
        vx_ref[:, 0:LANES] = v_ref[...]
        vx_ref[:, LANES:2 * LANES] = jnp.ones(v_ref.shape, BF16)

    k = k_ref[...]
    vx = vx_ref[...]
    rows_per_chunk = q_ref.shape[0] // ATT_ROW_CHUNKS

    def scores(qm):
        return lax.dot_general(qm, k, (((1,), (1,)), ((), ())), preferred_element_type=F32)

    def probs(s):
        return jnp.exp2(s - jnp.max(s, axis=-1, keepdims=True)).astype(BF16)

    def weighted(e):
        ol = jnp.dot(e, vx, preferred_element_type=F32)
        return ol[:, 0:LANES] / ol[:, LANES:2 * LANES]

    chains = []
    for c in range(ATT_ROW_CHUNKS):
        rows = slice(c * rows_per_chunk, (c + 1) * rows_per_chunk)
        q = q_ref[rows, :]
        first = lax.broadcasted_iota(jnp.int32, q.shape, 1) < ATT_HEAD_DIM
        zero = jnp.zeros_like(q)
        chains.append((rows, jnp.where(first, q, zero)))
        chains.append((rows, jnp.where(first, zero, q)))

    s_val, e_val, o_val = {}, {}, {}
    for t in range(len(chains) + 2):
        if t < len(chains):
            s_val[t] = scores(chains[t][1])
        if 0 <= t - 1 < len(chains):
            e_val[t - 1] = probs(s_val.pop(t - 1))
        if 0 <= t - 2 < len(chains):
            o_val[t - 2] = weighted(e_val.pop(t - 2))
            if (t - 2) % 2 == 1:
                o = o_val.pop(t - 3) - lam_ref[0] * o_val.pop(t - 2)
                o = o * lax.rsqrt(jnp.mean(o * o, axis=-1, keepdims=True) + LN_EPS) * g_ref[...]
                o_ref[chains[t - 2][0], :] = o.astype(BF16)


def _attention(lam, q, k, v, g_row):
    bsz, seq, _ = q.shape
    tq = min(ATT_TILE, seq)
    return pl.pallas_call(
        _attn_kernel,
        grid=(bsz, ATT_HEADS, seq // tq),
        in_specs=[pl.BlockSpec(memory_space=pltpu.SMEM),
                  pl.BlockSpec((None, tq, LANES), lambda b, h, i: (b, i, h)),
                  pl.BlockSpec((None, seq, LANES), lambda b, h, i: (b, 0, h)),
                  pl.BlockSpec((None, seq, LANES), lambda b, h, i: (b, 0, h)),
                  pl.BlockSpec((1, LANES), lambda b, h, i: (0, 0))],
        out_specs=pl.BlockSpec((None, tq, LANES), lambda b, h, i: (b, i, h)),
        out_shape=jax.ShapeDtypeStruct((bsz, seq, ATT_WIDTH), BF16),
        scratch_shapes=[pltpu.VMEM((seq, 2 * LANES), BF16)],
        compiler_params=_cparams(("parallel", "parallel", "arbitrary")),
        name="attn",
    )(lam, q, k, v, g_row)


def _merge_kernel(x_ref, u_ref, y_ref, o_ref, ga_ref, gb_ref, mod_ref, d_ref, wglu_ref, ws5_ref,
                  watt_ref, wo_ref, ln1g_ref, ln1b_ref, wq_ref, x1_ref, h2_ref, pq_ref):
    mod = mod_ref[...]
    s = _gelu_tanh(u_ref[...].astype(F32) * d_ref[...] + y_ref[...].astype(F32))
    s = s * _sigmoid(jnp.dot(s.astype(BF16), wglu_ref[...], preferred_element_type=F32))
    br_a = jnp.dot(s.astype(BF16), ws5_ref[...], preferred_element_type=F32)
    br_b = jnp.dot(o_ref[...], watt_ref[...], preferred_element_type=F32)
    merged = ga_ref[...].astype(F32) * br_a + gb_ref[...].astype(F32) * br_b
    z = jnp.dot(merged.astype(BF16), wo_ref[...], preferred_element_type=F32)
    x1 = _ln(DN_ALPHA * x_ref[...] + mod[2:3] * z) * ln1g_ref[...] + ln1b_ref[...]
    x1_ref[...] = x1
    h2 = (_ln(x1) * (1.0 + mod[4:5]) + mod[3:4]).astype(BF16)
    h2_ref[...] = h2
    pq_ref[...] = jnp.dot(h2, wq_ref[...], preferred_element_type=F32).astype(BF16)


def _merge(x, u, y, o, ga, gb, mod3, wts):
    bsz, seq, _ = x.shape
    tm = min(MERGE_TILE, seq)
    tok = lambda n: pl.BlockSpec((None, tm, n), lambda b, i: (b, i, 0))
    row = lambda n: _const_spec((1, n))
    return pl.pallas_call(
        _merge_kernel,
        grid=(bsz, seq // tm),
        in_specs=[tok(D_MODEL), tok(S5_WIDTH), tok(S5_WIDTH), tok(ATT_WIDTH), tok(D_MODEL), tok(D_MODEL),
                  pl.BlockSpec((None, 6, D_MODEL), lambda b, i: (b, 0, 0)),
                  row(S5_WIDTH), _const_spec((S5_WIDTH, S5_WIDTH)), _const_spec((S5_WIDTH, D_MODEL)),
                  _const_spec((ATT_WIDTH, D_MODEL)), _const_spec((D_MODEL, D_MODEL)),
                  row(D_MODEL), row(D_MODEL), _const_spec((D_MODEL, PEER_NQ))],
        out_specs=[tok(D_MODEL), tok(D_MODEL), tok(PEER_NQ)],
        out_shape=[jax.ShapeDtypeStruct((bsz, seq, D_MODEL), F32),
                   jax.ShapeDtypeStruct((bsz, seq, D_MODEL), BF16),
                   jax.ShapeDtypeStruct((bsz, seq, PEER_NQ), BF16)],
        compiler_params=_cparams(("parallel", "parallel")),
        name="merge",
    )(x, u, y, o, ga, gb, mod3, wts['s5_d'], wts['w_glu'], wts['w_s5_out'], wts['w_attn_out'],
      wts['w_o'], wts['ln1_g'], wts['ln1_b'], wts['w_q'])


def _route_kernel(pq_ref, keys_ref, w_ref, a_sc, b_sc, th_sc, wt_sc, ix_sc, wr_sc, ir_sc, s2_sc, e2_sc):
    tn = pq_ref.shape[0]
    row8 = lax.broadcasted_iota(jnp.int32, (8, tn), 0)
    row8_f = row8.astype(F32)
    nt_dims = (((1,), (1,)), ((), ()))
    nv = PEER_NKEYS // 8

    def sorted_slabs(st, with_index):
        vals = [st[8 * v:8 * v + 8, :] for v in range(nv)]
        idxs = [row8_f + float(8 * v) for v in range(nv)] if with_index else None
        for lo, hi in _SORT16_PAIRS:
            a, b = vals[lo], vals[hi]
            vals[lo], vals[hi] = jnp.maximum(a, b), jnp.minimum(a, b)
            if with_index:
                keep = a >= b
                ia, ib = idxs[lo], idxs[hi]
                idxs[lo], idxs[hi] = jnp.where(keep, ia, ib), jnp.where(keep, ib, ia)
        return vals, idxs

    def pop_heads(vals, idxs, pick, k):
        for d in range(PEER_TOPK - 1 - k):
            vals[d] = jnp.where(pick, vals[d + 1], vals[d])
            if idxs is not None:
                idxs[d] = jnp.where(pick, idxs[d + 1], idxs[d])

    for h in range(PEER_HEADS):
        q1 = pq_ref[:, (2 * h) * PEER_HALF:(2 * h + 1) * PEER_HALF]
        q2 = pq_ref[:, (2 * h + 1) * PEER_HALF:(2 * h + 2) * PEER_HALF]
        st1 = lax.dot_general(keys_ref[2 * h], q1, nt_dims, preferred_element_type=F32)
        st2 = lax.dot_general(keys_ref[2 * h + 1], q2, nt_dims, preferred_element_type=F32)
        s2 = st2.T
        s2_sc[h] = s2
        e2_sc[h] = jnp.exp(s2 - jnp.max(s2, axis=-1, keepdims=True))

        vals, idxs = sorted_slabs(st1, True)
        a_vals = []
        for k in range(PEER_TOPK):
            m = jnp.max(vals[0], axis=0, keepdims=True)
            ix = jnp.min(jnp.where(vals[0] == m, idxs[0], float(PEER_NKEYS)), axis=0, keepdims=True)
            pop_heads(vals, idxs, idxs[0] == ix, k)
            a_vals.append(m)
            a_sc[k:k + 1, :] = m
            ix_sc[h * PEER_TOPK + k:h * PEER_TOPK + k + 1, :] = ix
        vals, _ = sorted_slabs(st2, False)
        for k in range(PEER_TOPK):
            m = jnp.max(vals[0], axis=0, keepdims=True)
            pop_heads(vals, None, vals[0] == m, k)
            b_sc[k:k + 1, :] = m
        b_lo = b_sc[0:8, :]
        b_hi = b_sc[8:16, :]
        b0 = b_sc[0:1, :]

        shift = lambda x, n: pltpu.roll(x, n, 0)
        cands = [
            a_vals[0] + b_lo,
            a_vals[0] + b_hi,
            a_vals[1] + b_lo,
            jnp.where(row8 < 5, a_vals[2] + b_lo, a_vals[4] + shift(b_lo, 5)),
            jnp.where(row8 < 4, a_vals[3] + b_lo,
                      jnp.where(row8 < 6, a_vals[5] + shift(b_lo, 4), a_vals[6] + shift(b_lo, 6))),
            jnp.where(row8 < 2, a_vals[7] + b_lo, a_sc[6:14, :] + b0),
            jnp.where(row8 < 2, shift(a_sc[8:16, :], 2) + b0, NEG_BIG),
        ]
        m0 = a_vals[0] + b0
        z = jnp.zeros_like(m0)
        tau = m0
        for _ in range(PEER_TOPK):
            tau = jnp.max(functools.reduce(jnp.maximum, cands), axis=0, keepdims=True)
            cands = [jnp.where(c == tau, NEG_BIG, c) for c in cands]
            z = z + jnp.exp(tau - m0)
        inv_z = 1.0 / z
        for k in range(PEER_TOPK):
            th = jnp.where(a_vals[k] + b_lo >= tau, b_lo, POS_BIG)
            if PEER_TOPK // (k + 1) > 8:
                th = jnp.minimum(th, jnp.where(a_vals[k] + b_hi >= tau, b_hi, POS_BIG))
            j = h * PEER_TOPK + k
            th_sc[j:j + 1, :] = jnp.min(th, axis=0, keepdims=True)
            wt_sc[j:j + 1, :] = jnp.exp(a_vals[k] - a_vals[0]) * inv_z

    wr_sc[...] = wt_sc[...].T
    ir_sc[...] = ix_sc[...].T
    lane_i = lax.broadcasted_iota(jnp.int32, (PEER_NKEYS, tn), 1)
    sub_j = lax.broadcasted_iota(jnp.int32, (PEER_NKEYS, PEER_NKEYS), 0).astype(F32)

    def token(t):
        th_col = jnp.sum(jnp.where(lane_i == t, th_sc[...], 0.0), axis=1, keepdims=True)
        s2b = jnp.concatenate([jnp.broadcast_to(s2_sc[h, pl.ds(t, 1), :], (PEER_TOPK, PEER_NKEYS))
                               for h in range(PEER_HEADS)], axis=0)
        e2b = jnp.concatenate([jnp.broadcast_to(e2_sc[h, pl.ds(t, 1), :], (PEER_TOPK, PEER_NKEYS))
                               for h in range(PEER_HEADS)], axis=0)
        r = jnp.where(s2b >= th_col, e2b, 0.0).astype(BF16)
        pt = jnp.where(sub_j == ir_sc[pl.ds(t, 1), :], wr_sc[pl.ds(t, 1), :], 0.0).astype(BF16)
        return jnp.dot(pt, r, preferred_element_type=F32)

    def token_group(g, carry):
        for half in range(ROUTE_GROUPS_PER_TRIP):
            t0 = pl.multiple_of((g * ROUTE_GROUPS_PER_TRIP + half) * 8, 8)
            tiles = jnp.stack([token(t0 + u) for u in range(8)])
            w_ref[:, pl.ds(t0, 8), :] = jnp.swapaxes(tiles, 0, 1)
        return carry

    lax.fori_loop(0, tn // (8 * ROUTE_GROUPS_PER_TRIP), token_group, 0)


def _route(pq, keys_b):
    ntok = pq.shape[0]
    tn = ROUTE_TILE
    nj = PEER_HEADS * PEER_TOPK
    return pl.pallas_call(
        _route_kernel,
        grid=(ntok // tn,),
        in_specs=[pl.BlockSpec((tn, PEER_NQ), lambda i: (i, 0)),
                  _const_spec((2 * PEER_HEADS, PEER_NKEYS, PEER_HALF))],
        out_specs=pl.BlockSpec((None, PEER_NKEYS, tn, PEER_NKEYS), lambda i: (i, 0, 0, 0)),
        out_shape=jax.ShapeDtypeStruct((ntok // tn, PEER_NKEYS, tn, PEER_NKEYS), F32),
        scratch_shapes=[pltpu.VMEM((PEER_TOPK, tn), F32), pltpu.VMEM((PEER_TOPK, tn), F32),
                        pltpu.VMEM((nj, tn), F32), pltpu.VMEM((nj, tn), F32), pltpu.VMEM((nj, tn), F32),
                        pltpu.VMEM((tn, nj), F32), pltpu.VMEM((tn, nj), F32),
                        pltpu.VMEM((PEER_HEADS, tn, PEER_NKEYS), F32),
                        pltpu.VMEM((PEER_HEADS, tn, PEER_NKEYS), F32)],
        compiler_params=_cparams(("parallel",)),
        name="route",
    )(pq, keys_b)


def _expert_kernel(h_ref, ut_ref, v_ref, w_ref, x1_ref, mod_ref, g_ref, b_ref, o_ref, acc_ref, gate_sc):
    e = pl.program_id(2)

    @pl.when(e == 0)
    def _():
        acc_ref[...] = jnp.zeros_like(acc_ref)

    act = jnp.dot(h_ref[...], ut_ref[...], preferred_element_type=F32)
    rt = w_ref.shape[2]
    for j in range(w_ref.shape[0]):
        for k in range(w_ref.shape[1]):
            a = act[j * rt:(j + 1) * rt, k * PEER_NKEYS:(k + 1) * PEER_NKEYS]
            gate_sc[j * rt:(j + 1) * rt, k * PEER_NKEYS:(k + 1) * PEER_NKEYS] = (
                _gelu_tanh(a) * w_ref[j, k]).astype(BF16)
    acc_ref[...] += jnp.dot(gate_sc[...], v_ref[...], preferred_element_type=F32)

    @pl.when(e == pl.num_programs(2) - 1)
    def _():
        mod = mod_ref[...]
        o_ref[...] = _ln(DN_ALPHA * x1_ref[...] + mod[5:6] * acc_ref[...]) * g_ref[...] + b_ref[...]


def _experts(h2, w4, x1, mod3, ut_b, v_b, ln2_g, ln2_b):
    bsz, seq, _ = x1.shape
    tt = min(EXPERT_TOK_TILE, seq)
    ce = EXPERT_CHUNK
    rt = w4.shape[2]
    nblk = seq // tt
    tok = lambda n: pl.BlockSpec((None, tt, n), lambda b, i, e: (b, i, 0))
    return pl.pallas_call(
        _expert_kernel,
        grid=(bsz, nblk, PEER_EXPERTS // ce),
        in_specs=[tok(D_MODEL),
                  pl.BlockSpec((None, D_MODEL, ce), lambda b, i, e: (e, 0, 0)),
                  pl.BlockSpec((ce, D_MODEL), lambda b, i, e: (e, 0)),
                  pl.BlockSpec((tt // rt, ce // PEER_NKEYS, rt, PEER_NKEYS),
                               lambda b, i, e: (b * nblk + i, e, 0, 0)),
                  pl.BlockSpec((None, tt, D_MODEL), lambda b, i, e: (b, i, 0),
                               pipeline_mode=pl.Buffered(1)),
                  pl.BlockSpec((None, 6, D_MODEL), lambda b, i, e: (b, 0, 0)),
                  pl.BlockSpec((1, D_MODEL), lambda b, i, e: (0, 0)),
                  pl.BlockSpec((1, D_MODEL), lambda b, i, e: (0, 0))],
        out_specs=tok(D_MODEL),
        out_shape=jax.ShapeDtypeStruct((bsz, seq, D_MODEL), F32),
        scratch_shapes=[pltpu.VMEM((tt, D_MODEL), F32), pltpu.VMEM((tt, ce), BF16)],
        compiler_params=_cparams(("parallel", "parallel", "arbitrary")),
        name="expert",
    )(h2, ut_b, v_b, w4, x1, mod3, ln2_g, ln2_b)


def _rope_tables(seq):
    half = ATT_HEAD_DIM // 2
    inv_freq = ROPE_THETA ** (-jnp.arange(half, dtype=F32) * 2.0 / ATT_HEAD_DIM)
    ang = jnp.arange(seq, dtype=F32)[:, None] * inv_freq[None, :]
    cos = jnp.cos(ang)
    sin = jnp.sin(ang)
    zero = jnp.zeros_like(sin)
    reps = LANES // ATT_HEAD_DIM
    cos_t = jnp.tile(jnp.concatenate([cos, cos], axis=1), (1, reps))
    sa_t = jnp.tile(jnp.concatenate([-sin, zero], axis=1), (1, reps))
    sb_t = jnp.tile(jnp.concatenate([zero, sin], axis=1), (1, reps))
    return cos_t, sa_t, sb_t


def _prepare(layer_idx, p):
    lam_init = 0.8 - 0.6 * math.exp(-0.3 * layer_idx)
    lam = (jnp.exp(jnp.sum(p['attn_lambda_q1'].astype(F32) * p['attn_lambda_k1'].astype(F32)))
           - jnp.exp(jnp.sum(p['attn_lambda_q2'].astype(F32) * p['attn_lambda_k2'].astype(F32))) + lam_init)
    row = lambda a: a.astype(F32).reshape(1, -1)
    return dict(
        w_ada=p['w_ada'].astype(F32), b_ada=p['b_ada'].astype(F32),
        w_in=p['w_in'].astype(BF16),
        s5w=_s5_weights(p['s5_lambda_re'], p['s5_lambda_im'], p['s5_log_dt'], p['s5_b_re'],
                        p['s5_b_im'], p['s5_c_re'], p['s5_c_im']),
        s5_d=row(p['s5_d']), w_glu=p['s5_w_glu'].astype(BF16), w_s5_out=p['w_s5_out'].astype(BF16),
        lam=lam.reshape(1).astype(F32),
        subln=row(p['attn_subln_g']) * (1.0 - lam_init),
        w_attn_out=p['w_attn_out'].astype(BF16), w_o=p['w_o'].astype(BF16),
        ln1_g=row(p['ln1_g']), ln1_b=row(p['ln1_b']),
        w_q=p['peer_w_q'].astype(BF16),
        keys=p['peer_keys'].astype(BF16).reshape(2 * PEER_HEADS, PEER_NKEYS, PEER_HALF),
        ut=(p['peer_u'].astype(BF16).reshape(PEER_EXPERTS // EXPERT_CHUNK, EXPERT_CHUNK, D_MODEL)
            .transpose(0, 2, 1)),
        v=p['peer_v'].astype(BF16),
        ln2_g=row(p['ln2_g']), ln2_b=row(p['ln2_b']),
    )


def _encoder_layer(x, c, wts, rope):
    bsz, seq, _ = x.shape
    mod3 = _ada(c.astype(F32), wts['w_ada'], wts['b_ada']).reshape(bsz, 6, D_MODEL)
    u, q, k, v, ga, gb = _inproj(x.astype(F32), mod3, wts['w_in'], *rope)
    y = _s5(u, wts['s5w'])
    o = _attention(wts['lam'], q, k, v, wts['subln'])
    x1, h2, pq = _merge(x.astype(F32), u, y, o, ga, gb, mod3, wts)
    w4 = _route(pq.reshape(bsz * seq, PEER_NQ), wts['keys'])
    x2 = _experts(h2, w4, x1, mod3, wts['ut'], wts['v'], wts['ln2_g'], wts['ln2_b'])
    return x2.astype(x.dtype)


def kernel(x_prompt, x_sample, c_prompt, c_sample, w_ada, b_ada, w_in, s5_lambda_re, s5_lambda_im, s5_log_dt, s5_b_re, s5_b_im, s5_c_re, s5_c_im, s5_d, s5_w_glu, w_s5_out, attn_lambda_q1, attn_lambda_k1, attn_lambda_q2, attn_lambda_k2, attn_subln_g, w_attn_out, w_o, ln1_g, ln1_b, peer_w_q, peer_keys, peer_u, peer_v, ln2_g, ln2_b):
    params = dict(w_ada=w_ada, b_ada=b_ada, w_in=w_in, s5_lambda_re=s5_lambda_re,
                  s5_lambda_im=s5_lambda_im, s5_log_dt=s5_log_dt, s5_b_re=s5_b_re, s5_b_im=s5_b_im,
                  s5_c_re=s5_c_re, s5_c_im=s5_c_im, s5_d=s5_d, s5_w_glu=s5_w_glu, w_s5_out=w_s5_out,
                  attn_lambda_q1=attn_lambda_q1, attn_lambda_k1=attn_lambda_k1,
                  attn_lambda_q2=attn_lambda_q2, attn_lambda_k2=attn_lambda_k2,
                  attn_subln_g=attn_subln_g, w_attn_out=w_attn_out, w_o=w_o, ln1_g=ln1_g, ln1_b=ln1_b,
                  peer_w_q=peer_w_q, peer_keys=peer_keys, peer_u=peer_u, peer_v=peer_v,
                  ln2_g=ln2_g, ln2_b=ln2_b)
    y_prompt, y_sample = x_prompt, x_sample
    rope_p = _rope_tables(x_prompt.shape[1])
    rope_s = _rope_tables(x_sample.shape[1])
    for l in range(DEPTH):
        wts = _prepare(l, {name: a[l] for name, a in params.items()})
        y_prompt = _encoder_layer(y_prompt, c_prompt, wts, rope_p)
        y_sample = _encoder_layer(y_sample, c_sample, wts, rope_s)
    return (y_prompt, y_sample)
```

```python
import functools
import math

import jax
import jax.numpy as jnp
from jax import lax
from jax.experimental import pallas as pl
from jax.experimental.pallas import tpu as pltpu

F32 = jnp.float32
BF16 = jnp.bfloat16
HIGHEST = lax.Precision.HIGHEST

D_MODEL = 1024
DEPTH = 1
S5_WIDTH = D_MODEL // 2
S5_GROUP_CH = 16
S5_GROUPS = S5_WIDTH // S5_GROUP_CH
S5_STATE = 64
S5_CHUNK = 16
S5_ROW = S5_CHUNK * S5_GROUP_CH
ATT_HEADS = 8
ATT_HEAD_DIM = D_MODEL // (2 * ATT_HEADS)
ATT_WIDTH = ATT_HEADS * 2 * ATT_HEAD_DIM
ROPE_THETA = 10000.0
N_IN = S5_WIDTH + 3 * ATT_WIDTH + 2 * D_MODEL
PEER_HEADS = 8
PEER_NKEYS = 128
PEER_EXPERTS = PEER_NKEYS * PEER_NKEYS
PEER_TOPK = 16
PEER_QDIM = 256
PEER_HALF = PEER_QDIM // 2
PEER_NQ = PEER_HEADS * PEER_QDIM
DN_ALPHA = (2 * DEPTH) ** 0.25
LN_EPS = 1e-5

LANES = 128
VMEM_LIMIT_BYTES = 56 * 1024 * 1024
NEG_BIG = -3.0e38
POS_BIG = 3.0e38

IN_TILE = 512
ATT_TILE = 1024
ATT_ROW_CHUNKS = 2
MERGE_TILE = 256
ROUTE_TILE = 128
ROUTE_PAIRS_PER_TRIP = 2
EXPERT_TOK_TILE = 1024
EXPERT_CHUNK = 1024


def _merge_exchange_pairs(n):
    pairs = []
    p = 1
    while p < n:
        k = p
        while k >= 1:
            for j in range(k % p, n - k, 2 * k):
                for i in range(min(k, n - j - k)):
                    if (i + j) // (2 * p) == (i + j + k) // (2 * p):
                        pairs.append((i + j, i + j + k))
            k //= 2
        p *= 2
    return pairs


_SORT16_PAIRS = _merge_exchange_pairs(PEER_NKEYS // 8)


def _cparams(sem):
    return pltpu.CompilerParams(dimension_semantics=sem, vmem_limit_bytes=VMEM_LIMIT_BYTES)


def _const_spec(shape):
    nd = len(shape)
    return pl.BlockSpec(shape, lambda *_: (0,) * nd, pipeline_mode=pl.Buffered(1))


def _ln(x):
    mu = jnp.mean(x, axis=-1, keepdims=True)
    xc = x - mu
    var = jnp.mean(xc * xc, axis=-1, keepdims=True)
    return xc * lax.rsqrt(var + LN_EPS)


def _gelu_tanh(x):
    return 0.5 * x * (1.0 + jnp.tanh(math.sqrt(2.0 / math.pi) * (x + 0.044715 * (x * x * x))))


def _sigmoid(x):
    return 1.0 / (1.0 + jnp.exp(-x))


def _ada_kernel(c_ref, w_ref, b_ref, o_ref):
    c = c_ref[...]
    s = c * _sigmoid(c)
    o_ref[...] = jnp.dot(s, w_ref[...], precision=HIGHEST, preferred_element_type=F32) + b_ref[...]


def _ada(c, w_ada, b_ada):
    bsz = c.shape[0]
    nblk = w_ada.shape[1] // D_MODEL
    return pl.pallas_call(
        _ada_kernel,
        grid=(nblk,),
        in_specs=[
            pl.BlockSpec((bsz, D_MODEL), lambda j: (0, 0)),
            pl.BlockSpec((D_MODEL, D_MODEL), lambda j: (0, j)),
            pl.BlockSpec((1, D_MODEL), lambda j: (0, j)),
        ],
        out_specs=pl.BlockSpec((bsz, D_MODEL), lambda j: (0, j)),
        out_shape=jax.ShapeDtypeStruct((bsz, nblk * D_MODEL), F32),
        compiler_params=_cparams(("arbitrary",)),
        name="ada",
    )(c, w_ada, b_ada.reshape(1, -1))


def _in_kernel(x_ref, mod_ref, w_ref, cos_ref, sa_ref, sb_ref,
               u_ref, q_ref, k_ref, v_ref, ga_ref, gb_ref):
    mod = mod_ref[...]
    h = _ln(x_ref[...]) * (1.0 + mod[1:2]) + mod[0:1]
    hb = h.astype(BF16)

    def proj(a, b):
        return jnp.dot(hb, w_ref[:, a:b], preferred_element_type=F32)

    o1 = S5_WIDTH
    o2 = o1 + ATT_WIDTH
    o3 = o2 + ATT_WIDTH
    o4 = o3 + ATT_WIDTH
    o5 = o4 + D_MODEL
    u_ref[...] = proj(0, o1).astype(BF16)
    cos, sa, sb = cos_ref[...], sa_ref[...], sb_ref[...]
    half = ATT_HEAD_DIM // 2

    def rope_store(t, o_ref, scale):
        for j in range(ATT_WIDTH // LANES):
            tj = t[:, j * LANES:(j + 1) * LANES]
            r = (tj * cos + pltpu.roll(tj, LANES - half, 1) * sa + pltpu.roll(tj, half, 1) * sb)
            o_ref[:, j * LANES:(j + 1) * LANES] = (r * scale).astype(BF16)

    rope_store(proj(o1, o2), q_ref, ATT_HEAD_DIM ** -0.5 * math.log2(math.e))
    rope_store(proj(o2, o3), k_ref, 1.0)
    v_ref[...] = proj(o3, o4).astype(BF16)
    ga_ref[...] = _sigmoid(proj(o4, o5)).astype(BF16)
    gb_ref[...] = _sigmoid(proj(o5, N_IN)).astype(BF16)


def _inproj(x, mod3, w_in_b, cos, sa, sb):
    bsz, seq, _ = x.shape
    tm = min(IN_TILE, seq)
    tok = lambda n: pl.BlockSpec((None, tm, n), lambda b, i: (b, i, 0))
    rope = pl.BlockSpec((tm, LANES), lambda b, i: (i, 0))
    outs = [(S5_WIDTH, BF16), (ATT_WIDTH, BF16), (ATT_WIDTH, BF16), (ATT_WIDTH, BF16),
            (D_MODEL, BF16), (D_MODEL, BF16)]
    return pl.pallas_call(
        _in_kernel,
        grid=(bsz, seq // tm),
        in_specs=[tok(D_MODEL),
                  pl.BlockSpec((None, 6, D_MODEL), lambda b, i: (b, 0, 0)),
                  _const_spec((D_MODEL, N_IN)), rope, rope, rope],
        out_specs=[tok(n) for n, _ in outs],
        out_shape=[jax.ShapeDtypeStruct((bsz, seq, n), dt) for n, dt in outs],
        compiler_params=_cparams(("parallel", "parallel")),
        name="inproj",
    )(x, mod3, w_in_b, cos, sa, sb)


def _s5_weights(lam_re, lam_im, log_dt, b_re, b_im, c_re, c_im):
    n = S5_CHUNK
    per_dir = []
    for d in range(2):
        lr = lam_re[d].astype(F32)
        li = lam_im[d].astype(F32)
        dt = jnp.exp(log_dt[d].astype(F32))[:, None]
        mag = jnp.exp(lr * dt)
        lbr = mag * jnp.cos(li * dt)
        lbi = mag * jnp.sin(li * dt)
        den = lr * lr + li * li
        nr = lbr - 1.0
        cr = (nr * lr + lbi * li) / den
        ci = (lbi * lr - nr * li) / den
        br = b_re[d].astype(F32)
        bi = b_im[d].astype(F32)
        bbr = cr[..., None] * br - ci[..., None] * bi
        bbi = cr[..., None] * bi + ci[..., None] * br
        pr = [jnp.ones_like(lbr)]
        pi = [jnp.zeros_like(lbi)]
        for _ in range(n):
            pr.append(pr[-1] * lbr - pi[-1] * lbi)
            pi.append(pr[-2] * lbi + pi[-1] * lbr)
        pr = jnp.stack(pr)
        pi = jnp.stack(pi)
        cre = c_re[d].astype(F32)
        cim = c_im[d].astype(F32)
        cpr = cre[None] * pr[:, :, None, :] - cim[None] * pi[:, :, None, :]
        cpi = cre[None] * pi[:, :, None, :] + cim[None] * pr[:, :, None, :]
        kern = (jnp.einsum('ngcp,gpd->ngcd', cpr, bbr, precision=HIGHEST)
                - jnp.einsum('ngcp,gpd->ngcd', cpi, bbi, precision=HIGHEST))
        pbr = pr[..., None] * bbr[None] - pi[..., None] * bbi[None]
        pbi = pr[..., None] * bbi[None] + pi[..., None] * bbr[None]
        per_dir.append(dict(pr=pr, pi=pi, cpr=cpr, cpi=cpi, kern=kern, pbr=pbr, pbi=pbi))

    g, c, p = S5_GROUPS, S5_GROUP_CH, S5_STATE
    s_idx = jnp.arange(n)[:, None]
    t_idx = jnp.arange(n)[None, :]
    tau_f = t_idx - s_idx
    tau_b = s_idx - t_idx
    kf = per_dir[0]['kern'][jnp.clip(tau_f, 0, n)] * (tau_f >= 0)[..., None, None, None]
    kb = per_dir[1]['kern'][jnp.clip(tau_b, 0, n)] * (tau_b >= 0)[..., None, None, None]
    m = (kf + kb).transpose(2, 0, 4, 1, 3).reshape(g, S5_ROW, S5_ROW)

    pow_f = n - 1 - jnp.arange(n)
    pow_b = jnp.arange(n)

    def ws_part(key, d, pows):
        return per_dir[d][key][pows].transpose(1, 0, 3, 2).reshape(g, S5_ROW, p)

    ws = jnp.concatenate([ws_part('pbr', 0, pow_f), ws_part('pbr', 1, pow_b),
                          ws_part('pbi', 0, pow_f), ws_part('pbi', 1, pow_b)], axis=-1)

    out_f = jnp.arange(n) + 1
    out_b = n - jnp.arange(n)

    def wo_part(key, d, pows):
        return per_dir[d][key][pows].transpose(1, 3, 0, 2).reshape(g, p, S5_ROW)

    zero = jnp.zeros((g, p, S5_ROW), F32)
    wof = jnp.concatenate([wo_part('cpr', 0, out_f), zero, -wo_part('cpi', 0, out_f), zero], axis=1)
    wob = jnp.concatenate([zero, wo_part('cpr', 1, out_b), zero, -wo_part('cpi', 1, out_b)], axis=1)
    ar = jnp.concatenate([per_dir[0]['pr'][n], per_dir[1]['pr'][n]], axis=-1)[:, None, :]
    ai = jnp.concatenate([per_dir[0]['pi'][n], per_dir[1]['pi'][n]], axis=-1)[:, None, :]
    return m.astype(BF16), ws.astype(BF16), wof.astype(BF16), wob.astype(BF16), ar, ai


def _s5_kernel(u_ref, m_ref, ws_ref, wof_ref, wob_ref, ar_ref, ai_ref, y_ref,
               s_ref, hf_ref, hb_ref, *, nb, nchunk):
    u = u_ref[...]
    s_ref[...] = jnp.dot(u, ws_ref[...], preferred_element_type=F32)
    ar = ar_ref[...]
    ai = ai_ref[...]
    is_fwd = lax.broadcasted_iota(jnp.int32, (nb, LANES), 1) < S5_STATE

    def step(i, carry):
        h_re, h_im = carry
        rf = pl.multiple_of(i * nb, 8)
        rb = pl.multiple_of((nchunk - 1 - i) * nb, 8)
        hf_ref[pl.ds(rf, nb), 0:LANES] = h_re
        hf_ref[pl.ds(rf, nb), LANES:2 * LANES] = h_im
        hb_ref[pl.ds(rb, nb), 0:LANES] = h_re
        hb_ref[pl.ds(rb, nb), LANES:2 * LANES] = h_im
        s_re = jnp.where(is_fwd, s_ref[pl.ds(rf, nb), 0:LANES], s_ref[pl.ds(rb, nb), 0:LANES])
        s_im = jnp.where(is_fwd, s_ref[pl.ds(rf, nb), LANES:2 * LANES],
                         s_ref[pl.ds(rb, nb), LANES:2 * LANES])
        return (ar * h_re - ai * h_im + s_re, ar * h_im + ai * h_re + s_im)

    zero = jnp.zeros((nb, LANES), F32)
    lax.fori_loop(0, nchunk, step, (zero, zero))
    y = jnp.dot(u, m_ref[...], preferred_element_type=F32)
    y = y + jnp.dot(hf_ref[...].astype(BF16), wof_ref[...], preferred_element_type=F32)
    y = y + jnp.dot(hb_ref[...].astype(BF16), wob_ref[...], preferred_element_type=F32)
    y_ref[...] = y.astype(BF16)


def _s5(u, s5w):
    m, ws, wof, wob, ar, ai = s5w
    bsz, seq, _ = u.shape
    nchunk = seq // S5_CHUNK
    rows = nchunk * bsz
    u2 = (u.reshape(bsz, nchunk, S5_CHUNK, S5_GROUPS, S5_GROUP_CH)
          .transpose(3, 1, 0, 2, 4).reshape(S5_GROUPS, rows, S5_ROW))
    grp = lambda r, c: pl.BlockSpec((None, r, c), lambda g: (g, 0, 0))
    y2 = pl.pallas_call(
        functools.partial(_s5_kernel, nb=bsz, nchunk=nchunk),
        grid=(S5_GROUPS,),
        in_specs=[grp(rows, S5_ROW), grp(S5_ROW, S5_ROW), grp(S5_ROW, S5_ROW),
                  grp(S5_ROW, S5_ROW), grp(S5_ROW, S5_ROW), grp(1, LANES), grp(1, LANES)],
        out_specs=grp(rows, S5_ROW),
        out_shape=jax.ShapeDtypeStruct((S5_GROUPS, rows, S5_ROW), BF16),
        scratch_shapes=[pltpu.VMEM((rows, S5_ROW), F32)] * 3,
        compiler_params=_cparams(("parallel",)),
        name="s5",
    )(u2, m, ws, wof, wob, ar, ai)
    return (y2.reshape(S5_GROUPS, nchunk, bsz, S5_CHUNK, S5_GROUP_CH)
            .transpose(2, 1, 3, 0, 4).reshape(bsz, seq, S5_WIDTH))


def _attn_kernel(lam_ref, q_ref, k_ref, v_ref, g_ref, o_ref, vx_ref):
    @pl.when(pl.program_id(2) == 0)
    def _():
        vx_ref[:, 0:LANES] = v_ref[...]
        vx_ref[:, LANES:2 * LANES] = jnp.ones(v_ref.shape, BF16)

    k = k_ref[...]
    vx = vx_ref[...]
    rows_per_chunk = q_ref.shape[0] // ATT_ROW_CHUNKS

    def scores(qm):
        return lax.dot_general(qm, k, (((1,), (1,)), ((), ())), preferred_element_type=F32)

    def probs(s):
        return jnp.exp2(s - jnp.max(s, axis=-1, keepdims=True)).astype(BF16)

    def weighted(e):
        ol = jnp.dot(e, vx, preferred_element_type=F32)
        return ol[:, 0:LANES] / ol[:, LANES:2 * LANES]

    chains = []
    for c in range(ATT_ROW_CHUNKS):
        rows = slice(c * rows_per_chunk, (c + 1) * rows_per_chunk)
        q = q_ref[rows, :]
        first = lax.broadcasted_iota(jnp.int32, q.shape, 1) < ATT_HEAD_DIM
        zero = jnp.zeros_like(q)
        chains.append((rows, jnp.where(first, q, zero)))
        chains.append((rows, jnp.where(first, zero, q)))

    s_val, e_val, o_val = {}, {}, {}
    for t in range(len(chains) + 2):
        if t < len(chains):
            s_val[t] = scores(chains[t][1])
        if 0 <= t - 1 < len(chains):
            e_val[t - 1] = probs(s_val.pop(t - 1))
        if 0 <= t - 2 < len(chains):
            o_val[t - 2] = weighted(e_val.pop(t - 2))
            if (t - 2) % 2 == 1:
                o = o_val.pop(t - 3) - lam_ref[0] * o_val.pop(t - 2)
                o = o * lax.rsqrt(jnp.mean(o * o, axis=-1, keepdims=True) + LN_EPS) * g_ref[...]
                o_ref[chains[t - 2][0], :] = o.astype(BF16)


def _attention(lam, q, k, v, g_row):
    bsz, seq, _ = q.shape
    tq = min(ATT_TILE, seq)
    return pl.pallas_call(
        _attn_kernel,
        grid=(bsz, ATT_HEADS, seq // tq),
        in_specs=[pl.BlockSpec(memory_space=pltpu.SMEM),
                  pl.BlockSpec((None, tq, LANES), lambda b, h, i: (b, i, h)),
                  pl.BlockSpec((None, seq, LANES), lambda b, h, i: (b, 0, h)),
                  pl.BlockSpec((None, seq, LANES), lambda b, h, i: (b, 0, h)),
                  pl.BlockSpec((1, LANES), lambda b, h, i: (0, 0))],
        out_specs=pl.BlockSpec((None, tq, LANES), lambda b, h, i: (b, i, h)),
        out_shape=jax.ShapeDtypeStruct((bsz, seq, ATT_WIDTH), BF16),
        scratch_shapes=[pltpu.VMEM((seq, 2 * LANES), BF16)],
        compiler_params=_cparams(("parallel", "parallel", "arbitrary")),
        name="attn",
    )(lam, q, k, v, g_row)


def _merge_kernel(x_ref, u_ref, y_ref, o_ref, ga_ref, gb_ref, mod_ref, d_ref, wglu_ref, ws5_ref,
                  watt_ref, wo_ref, ln1g_ref, ln1b_ref, wq_ref, x1_ref, h2_ref, pq_ref):
    mod = mod_ref[...]
    s = _gelu_tanh(u_ref[...].astype(F32) * d_ref[...] + y_ref[...].astype(F32))
    s = s * _sigmoid(jnp.dot(s.astype(BF16), wglu_ref[...], preferred_element_type=F32))
    br_a = jnp.dot(s.astype(BF16), ws5_ref[...], preferred_element_type=F32)
    br_b = jnp.dot(o_ref[...], watt_ref[...], preferred_element_type=F32)
    merged = ga_ref[...].astype(F32) * br_a + gb_ref[...].astype(F32) * br_b
    z = jnp.dot(merged.astype(BF16), wo_ref[...], preferred_element_type=F32)
    x1 = _ln(DN_ALPHA * x_ref[...] + mod[2:3] * z) * ln1g_ref[...] + ln1b_ref[...]
    x1_ref[...] = x1
    h2 = (_ln(x1) * (1.0 + mod[4:5]) + mod[3:4]).astype(BF16)
    h2_ref[...] = h2
    pq_ref[...] = jnp.dot(h2, wq_ref[...], preferred_element_type=F32).astype(BF16)


def _merge(x, u, y, o, ga, gb, mod3, wts):
    bsz, seq, _ = x.shape
    tm = min(MERGE_TILE, seq)
    tok = lambda n: pl.BlockSpec((None, tm, n), lambda b, i: (b, i, 0))
    row = lambda n: _const_spec((1, n))
    return pl.pallas_call(
        _merge_kernel,
        grid=(bsz, seq // tm),
        in_specs=[tok(D_MODEL), tok(S5_WIDTH), tok(S5_WIDTH), tok(ATT_WIDTH), tok(D_MODEL), tok(D_MODEL),
                  pl.BlockSpec((None, 6, D_MODEL), lambda b, i: (b, 0, 0)),
                  row(S5_WIDTH), _const_spec((S5_WIDTH, S5_WIDTH)), _const_spec((S5_WIDTH, D_MODEL)),
                  _const_spec((ATT_WIDTH, D_MODEL)), _const_spec((D_MODEL, D_MODEL)),
                  row(D_MODEL), row(D_MODEL), _const_spec((D_MODEL, PEER_NQ))],
        out_specs=[tok(D_MODEL), tok(D_MODEL), tok(PEER_NQ)],
        out_shape=[jax.ShapeDtypeStruct((bsz, seq, D_MODEL), F32),
                   jax.ShapeDtypeStruct((bsz, seq, D_MODEL), BF16),
                   jax.ShapeDtypeStruct((bsz, seq, PEER_NQ), BF16)],
        compiler_params=_cparams(("parallel", "parallel")),
        name="merge",
    )(x, u, y, o, ga, gb, mod3, wts['s5_d'], wts['w_glu'], wts['w_s5_out'], wts['w_attn_out'],
      wts['w_o'], wts['ln1_g'], wts['ln1_b'], wts['w_q'])


def _route_kernel(pq_ref, keys_ref, w_ref, a_sc, b_sc, th_sc, wt_sc, ix_sc, wr_sc, ir_sc, s2_sc, e2_sc):
    tn = pq_ref.shape[0]
    row8 = lax.broadcasted_iota(jnp.int32, (8, tn), 0)
    row8_f = row8.astype(F32)
    nt_dims = (((1,), (1,)), ((), ()))
    nv = PEER_NKEYS // 8

    def sorted_slabs(st, with_index):
        vals = [st[8 * v:8 * v + 8, :] for v in range(nv)]
        idxs = [row8_f + float(8 * v) for v in range(nv)] if with_index else None
        for lo, hi in _SORT16_PAIRS:
            a, b = vals[lo], vals[hi]
            vals[lo], vals[hi] = jnp.maximum(a, b), jnp.minimum(a, b)
            if with_index:
                keep = a >= b
                ia, ib = idxs[lo], idxs[hi]
                idxs[lo], idxs[hi] = jnp.where(keep, ia, ib), jnp.where(keep, ib, ia)
        return vals, idxs

    def pop_heads(vals, idxs, pick, k):
        for d in range(PEER_TOPK - 1 - k):
            vals[d] = jnp.where(pick, vals[d + 1], vals[d])
            if idxs is not None:
                idxs[d] = jnp.where(pick, idxs[d + 1], idxs[d])

    for h in range(PEER_HEADS):
        q1 = pq_ref[:, (2 * h) * PEER_HALF:(2 * h + 1) * PEER_HALF]
        q2 = pq_ref[:, (2 * h + 1) * PEER_HALF:(2 * h + 2) * PEER_HALF]
        st1 = lax.dot_general(keys_ref[2 * h], q1, nt_dims, preferred_element_type=F32)
        st2 = lax.dot_general(keys_ref[2 * h + 1], q2, nt_dims, preferred_element_type=F32)
        s2 = st2.T
        s2_sc[h] = s2
        e2_sc[h] = jnp.exp(s2 - jnp.max(s2, axis=-1, keepdims=True))

        vals, idxs = sorted_slabs(st1, True)
        a_vals = []
        for k in range(PEER_TOPK):
            m = jnp.max(vals[0], axis=0, keepdims=True)
            ix = jnp.min(jnp.where(vals[0] == m, idxs[0], float(PEER_NKEYS)), axis=0, keepdims=True)
            pop_heads(vals, idxs, idxs[0] == ix, k)
            a_vals.append(m)
            a_sc[k:k + 1, :] = m
            ix_sc[h * PEER_TOPK + k:h * PEER_TOPK + k + 1, :] = ix
        vals, _ = sorted_slabs(st2, False)
        for k in range(PEER_TOPK):
            m = jnp.max(vals[0], axis=0, keepdims=True)
            pop_heads(vals, None, vals[0] == m, k)
            b_sc[k:k + 1, :] = m
        b_lo = b_sc[0:8, :]
        b_hi = b_sc[8:16, :]
        b0 = b_sc[0:1, :]

        shift = lambda x, n: pltpu.roll(x, n, 0)
        cands = [
            a_vals[0] + b_lo,
            a_vals[0] + b_hi,
            a_vals[1] + b_lo,
            jnp.where(row8 < 5, a_vals[2] + b_lo, a_vals[4] + shift(b_lo, 5)),
            jnp.where(row8 < 4, a_vals[3] + b_lo,
                      jnp.where(row8 < 6, a_vals[5] + shift(b_lo, 4), a_vals[6] + shift(b_lo, 6))),
            jnp.where(row8 < 2, a_vals[7] + b_lo, a_sc[6:14, :] + b0),
            jnp.where(row8 < 2, shift(a_sc[8:16, :], 2) + b0, NEG_BIG),
        ]
        m0 = a_vals[0] + b0
        z = jnp.zeros_like(m0)
        tau = m0
        for _ in range(PEER_TOPK):
            tau = jnp.max(functools.reduce(jnp.maximum, cands), axis=0, keepdims=True)
            cands = [jnp.where(c == tau, NEG_BIG, c) for c in cands]
            z = z + jnp.exp(tau - m0)
        inv_z = 1.0 / z
        for k in range(PEER_TOPK):
            th = jnp.where(a_vals[k] + b_lo >= tau, b_lo, POS_BIG)
            if PEER_TOPK // (k + 1) > 8:
                th = jnp.minimum(th, jnp.where(a_vals[k] + b_hi >= tau, b_hi, POS_BIG))
            j = h * PEER_TOPK + k
            th_sc[j:j + 1, :] = jnp.min(th, axis=0, keepdims=True)
            wt_sc[j:j + 1, :] = jnp.exp(a_vals[k] - a_vals[0]) * inv_z

    wr_sc[...] = wt_sc[...].T
    ir_sc[...] = ix_sc[...].T
    lane_i = lax.broadcasted_iota(jnp.int32, (PEER_NKEYS, tn), 1)
    sub_j = lax.broadcasted_iota(jnp.int32, (PEER_NKEYS, PEER_NKEYS), 0).astype(F32)

    def token(t):
        th_col = jnp.sum(jnp.where(lane_i == t, th_sc[...], 0.0), axis=1, keepdims=True)
        s2b = jnp.concatenate([jnp.broadcast_to(s2_sc[h, pl.ds(t, 1), :], (PEER_TOPK, PEER_NKEYS))
                               for h in range(PEER_HEADS)], axis=0)
        e2b = jnp.concatenate([jnp.broadcast_to(e2_sc[h, pl.ds(t, 1), :], (PEER_TOPK, PEER_NKEYS))
                               for h in range(PEER_HEADS)], axis=0)
        r = jnp.where(s2b >= th_col, e2b, 0.0).astype(BF16)
        pt = jnp.where(sub_j == ir_sc[pl.ds(t, 1), :], wr_sc[pl.ds(t, 1), :], 0.0).astype(BF16)
        return jnp.dot(pt, r, preferred_element_type=F32)

    def token_group(g, carry):
        for pair in range(ROUTE_PAIRS_PER_TRIP):
            t0 = pl.multiple_of((g * ROUTE_PAIRS_PER_TRIP + pair) * 16, 16)
            halves = []
            for half in range(2):
                tiles = jnp.stack([token(t0 + 8 * half + u) for u in range(8)])
                halves.append(jnp.swapaxes(tiles, 0, 1))
            w_ref[:, pl.ds(t0, 16), :] = jnp.concatenate(halves, axis=1).astype(BF16)
        return carry

    lax.fori_loop(0, tn // (16 * ROUTE_PAIRS_PER_TRIP), token_group, 0)


def _route(pq, keys_b):
    ntok = pq.shape[0]
    tn = ROUTE_TILE
    nj = PEER_HEADS * PEER_TOPK
    return pl.pallas_call(
        _route_kernel,
        grid=(ntok // tn,),
        in_specs=[pl.BlockSpec((tn, PEER_NQ), lambda i: (i, 0)),
                  _const_spec((2 * PEER_HEADS, PEER_NKEYS, PEER_HALF))],
        out_specs=pl.BlockSpec((None, PEER_NKEYS, tn, PEER_NKEYS), lambda i: (i, 0, 0, 0)),
        out_shape=jax.ShapeDtypeStruct((ntok // tn, PEER_NKEYS, tn, PEER_NKEYS), BF16),
        scratch_shapes=[pltpu.VMEM((PEER_TOPK, tn), F32), pltpu.VMEM((PEER_TOPK, tn), F32),
                        pltpu.VMEM((nj, tn), F32), pltpu.VMEM((nj, tn), F32), pltpu.VMEM((nj, tn), F32),
                        pltpu.VMEM((tn, nj), F32), pltpu.VMEM((tn, nj), F32),
                        pltpu.VMEM((PEER_HEADS, tn, PEER_NKEYS), F32),
                        pltpu.VMEM((PEER_HEADS, tn, PEER_NKEYS), F32)],
        compiler_params=_cparams(("parallel",)),
        name="route",
    )(pq, keys_b)


def _expert_kernel(h_ref, ut_ref, v_ref, w_ref, x1_ref, mod_ref, g_ref, b_ref, o_ref, acc_ref, gate_sc):
    e = pl.program_id(2)

    @pl.when(e == 0)
    def _():
        acc_ref[...] = jnp.zeros_like(acc_ref)

    act = jnp.dot(h_ref[...], ut_ref[...], preferred_element_type=F32)
    rt = w_ref.shape[2]
    for j in range(w_ref.shape[0]):
        for k in range(w_ref.shape[1]):
            a = act[j * rt:(j + 1) * rt, k * PEER_NKEYS:(k + 1) * PEER_NKEYS].astype(BF16)
            gate_sc[j * rt:(j + 1) * rt, k * PEER_NKEYS:(k + 1) * PEER_NKEYS] = (
                _gelu_tanh(a) * w_ref[j, k])
    acc_ref[...] += jnp.dot(gate_sc[...], v_ref[...], preferred_element_type=F32)

    @pl.when(e == pl.num_programs(2) - 1)
    def _():
        mod = mod_ref[...]
        o_ref[...] = _ln(DN_ALPHA * x1_ref[...] + mod[5:6] * acc_ref[...]) * g_ref[...] + b_ref[...]


def _experts(h2, w4, x1, mod3, ut_b, v_b, ln2_g, ln2_b):
    bsz, seq, _ = x1.shape
    tt = min(EXPERT_TOK_TILE, seq)
    ce = EXPERT_CHUNK
    rt = w4.shape[2]
    nblk = seq // tt
    tok = lambda n: pl.BlockSpec((None, tt, n), lambda b, i, e: (b, i, 0))
    return pl.pallas_call(
        _expert_kernel,
        grid=(bsz, nblk, PEER_EXPERTS // ce),
        in_specs=[tok(D_MODEL),
                  pl.BlockSpec((None, D_MODEL, ce), lambda b, i, e: (e, 0, 0)),
                  pl.BlockSpec((ce, D_MODEL), lambda b, i, e: (e, 0)),
                  pl.BlockSpec((tt // rt, ce // PEER_NKEYS, rt, PEER_NKEYS),
                               lambda b, i, e: (b * nblk + i, e, 0, 0)),
                  pl.BlockSpec((None, tt, D_MODEL), lambda b, i, e: (b, i, 0),
                               pipeline_mode=pl.Buffered(1)),
                  pl.BlockSpec((None, 6, D_MODEL), lambda b, i, e: (b, 0, 0)),
                  pl.BlockSpec((1, D_MODEL), lambda b, i, e: (0, 0)),
                  pl.BlockSpec((1, D_MODEL), lambda b, i, e: (0, 0))],
        out_specs=tok(D_MODEL),
        out_shape=jax.ShapeDtypeStruct((bsz, seq, D_MODEL), F32),
        scratch_shapes=[pltpu.VMEM((tt, D_MODEL), F32), pltpu.VMEM((tt, ce), BF16)],
        compiler_params=_cparams(("parallel", "parallel", "arbitrary")),
        name="expert",
    )(h2, ut_b, v_b, w4, x1, mod3, ln2_g, ln2_b)


def _rope_tables(seq):
    half = ATT_HEAD_DIM // 2
    inv_freq = ROPE_THETA ** (-jnp.arange(half, dtype=F32) * 2.0 / ATT_HEAD_DIM)
    ang = jnp.arange(seq, dtype=F32)[:, None] * inv_freq[None, :]
    cos = jnp.cos(ang)
    sin = jnp.sin(ang)
    zero = jnp.zeros_like(sin)
    reps = LANES // ATT_HEAD_DIM
    cos_t = jnp.tile(jnp.concatenate([cos, cos], axis=1), (1, reps))
    sa_t = jnp.tile(jnp.concatenate([-sin, zero], axis=1), (1, reps))
    sb_t = jnp.tile(jnp.concatenate([zero, sin], axis=1), (1, reps))
    return cos_t, sa_t, sb_t


def _prepare(layer_idx, p):
    lam_init = 0.8 - 0.6 * math.exp(-0.3 * layer_idx)
    lam = (jnp.exp(jnp.sum(p['attn_lambda_q1'].astype(F32) * p['attn_lambda_k1'].astype(F32)))
           - jnp.exp(jnp.sum(p['attn_lambda_q2'].astype(F32) * p['attn_lambda_k2'].astype(F32))) + lam_init)
    row = lambda a: a.astype(F32).reshape(1, -1)
    return dict(
        w_ada=p['w_ada'].astype(F32), b_ada=p['b_ada'].astype(F32),
        w_in=p['w_in'].astype(BF16),
        s5w=_s5_weights(p['s5_lambda_re'], p['s5_lambda_im'], p['s5_log_dt'], p['s5_b_re'],
                        p['s5_b_im'], p['s5_c_re'], p['s5_c_im']),
        s5_d=row(p['s5_d']), w_glu=p['s5_w_glu'].astype(BF16), w_s5_out=p['w_s5_out'].astype(BF16),
        lam=lam.reshape(1).astype(F32),
        subln=row(p['attn_subln_g']) * (1.0 - lam_init),
        w_attn_out=p['w_attn_out'].astype(BF16), w_o=p['w_o'].astype(BF16),
        ln1_g=row(p['ln1_g']), ln1_b=row(p['ln1_b']),
        w_q=p['peer_w_q'].astype(BF16),
        keys=p['peer_keys'].astype(BF16).reshape(2 * PEER_HEADS, PEER_NKEYS, PEER_HALF),
        ut=(p['peer_u'].astype(BF16).reshape(PEER_EXPERTS // EXPERT_CHUNK, EXPERT_CHUNK, D_MODEL)
            .transpose(0, 2, 1)),
        v=p['peer_v'].astype(BF16),
        ln2_g=row(p['ln2_g']), ln2_b=row(p['ln2_b']),
    )


def _encoder_layer(x, c, wts, rope):
    bsz, seq, _ = x.shape
    mod3 = _ada(c.astype(F32), wts['w_ada'], wts['b_ada']).reshape(bsz, 6, D_MODEL)
    u, q, k, v, ga, gb = _inproj(x.astype(F32), mod3, wts['w_in'], *rope)
    y = _s5(u, wts['s5w'])
    o = _attention(wts['lam'], q, k, v, wts['subln'])
    x1, h2, pq = _merge(x.astype(F32), u, y, o, ga, gb, mod3, wts)
    w4 = _route(pq.reshape(bsz * seq, PEER_NQ), wts['keys'])
    x2 = _experts(h2, w4, x1, mod3, wts['ut'], wts['v'], wts['ln2_g'], wts['ln2_b'])
    return x2.astype(x.dtype)


def kernel(x_prompt, x_sample, c_prompt, c_sample, w_ada, b_ada, w_in, s5_lambda_re, s5_lambda_im, s5_log_dt, s5_b_re, s5_b_im, s5_c_re, s5_c_im, s5_d, s5_w_glu, w_s5_out, attn_lambda_q1, attn_lambda_k1, attn_lambda_q2, attn_lambda_k2, attn_subln_g, w_attn_out, w_o, ln1_g, ln1_b, peer_w_q, peer_keys, peer_u, peer_v, ln2_g, ln2_b):
    params = dict(w_ada=w_ada, b_ada=b_ada, w_in=w_in, s5_lambda_re=s5_lambda_re,
                  s5_lambda_im=s5_lambda_im, s5_log_dt=s5_log_dt, s5_b_re=s5_b_re, s5_b_im=s5_b_im,
                  s5_c_re=s5_c_re, s5_c_im=s5_c_im, s5_d=s5_d, s5_w_glu=s5_w_glu, w_s5_out=w_s5_out,
                  attn_lambda_q1=attn_lambda_q1, attn_lambda_k1=attn_lambda_k1,
                  attn_lambda_q2=attn_lambda_q2, attn_lambda_k2=attn_lambda_k2,
                  attn_subln_g=attn_subln_g, w_attn_out=w_attn_out, w_o=w_o, ln1_g=ln1_g, ln1_b=ln1_b,
                  peer_w_q=peer_w_q, peer_keys=peer_keys, peer_u=peer_u, peer_v=peer_v,
                  ln2_g=ln2_g, ln2_b=ln2_b)
    y_prompt, y_sample = x_prompt, x_sample
    rope_p = _rope_tables(x_prompt.shape[1])
    rope_s = _rope_tables(x_sample.shape[1])
    for l in range(DEPTH):
        wts = _prepare(l, {name: a[l] for name, a in params.items()})
        y_prompt = _encoder_layer(y_prompt, c_prompt, wts, rope_p)
        y_sample = _encoder_layer(y_sample, c_sample, wts, rope_s)
    return (y_prompt, y_sample)
```

```python
import functools
import math

import jax
import jax.numpy as jnp
from jax import lax
from jax.experimental import pallas as pl
from jax.experimental.pallas import tpu as pltpu

F32 = jnp.float32
BF16 = jnp.bfloat16
HIGHEST = lax.Precision.HIGHEST

D_MODEL = 1024
DEPTH = 1
S5_WIDTH = D_MODEL // 2
S5_GROUP_CH = 16
S5_GROUPS = S5_WIDTH // S5_GROUP_CH
S5_STATE = 64
S5_CHUNK = 16
S5_ROW = S5_CHUNK * S5_GROUP_CH
ATT_HEADS = 8
ATT_HEAD_DIM = D_MODEL // (2 * ATT_HEADS)
ATT_WIDTH = ATT_HEADS * 2 * ATT_HEAD_DIM
ROPE_THETA = 10000.0
N_IN = S5_WIDTH + 3 * ATT_WIDTH + 2 * D_MODEL
PEER_HEADS = 8
PEER_NKEYS = 128
PEER_EXPERTS = PEER_NKEYS * PEER_NKEYS
PEER_TOPK = 16
PEER_QDIM = 256
PEER_HALF = PEER_QDIM // 2
PEER_NQ = PEER_HEADS * PEER_QDIM
DN_ALPHA = (2 * DEPTH) ** 0.25
LN_EPS = 1e-5

LANES = 128
VMEM_LIMIT_BYTES = 56 * 1024 * 1024
NEG_BIG = -3.0e38
POS_BIG = 3.0e38

IN_TILE = 512
ATT_TILE = 1024
ATT_ROW_CHUNKS = 2
MERGE_TILE = 512
MERGE_ROW_CHUNKS = 2
ROUTE_TILE = 128
DEPTH_DIRECT = 4
ROUTE_PAIRS_PER_TRIP = 2
EXPERT_TOK_TILE = 1024
EXPERT_CHUNK = 1024


def _merge_exchange_pairs(n):
    pairs = []
    p = 1
    while p < n:
        k = p
        while k >= 1:
            for j in range(k % p, n - k, 2 * k):
                for i in range(min(k, n - j - k)):
                    if (i + j) // (2 * p) == (i + j + k) // (2 * p):
                        pairs.append((i + j, i + j + k))
            k //= 2
        p *= 2
    return pairs


_SORT16_PAIRS = _merge_exchange_pairs(PEER_NKEYS // 8)


def _cparams(sem):
    return pltpu.CompilerParams(dimension_semantics=sem, vmem_limit_bytes=VMEM_LIMIT_BYTES)


def _const_spec(shape):
    nd = len(shape)
    return pl.BlockSpec(shape, lambda *_: (0,) * nd, pipeline_mode=pl.Buffered(1))


def _ln(x):
    mu = jnp.mean(x, axis=-1, keepdims=True)
    xc = x - mu
    var = jnp.mean(xc * xc, axis=-1, keepdims=True)
    return xc * lax.rsqrt(var + LN_EPS)


def _gelu_tanh(x):
    return 0.5 * x * (1.0 + jnp.tanh(math.sqrt(2.0 / math.pi) * (x + 0.044715 * (x * x * x))))


def _sigmoid(x):
    return 1.0 / (1.0 + jnp.exp(-x))


def _ada_kernel(c_ref, w_ref, b_ref, o_ref):
    c = c_ref[...]
    s = c * _sigmoid(c)
    o_ref[...] = jnp.dot(s, w_ref[...], precision=HIGHEST, preferred_element_type=F32) + b_ref[...]


def _ada(c, w_ada, b_ada):
    bsz = c.shape[0]
    nblk = w_ada.shape[1] // D_MODEL
    return pl.pallas_call(
        _ada_kernel,
        grid=(nblk,),
        in_specs=[
            pl.BlockSpec((bsz, D_MODEL), lambda j: (0, 0)),
            pl.BlockSpec((D_MODEL, D_MODEL), lambda j: (0, j)),
            pl.BlockSpec((1, D_MODEL), lambda j: (0, j)),
        ],
        out_specs=pl.BlockSpec((bsz, D_MODEL), lambda j: (0, j)),
        out_shape=jax.ShapeDtypeStruct((bsz, nblk * D_MODEL), F32),
        compiler_params=_cparams(("arbitrary",)),
        name="ada",
    )(c, w_ada, b_ada.reshape(1, -1))


def _in_kernel(x_ref, mod_ref, w_ref, cos_ref, sa_ref, sb_ref,
               u_ref, q_ref, k_ref, v_ref, ga_ref, gb_ref):
    mod = mod_ref[...]
    h = _ln(x_ref[...]) * (1.0 + mod[1:2]) + mod[0:1]
    hb = h.astype(BF16)

    def proj(a, b):
        return jnp.dot(hb, w_ref[:, a:b], preferred_element_type=F32)

    o1 = S5_WIDTH
    o2 = o1 + ATT_WIDTH
    o3 = o2 + ATT_WIDTH
    o4 = o3 + ATT_WIDTH
    o5 = o4 + D_MODEL
    u_ref[...] = proj(0, o1).astype(BF16)
    cos, sa, sb = cos_ref[...], sa_ref[...], sb_ref[...]
    half = ATT_HEAD_DIM // 2

    def rope_store(t, o_ref, scale):
        for j in range(ATT_WIDTH // LANES):
            tj = t[:, j * LANES:(j + 1) * LANES]
            r = (tj * cos + pltpu.roll(tj, LANES - half, 1) * sa + pltpu.roll(tj, half, 1) * sb)
            o_ref[:, j * LANES:(j + 1) * LANES] = (r * scale).astype(BF16)

    rope_store(proj(o1, o2), q_ref, ATT_HEAD_DIM ** -0.5 * math.log2(math.e))
    rope_store(proj(o2, o3), k_ref, 1.0)
    v_ref[...] = proj(o3, o4).astype(BF16)
    ga_ref[...] = _sigmoid(proj(o4, o5)).astype(BF16)
    gb_ref[...] = _sigmoid(proj(o5, N_IN)).astype(BF16)


def _inproj(x, mod3, w_in_b, cos, sa, sb):
    bsz, seq, _ = x.shape
    tm = min(IN_TILE, seq)
    tok = lambda n: pl.BlockSpec((None, tm, n), lambda b, i: (b, i, 0))
    rope = pl.BlockSpec((tm, LANES), lambda b, i: (i, 0))
    outs = [(S5_WIDTH, BF16), (ATT_WIDTH, BF16), (ATT_WIDTH, BF16), (ATT_WIDTH, BF16),
            (D_MODEL, BF16), (D_MODEL, BF16)]
    return pl.pallas_call(
        _in_kernel,
        grid=(bsz, seq // tm),
        in_specs=[tok(D_MODEL),
                  pl.BlockSpec((None, 6, D_MODEL), lambda b, i: (b, 0, 0)),
                  _const_spec((D_MODEL, N_IN)), rope, rope, rope],
        out_specs=[tok(n) for n, _ in outs],
        out_shape=[jax.ShapeDtypeStruct((bsz, seq, n), dt) for n, dt in outs],
        compiler_params=_cparams(("parallel", "parallel")),
        name="inproj",
    )(x, mod3, w_in_b, cos, sa, sb)


def _s5_weights(lam_re, lam_im, log_dt, b_re, b_im, c_re, c_im):
    n = S5_CHUNK
    per_dir = []
    for d in range(2):
        lr = lam_re[d].astype(F32)
        li = lam_im[d].astype(F32)
        dt = jnp.exp(log_dt[d].astype(F32))[:, None]
        mag = jnp.exp(lr * dt)
        lbr = mag * jnp.cos(li * dt)
        lbi = mag * jnp.sin(li * dt)
        den = lr * lr + li * li
        nr = lbr - 1.0
        cr = (nr * lr + lbi * li) / den
        ci = (lbi * lr - nr * li) / den
        br = b_re[d].astype(F32)
        bi = b_im[d].astype(F32)
        bbr = cr[..., None] * br - ci[..., None] * bi
        bbi = cr[..., None] * bi + ci[..., None] * br
        pr = [jnp.ones_like(lbr)]
        pi = [jnp.zeros_like(lbi)]
        for _ in range(n):
            pr.append(pr[-1] * lbr - pi[-1] * lbi)
            pi.append(pr[-2] * lbi + pi[-1] * lbr)
        pr = jnp.stack(pr)
        pi = jnp.stack(pi)
        cre = c_re[d].astype(F32)
        cim = c_im[d].astype(F32)
        cpr = cre[None] * pr[:, :, None, :] - cim[None] * pi[:, :, None, :]
        cpi = cre[None] * pi[:, :, None, :] + cim[None] * pr[:, :, None, :]
        kern = (jnp.einsum('ngcp,gpd->ngcd', cpr, bbr, precision=HIGHEST)
                - jnp.einsum('ngcp,gpd->ngcd', cpi, bbi, precision=HIGHEST))
        pbr = pr[..., None] * bbr[None] - pi[..., None] * bbi[None]
        pbi = pr[..., None] * bbi[None] + pi[..., None] * bbr[None]
        per_dir.append(dict(pr=pr, pi=pi, cpr=cpr, cpi=cpi, kern=kern, pbr=pbr, pbi=pbi))

    g, c, p = S5_GROUPS, S5_GROUP_CH, S5_STATE
    s_idx = jnp.arange(n)[:, None]
    t_idx = jnp.arange(n)[None, :]
    tau_f = t_idx - s_idx
    tau_b = s_idx - t_idx
    kf = per_dir[0]['kern'][jnp.clip(tau_f, 0, n)] * (tau_f >= 0)[..., None, None, None]
    kb = per_dir[1]['kern'][jnp.clip(tau_b, 0, n)] * (tau_b >= 0)[..., None, None, None]
    m = (kf + kb).transpose(2, 0, 4, 1, 3).reshape(g, S5_ROW, S5_ROW)

    pow_f = n - 1 - jnp.arange(n)
    pow_b = jnp.arange(n)

    def ws_part(key, d, pows):
        return per_dir[d][key][pows].transpose(1, 0, 3, 2).reshape(g, S5_ROW, p)

    ws = jnp.concatenate([ws_part('pbr', 0, pow_f), ws_part('pbr', 1, pow_b),
                          ws_part('pbi', 0, pow_f), ws_part('pbi', 1, pow_b)], axis=-1)

    out_f = jnp.arange(n) + 1
    out_b = n - jnp.arange(n)

    def wo_part(key, d, pows):
        return per_dir[d][key][pows].transpose(1, 3, 0, 2).reshape(g, p, S5_ROW)

    zero = jnp.zeros((g, p, S5_ROW), F32)
    wof = jnp.concatenate([wo_part('cpr', 0, out_f), zero, -wo_part('cpi', 0, out_f), zero], axis=1)
    wob = jnp.concatenate([zero, wo_part('cpr', 1, out_b), zero, -wo_part('cpi', 1, out_b)], axis=1)
    ar = jnp.concatenate([per_dir[0]['pr'][n], per_dir[1]['pr'][n]], axis=-1)[:, None, :]
    ai = jnp.concatenate([per_dir[0]['pi'][n], per_dir[1]['pi'][n]], axis=-1)[:, None, :]
    return m.astype(BF16), ws.astype(BF16), wof.astype(BF16), wob.astype(BF16), ar, ai


def _s5_kernel(u_ref, m_ref, ws_ref, wof_ref, wob_ref, ar_ref, ai_ref, y_ref,
               s_ref, hf_ref, hb_ref, *, nb, nchunk):
    u = u_ref[...]
    s_ref[...] = jnp.dot(u, ws_ref[...], preferred_element_type=F32)
    ar = ar_ref[...]
    ai = ai_ref[...]
    is_fwd = lax.broadcasted_iota(jnp.int32, (nb, LANES), 1) < S5_STATE

    def step(i, carry):
        h_re, h_im = carry
        rf = pl.multiple_of(i * nb, 8)
        rb = pl.multiple_of((nchunk - 1 - i) * nb, 8)
        hf_ref[pl.ds(rf, nb), 0:LANES] = h_re
        hf_ref[pl.ds(rf, nb), LANES:2 * LANES] = h_im
        hb_ref[pl.ds(rb, nb), 0:LANES] = h_re
        hb_ref[pl.ds(rb, nb), LANES:2 * LANES] = h_im
        s_re = jnp.where(is_fwd, s_ref[pl.ds(rf, nb), 0:LANES], s_ref[pl.ds(rb, nb), 0:LANES])
        s_im = jnp.where(is_fwd, s_ref[pl.ds(rf, nb), LANES:2 * LANES],
                         s_ref[pl.ds(rb, nb), LANES:2 * LANES])
        return (ar * h_re - ai * h_im + s_re, ar * h_im + ai * h_re + s_im)

    zero = jnp.zeros((nb, LANES), F32)
    lax.fori_loop(0, nchunk, step, (zero, zero))
    y = jnp.dot(u, m_ref[...], preferred_element_type=F32)
    y = y + jnp.dot(hf_ref[...].astype(BF16), wof_ref[...], preferred_element_type=F32)
    y = y + jnp.dot(hb_ref[...].astype(BF16), wob_ref[...], preferred_element_type=F32)
    y_ref[...] = y.astype(BF16)


def _s5(u, s5w):
    m, ws, wof, wob, ar, ai = s5w
    bsz, seq, _ = u.shape
    nchunk = seq // S5_CHUNK
    rows = nchunk * bsz
    u2 = (u.reshape(bsz, nchunk, S5_CHUNK, S5_GROUPS, S5_GROUP_CH)
          .transpose(3, 1, 0, 2, 4).reshape(S5_GROUPS, rows, S5_ROW))
    grp = lambda r, c: pl.BlockSpec((None, r, c), lambda g: (g, 0, 0))
    y2 = pl.pallas_call(
        functools.partial(_s5_kernel, nb=bsz, nchunk=nchunk),
        grid=(S5_GROUPS,),
        in_specs=[grp(rows, S5_ROW), grp(S5_ROW, S5_ROW), grp(S5_ROW, S5_ROW),
                  grp(S5_ROW, S5_ROW), grp(S5_ROW, S5_ROW), grp(1, LANES), grp(1, LANES)],
        out_specs=grp(rows, S5_ROW),
        out_shape=jax.ShapeDtypeStruct((S5_GROUPS, rows, S5_ROW), BF16),
        scratch_shapes=[pltpu.VMEM((rows, S5_ROW), F32)] * 3,
        compiler_params=_cparams(("parallel",)),
        name="s5",
    )(u2, m, ws, wof, wob, ar, ai)
    return (y2.reshape(S5_GROUPS, nchunk, bsz, S5_CHUNK, S5_GROUP_CH)
            .transpose(2, 1, 3, 0, 4).reshape(bsz, seq, S5_WIDTH))


def _attn_kernel(lam_ref, q_ref, k_ref, v_ref, g_ref, o_ref, vx_ref):
    @pl.when(pl.program_id(2) == 0)
    def _():
        vx_ref[:, 0:LANES] = v_ref[...]
        vx_ref[:, LANES:2 * LANES] = jnp.ones(v_ref.shape, BF16)

    k = k_ref[...]
    vx = vx_ref[...]
    rows_per_chunk = q_ref.shape[0] // ATT_ROW_CHUNKS

    def scores(qm):
        return lax.dot_general(qm, k, (((1,), (1,)), ((), ())), preferred_element_type=F32)

    def probs(s):
        return jnp.exp2(s - jnp.max(s, axis=-1, keepdims=True)).astype(BF16)

    def weighted(e):
        ol = jnp.dot(e, vx, preferred_element_type=F32)
        return ol[:, 0:LANES] / ol[:, LANES:2 * LANES]

    chains = []
    for c in range(ATT_ROW_CHUNKS):
        rows = slice(c * rows_per_chunk, (c + 1) * rows_per_chunk)
        q = q_ref[rows, :]
        first = lax.broadcasted_iota(jnp.int32, q.shape, 1) < ATT_HEAD_DIM
        zero = jnp.zeros_like(q)
        chains.append((rows, jnp.where(first, q, zero)))
        chains.append((rows, jnp.where(first, zero, q)))

    s_val, e_val, o_val = {}, {}, {}
    for t in range(len(chains) + 2):
        if t < len(chains):
            s_val[t] = scores(chains[t][1])
        if 0 <= t - 1 < len(chains):
            e_val[t - 1] = probs(s_val.pop(t - 1))
        if 0 <= t - 2 < len(chains):
            o_val[t - 2] = weighted(e_val.pop(t - 2))
            if (t - 2) % 2 == 1:
                o = o_val.pop(t - 3) - lam_ref[0] * o_val.pop(t - 2)
                o = o * lax.rsqrt(jnp.mean(o * o, axis=-1, keepdims=True) + LN_EPS) * g_ref[...]
                o_ref[chains[t - 2][0], :] = o.astype(BF16)


def _attention(lam, q, k, v, g_row):
    bsz, seq, _ = q.shape
    tq = min(ATT_TILE, seq)
    return pl.pallas_call(
        _attn_kernel,
        grid=(bsz, ATT_HEADS, seq // tq),
        in_specs=[pl.BlockSpec(memory_space=pltpu.SMEM),
                  pl.BlockSpec((None, tq, LANES), lambda b, h, i: (b, i, h)),
                  pl.BlockSpec((None, seq, LANES), lambda b, h, i: (b, 0, h)),
                  pl.BlockSpec((None, seq, LANES), lambda b, h, i: (b, 0, h)),
                  pl.BlockSpec((1, LANES), lambda b, h, i: (0, 0))],
        out_specs=pl.BlockSpec((None, tq, LANES), lambda b, h, i: (b, i, h)),
        out_shape=jax.ShapeDtypeStruct((bsz, seq, ATT_WIDTH), BF16),
        scratch_shapes=[pltpu.VMEM((seq, 2 * LANES), BF16)],
        compiler_params=_cparams(("parallel", "parallel", "arbitrary")),
        name="attn",
    )(lam, q, k, v, g_row)


def _merge_kernel(x_ref, u_ref, y_ref, o_ref, ga_ref, gb_ref, mod_ref, d_ref, wglu_ref, ws5_ref,
                  watt_ref, wo_ref, ln1g_ref, ln1b_ref, wq_ref, x1_ref, h2_ref, pq_ref):
    mod = mod_ref[...]
    rc = x_ref.shape[0] // MERGE_ROW_CHUNKS
    rows = [slice(c * rc, (c + 1) * rc) for c in range(MERGE_ROW_CHUNKS)]
    dot = lambda a, b: jnp.dot(a, b, preferred_element_type=F32)
    s = [_gelu_tanh(u_ref[r, :].astype(F32) * d_ref[...] + y_ref[r, :].astype(F32)) for r in rows]
    glu = [dot(v.astype(BF16), wglu_ref[...]) for v in s]
    s = [a * _sigmoid(b) for a, b in zip(s, glu)]
    br_b = [dot(o_ref[r, :], watt_ref[...]) for r in rows]
    br_a = [dot(v.astype(BF16), ws5_ref[...]) for v in s]
    merged = [ga_ref[r, :].astype(F32) * a + gb_ref[r, :].astype(F32) * b
              for r, a, b in zip(rows, br_a, br_b)]
    z = [dot(v.astype(BF16), wo_ref[...]) for v in merged]
    h2s = []
    for r, zc in zip(rows, z):
        x1 = _ln(DN_ALPHA * x_ref[r, :] + mod[2:3] * zc) * ln1g_ref[...] + ln1b_ref[...]
        x1_ref[r, :] = x1
        h2 = (_ln(x1) * (1.0 + mod[4:5]) + mod[3:4]).astype(BF16)
        h2_ref[r, :] = h2
        h2s.append(h2)
    for r, h2 in zip(rows, h2s):
        pq_ref[r, :] = dot(h2, wq_ref[...]).astype(BF16)


def _merge(x, u, y, o, ga, gb, mod3, wts):
    bsz, seq, _ = x.shape
    tm = min(MERGE_TILE, seq)
    tok = lambda n: pl.BlockSpec((None, tm, n), lambda b, i: (b, i, 0))
    row = lambda n: _const_spec((1, n))
    return pl.pallas_call(
        _merge_kernel,
        grid=(bsz, seq // tm),
        in_specs=[tok(D_MODEL), tok(S5_WIDTH), tok(S5_WIDTH), tok(ATT_WIDTH), tok(D_MODEL), tok(D_MODEL),
                  pl.BlockSpec((None, 6, D_MODEL), lambda b, i: (b, 0, 0)),
                  row(S5_WIDTH), _const_spec((S5_WIDTH, S5_WIDTH)), _const_spec((S5_WIDTH, D_MODEL)),
                  _const_spec((ATT_WIDTH, D_MODEL)), _const_spec((D_MODEL, D_MODEL)),
                  row(D_MODEL), row(D_MODEL), _const_spec((D_MODEL, PEER_NQ))],
        out_specs=[tok(D_MODEL), tok(D_MODEL), tok(PEER_NQ)],
        out_shape=[jax.ShapeDtypeStruct((bsz, seq, D_MODEL), F32),
                   jax.ShapeDtypeStruct((bsz, seq, D_MODEL), BF16),
                   jax.ShapeDtypeStruct((bsz, seq, PEER_NQ), BF16)],
        compiler_params=_cparams(("parallel", "parallel")),
        name="merge",
    )(x, u, y, o, ga, gb, mod3, wts['s5_d'], wts['w_glu'], wts['w_s5_out'], wts['w_attn_out'],
      wts['w_o'], wts['ln1_g'], wts['ln1_b'], wts['w_q'])


def _route_kernel(pq_ref, keys_ref, w_ref, a_sc, b_sc, wt_sc, ix_sc, wr_sc, ir_sc, dp_sc, e2_sc):
    tn = pq_ref.shape[0]
    row8 = lax.broadcasted_iota(jnp.int32, (8, tn), 0)
    row8_f = row8.astype(F32)
    nt_dims = (((1,), (1,)), ((), ()))
    nv = PEER_NKEYS // 8

    def sorted_slabs(st, with_index):
        vals = [st[8 * v:8 * v + 8, :] for v in range(nv)]
        idxs = [row8_f + float(8 * v) for v in range(nv)] if with_index else None
        for lo, hi in _SORT16_PAIRS:
            a, b = vals[lo], vals[hi]
            vals[lo], vals[hi] = jnp.maximum(a, b), jnp.minimum(a, b)
            if with_index:
                keep = a >= b
                ia, ib = idxs[lo], idxs[hi]
                idxs[lo], idxs[hi] = jnp.where(keep, ia, ib), jnp.where(keep, ib, ia)
        return vals, idxs

    def pop_heads(vals, idxs, pick, k):
        for d in range(PEER_TOPK - 1 - k):
            vals[d] = jnp.where(pick, vals[d + 1], vals[d])
            if idxs is not None:
                idxs[d] = jnp.where(pick, idxs[d + 1], idxs[d])

    for h in range(PEER_HEADS):
        q1 = pq_ref[:, (2 * h) * PEER_HALF:(2 * h + 1) * PEER_HALF]
        q2 = pq_ref[:, (2 * h + 1) * PEER_HALF:(2 * h + 2) * PEER_HALF]
        st1 = lax.dot_general(keys_ref[2 * h], q1, nt_dims, preferred_element_type=F32)
        st2 = lax.dot_general(keys_ref[2 * h + 1], q2, nt_dims, preferred_element_type=F32)
        s2 = st2.T
        e2_sc[h] = jnp.exp(s2 - jnp.max(s2, axis=-1, keepdims=True))

        vals, idxs = sorted_slabs(st1, True)
        a_vals = []
        for k in range(PEER_TOPK):
            m = jnp.max(vals[0], axis=0, keepdims=True)
            ix = jnp.min(jnp.where(vals[0] == m, idxs[0], float(PEER_NKEYS)), axis=0, keepdims=True)
            pop_heads(vals, idxs, idxs[0] == ix, k)
            a_vals.append(m)
            a_sc[k:k + 1, :] = m
            ix_sc[h * PEER_TOPK + k:h * PEER_TOPK + k + 1, :] = ix
        vals, _ = sorted_slabs(st2, False)
        for k in range(PEER_TOPK):
            m = jnp.max(vals[0], axis=0, keepdims=True)
            pop_heads(vals, None, vals[0] == m, k)
            b_sc[k:k + 1, :] = m
        b_lo = b_sc[0:8, :]
        b_hi = b_sc[8:16, :]
        b0 = b_sc[0:1, :]

        shift = lambda x, n: pltpu.roll(x, n, 0)
        cands = [
            a_vals[0] + b_lo,
            a_vals[0] + b_hi,
            a_vals[1] + b_lo,
            jnp.where(row8 < 5, a_vals[2] + b_lo, a_vals[4] + shift(b_lo, 5)),
            jnp.where(row8 < 4, a_vals[3] + b_lo,
                      jnp.where(row8 < 6, a_vals[5] + shift(b_lo, 4), a_vals[6] + shift(b_lo, 6))),
            jnp.where(row8 < 2, a_vals[7] + b_lo, a_sc[6:14, :] + b0),
            jnp.where(row8 < 2, shift(a_sc[8:16, :], 2) + b0, NEG_BIG),
        ]
        m0 = a_vals[0] + b0
        z = jnp.zeros_like(m0)
        tau = m0
        for _ in range(PEER_TOPK):
            tau = jnp.max(functools.reduce(jnp.maximum, cands), axis=0, keepdims=True)
            cands = [jnp.where(c == tau, NEG_BIG, c) for c in cands]
            z = z + jnp.exp(tau - m0)
        inv_z = 1.0 / z
        for k in range(PEER_TOPK):
            j = h * PEER_TOPK + k
            wt_sc[j:j + 1, :] = jnp.exp(a_vals[k] - a_vals[0]) * inv_z

        depth = jnp.zeros_like(st2)
        for k in range(DEPTH_DIRECT):
            depth = depth + jnp.where(a_vals[k] + st2 >= tau, 1.0, 0.0)
        for l in range(PEER_TOPK // (DEPTH_DIRECT + 1)):
            b_l = b_sc[l:l + 1, :]
            extra = jnp.zeros_like(b_l)
            for k in range(DEPTH_DIRECT, PEER_TOPK // (l + 1)):
                extra = extra + jnp.where(a_vals[k] + b_l >= tau, 1.0, 0.0)
            depth = depth + jnp.where(st2 == b_l, extra, 0.0)
        dp_sc[h] = depth.T

    wr_sc[...] = wt_sc[...].T
    ir_sc[...] = ix_sc[...].T
    rank_j = (lax.broadcasted_iota(jnp.int32, (PEER_NKEYS, PEER_NKEYS), 0) % PEER_TOPK).astype(F32).astype(BF16)
    sub_j = lax.broadcasted_iota(jnp.int32, (PEER_NKEYS, PEER_NKEYS), 0).astype(F32).astype(BF16)
    zero_b = jnp.zeros((PEER_NKEYS, PEER_NKEYS), BF16)

    def token(t):
        def head_rows(ref):
            return jnp.concatenate(
                [jnp.broadcast_to(ref[h, pl.ds(t, 1), :], (PEER_TOPK, PEER_NKEYS)).astype(BF16)
                 for h in range(PEER_HEADS)], axis=0)

        def all_rows(ref):
            return jnp.broadcast_to(ref[pl.ds(t, 1), :], (PEER_NKEYS, PEER_NKEYS)).astype(BF16)

        r = jnp.where(rank_j < head_rows(dp_sc), head_rows(e2_sc), zero_b)
        pt = jnp.where(sub_j == all_rows(ir_sc), all_rows(wr_sc), zero_b)
        return jnp.dot(pt, r, preferred_element_type=F32)

    def token_group(g, carry):
        for pair in range(ROUTE_PAIRS_PER_TRIP):
            t0 = pl.multiple_of((g * ROUTE_PAIRS_PER_TRIP + pair) * 16, 16)
            halves = []
            for half in range(2):
                tiles = jnp.stack([token(t0 + 8 * half + u) for u in range(8)])
                halves.append(jnp.swapaxes(tiles, 0, 1))
            w_ref[:, pl.ds(t0, 16), :] = jnp.concatenate(halves, axis=1).astype(BF16)
        return carry

    lax.fori_loop(0, tn // (16 * ROUTE_PAIRS_PER_TRIP), token_group, 0)


def _route(pq, keys_b):
    ntok = pq.shape[0]
    tn = ROUTE_TILE
    nj = PEER_HEADS * PEER_TOPK
    return pl.pallas_call(
        _route_kernel,
        grid=(ntok // tn,),
        in_specs=[pl.BlockSpec((tn, PEER_NQ), lambda i: (i, 0)),
                  _const_spec((2 * PEER_HEADS, PEER_NKEYS, PEER_HALF))],
        out_specs=pl.BlockSpec((None, PEER_NKEYS, tn, PEER_NKEYS), lambda i: (i, 0, 0, 0)),
        out_shape=jax.ShapeDtypeStruct((ntok // tn, PEER_NKEYS, tn, PEER_NKEYS), BF16),
        scratch_shapes=[pltpu.VMEM((PEER_TOPK, tn), F32), pltpu.VMEM((PEER_TOPK, tn), F32),
                        pltpu.VMEM((nj, tn), F32), pltpu.VMEM((nj, tn), F32),
                        pltpu.VMEM((tn, nj), F32), pltpu.VMEM((tn, nj), F32),
                        pltpu.VMEM((PEER_HEADS, tn, PEER_NKEYS), F32),
                        pltpu.VMEM((PEER_HEADS, tn, PEER_NKEYS), F32)],
        compiler_params=_cparams(("parallel",)),
        name="route",
    )(pq, keys_b)


def _expert_kernel(h_ref, ut_ref, v_ref, w_ref, x1_ref, mod_ref, g_ref, b_ref, o_ref, acc_ref, gate_sc):
    e = pl.program_id(2)

    @pl.when(e == 0)
    def _():
        acc_ref[...] = jnp.zeros_like(acc_ref)

    act = jnp.dot(h_ref[...], ut_ref[...], preferred_element_type=F32)
    rt = w_ref.shape[2]
    for j in range(w_ref.shape[0]):
        for k in range(w_ref.shape[1]):
            a = act[j * rt:(j + 1) * rt, k * PEER_NKEYS:(k + 1) * PEER_NKEYS].astype(BF16)
            gate_sc[j * rt:(j + 1) * rt, k * PEER_NKEYS:(k + 1) * PEER_NKEYS] = (
                _gelu_tanh(a) * w_ref[j, k])
    acc_ref[...] += jnp.dot(gate_sc[...], v_ref[...], preferred_element_type=F32)

    @pl.when(e == pl.num_programs(2) - 1)
    def _():
        mod = mod_ref[...]
        o_ref[...] = _ln(DN_ALPHA * x1_ref[...] + mod[5:6] * acc_ref[...]) * g_ref[...] + b_ref[...]


def _experts(h2, w4, x1, mod3, ut_b, v_b, ln2_g, ln2_b):
    bsz, seq, _ = x1.shape
    tt = min(EXPERT_TOK_TILE, seq)
    ce = EXPERT_CHUNK
    rt = w4.shape[2]
    nblk = seq // tt
    tok = lambda n: pl.BlockSpec((None, tt, n), lambda b, i, e: (b, i, 0))
    return pl.pallas_call(
        _expert_kernel,
        grid=(bsz, nblk, PEER_EXPERTS // ce),
        in_specs=[tok(D_MODEL),
                  pl.BlockSpec((None, D_MODEL, ce), lambda b, i, e: (e, 0, 0)),
                  pl.BlockSpec((ce, D_MODEL), lambda b, i, e: (e, 0)),
                  pl.BlockSpec((tt // rt, ce // PEER_NKEYS, rt, PEER_NKEYS),
                               lambda b, i, e: (b * nblk + i, e, 0, 0)),
                  pl.BlockSpec((None, tt, D_MODEL), lambda b, i, e: (b, i, 0),
                               pipeline_mode=pl.Buffered(1)),
                  pl.BlockSpec((None, 6, D_MODEL), lambda b, i, e: (b, 0, 0)),
                  pl.BlockSpec((1, D_MODEL), lambda b, i, e: (0, 0)),
                  pl.BlockSpec((1, D_MODEL), lambda b, i, e: (0, 0))],
        out_specs=tok(D_MODEL),
        out_shape=jax.ShapeDtypeStruct((bsz, seq, D_MODEL), F32),
        scratch_shapes=[pltpu.VMEM((tt, D_MODEL), F32), pltpu.VMEM((tt, ce), BF16)],
        compiler_params=_cparams(("parallel", "parallel", "arbitrary")),
        name="expert",
    )(h2, ut_b, v_b, w4, x1, mod3, ln2_g, ln2_b)


def _rope_tables(seq):
    half = ATT_HEAD_DIM // 2
    inv_freq = ROPE_THETA ** (-jnp.arange(half, dtype=F32) * 2.0 / ATT_HEAD_DIM)
    ang = jnp.arange(seq, dtype=F32)[:, None] * inv_freq[None, :]
    cos = jnp.cos(ang)
    sin = jnp.sin(ang)
    zero = jnp.zeros_like(sin)
    reps = LANES // ATT_HEAD_DIM
    cos_t = jnp.tile(jnp.concatenate([cos, cos], axis=1), (1, reps))
    sa_t = jnp.tile(jnp.concatenate([-sin, zero], axis=1), (1, reps))
    sb_t = jnp.tile(jnp.concatenate([zero, sin], axis=1), (1, reps))
    return cos_t, sa_t, sb_t


def _prepare(layer_idx, p):
    lam_init = 0.8 - 0.6 * math.exp(-0.3 * layer_idx)
    lam = (jnp.exp(jnp.sum(p['attn_lambda_q1'].astype(F32) * p['attn_lambda_k1'].astype(F32)))
           - jnp.exp(jnp.sum(p['attn_lambda_q2'].astype(F32) * p['attn_lambda_k2'].astype(F32))) + lam_init)
    row = lambda a: a.astype(F32).reshape(1, -1)
    return dict(
        w_ada=p['w_ada'].astype(F32), b_ada=p['b_ada'].astype(F32),
        w_in=p['w_in'].astype(BF16),
        s5w=_s5_weights(p['s5_lambda_re'], p['s5_lambda_im'], p['s5_log_dt'], p['s5_b_re'],
                        p['s5_b_im'], p['s5_c_re'], p['s5_c_im']),
        s5_d=row(p['s5_d']), w_glu=p['s5_w_glu'].astype(BF16), w_s5_out=p['w_s5_out'].astype(BF16),
        lam=lam.reshape(1).astype(F32),
        subln=row(p['attn_subln_g']) * (1.0 - lam_init),
        w_attn_out=p['w_attn_out'].astype(BF16), w_o=p['w_o'].astype(BF16),
        ln1_g=row(p['ln1_g']), ln1_b=row(p['ln1_b']),
        w_q=p['peer_w_q'].astype(BF16),
        keys=p['peer_keys'].astype(BF16).reshape(2 * PEER_HEADS, PEER_NKEYS, PEER_HALF),
        ut=(p['peer_u'].astype(BF16).reshape(PEER_EXPERTS // EXPERT_CHUNK, EXPERT_CHUNK, D_MODEL)
            .transpose(0, 2, 1)),
        v=p['peer_v'].astype(BF16),
        ln2_g=row(p['ln2_g']), ln2_b=row(p['ln2_b']),
    )


def _encoder_layer(x, c, wts, rope):
    bsz, seq, _ = x.shape
    mod3 = _ada(c.astype(F32), wts['w_ada'], wts['b_ada']).reshape(bsz, 6, D_MODEL)
    u, q, k, v, ga, gb = _inproj(x.astype(F32), mod3, wts['w_in'], *rope)
    y = _s5(u, wts['s5w'])
    o = _attention(wts['lam'], q, k, v, wts['subln'])
    x1, h2, pq = _merge(x.astype(F32), u, y, o, ga, gb, mod3, wts)
    w4 = _route(pq.reshape(bsz * seq, PEER_NQ), wts['keys'])
    x2 = _experts(h2, w4, x1, mod3, wts['ut'], wts['v'], wts['ln2_g'], wts['ln2_b'])
    return x2.astype(x.dtype)


def kernel(x_prompt, x_sample, c_prompt, c_sample, w_ada, b_ada, w_in, s5_lambda_re, s5_lambda_im, s5_log_dt, s5_b_re, s5_b_im, s5_c_re, s5_c_im, s5_d, s5_w_glu, w_s5_out, attn_lambda_q1, attn_lambda_k1, attn_lambda_q2, attn_lambda_k2, attn_subln_g, w_attn_out, w_o, ln1_g, ln1_b, peer_w_q, peer_keys, peer_u, peer_v, ln2_g, ln2_b):
    params = dict(w_ada=w_ada, b_ada=b_ada, w_in=w_in, s5_lambda_re=s5_lambda_re,
                  s5_lambda_im=s5_lambda_im, s5_log_dt=s5_log_dt, s5_b_re=s5_b_re, s5_b_im=s5_b_im,
                  s5_c_re=s5_c_re, s5_c_im=s5_c_im, s5_d=s5_d, s5_w_glu=s5_w_glu, w_s5_out=w_s5_out,
                  attn_lambda_q1=attn_lambda_q1, attn_lambda_k1=attn_lambda_k1,
                  attn_lambda_q2=attn_lambda_q2, attn_lambda_k2=attn_lambda_k2,
                  attn_subln_g=attn_subln_g, w_attn_out=w_attn_out, w_o=w_o, ln1_g=ln1_g, ln1_b=ln1_b,
                  peer_w_q=peer_w_q, peer_keys=peer_keys, peer_u=peer_u, peer_v=peer_v,
                  ln2_g=ln2_g, ln2_b=ln2_b)
    y_prompt, y_sample = x_prompt, x_sample
    rope_p = _rope_tables(x_prompt.shape[1])
    rope_s = _rope_tables(x_sample.shape[1])
    for l in range(DEPTH):
        wts = _prepare(l, {name: a[l] for name, a in params.items()})
        y_prompt = _encoder_layer(y_prompt, c_prompt, wts, rope_p)
        y_sample = _encoder_layer(y_sample, c_sample, wts, rope_s)
    return (y_prompt, y_sample)
```

```python
import functools
import math

import jax
import jax.numpy as jnp
from jax import lax
from jax.experimental import pallas as pl
from jax.experimental.pallas import tpu as pltpu

F32 = jnp.float32
BF16 = jnp.bfloat16
HIGHEST = lax.Precision.HIGHEST

D_MODEL = 1024
DEPTH = 1
S5_WIDTH = D_MODEL // 2
S5_GROUP_CH = 16
S5_GROUPS = S5_WIDTH // S5_GROUP_CH
S5_STATE = 64
S5_CHUNK = 16
S5_ROW = S5_CHUNK * S5_GROUP_CH
ATT_HEADS = 8
ATT_HEAD_DIM = D_MODEL // (2 * ATT_HEADS)
ATT_WIDTH = ATT_HEADS * 2 * ATT_HEAD_DIM
ROPE_THETA = 10000.0
N_IN = S5_WIDTH + 3 * ATT_WIDTH + 2 * D_MODEL
PEER_HEADS = 8
PEER_NKEYS = 128
PEER_EXPERTS = PEER_NKEYS * PEER_NKEYS
PEER_TOPK = 16
PEER_QDIM = 256
PEER_HALF = PEER_QDIM // 2
PEER_NQ = PEER_HEADS * PEER_QDIM
DN_ALPHA = (2 * DEPTH) ** 0.25
LN_EPS = 1e-5

LANES = 128
VMEM_LIMIT_BYTES = 56 * 1024 * 1024
NEG_BIG = -3.0e38
POS_BIG = 3.0e38

IN_TILE = 512
IN_ROW_CHUNKS = 2
ATT_TILE = 2048
ATT_ROW_CHUNKS = 4
MERGE_TILE = 512
MERGE_ROW_CHUNKS = 2
ROUTE_TILE = 128
DEPTH_DIRECT = 4
ROUTE_PAIRS_PER_TRIP = 2
EXPERT_TOK_TILE = 1024
EXPERT_CHUNK = 1024


def _merge_exchange_pairs(n):
    pairs = []
    p = 1
    while p < n:
        k = p
        while k >= 1:
            for j in range(k % p, n - k, 2 * k):
                for i in range(min(k, n - j - k)):
                    if (i + j) // (2 * p) == (i + j + k) // (2 * p):
                        pairs.append((i + j, i + j + k))
            k //= 2
        p *= 2
    return pairs


_SORT16_PAIRS = _merge_exchange_pairs(PEER_NKEYS // 8)
_SORT8_PAIRS = _merge_exchange_pairs(8)


def _cparams(sem):
    return pltpu.CompilerParams(dimension_semantics=sem, vmem_limit_bytes=VMEM_LIMIT_BYTES)


def _const_spec(shape):
    nd = len(shape)
    return pl.BlockSpec(shape, lambda *_: (0,) * nd, pipeline_mode=pl.Buffered(1))


def _ln(x):
    mu = jnp.mean(x, axis=-1, keepdims=True)
    xc = x - mu
    var = jnp.mean(xc * xc, axis=-1, keepdims=True)
    return xc * lax.rsqrt(var + LN_EPS)


def _gelu_tanh(x):
    return 0.5 * x * (1.0 + jnp.tanh(math.sqrt(2.0 / math.pi) * (x + 0.044715 * (x * x * x))))


def _sigmoid(x):
    return 1.0 / (1.0 + jnp.exp(-x))


def _ada_kernel(c_ref, w_ref, b_ref, o_ref):
    c = c_ref[...]
    s = c * _sigmoid(c)
    o_ref[...] = jnp.dot(s, w_ref[...], precision=HIGHEST, preferred_element_type=F32) + b_ref[...]


def _ada(c, w_ada, b_ada):
    bsz = c.shape[0]
    nblk = w_ada.shape[1] // D_MODEL
    return pl.pallas_call(
        _ada_kernel,
        grid=(nblk,),
        in_specs=[
            pl.BlockSpec((bsz, D_MODEL), lambda j: (0, 0)),
            pl.BlockSpec((D_MODEL, D_MODEL), lambda j: (0, j)),
            pl.BlockSpec((1, D_MODEL), lambda j: (0, j)),
        ],
        out_specs=pl.BlockSpec((bsz, D_MODEL), lambda j: (0, j)),
        out_shape=jax.ShapeDtypeStruct((bsz, nblk * D_MODEL), F32),
        compiler_params=_cparams(("arbitrary",)),
        name="ada",
    )(c, w_ada, b_ada.reshape(1, -1))


def _in_kernel(x_ref, mod_ref, w_ref, cos_ref, sa_ref, sb_ref,
               u_ref, q_ref, k_ref, v_ref, ga_ref, gb_ref):
    mod = mod_ref[...]
    rc = x_ref.shape[0] // IN_ROW_CHUNKS
    rows = [slice(c * rc, (c + 1) * rc) for c in range(IN_ROW_CHUNKS)]
    hbs = [(_ln(x_ref[r, :]) * (1.0 + mod[1:2]) + mod[0:1]).astype(BF16) for r in rows]

    def proj(c, a, b):
        return jnp.dot(hbs[c], w_ref[:, a:b], preferred_element_type=F32)

    o1 = S5_WIDTH
    o2 = o1 + ATT_WIDTH
    o3 = o2 + ATT_WIDTH
    o4 = o3 + ATT_WIDTH
    o5 = o4 + D_MODEL
    half = ATT_HEAD_DIM // 2
    q_scale = ATT_HEAD_DIM ** -0.5 * math.log2(math.e)

    def rope_store(t, o_ref, r, scale):
        cos, sa, sb = cos_ref[r, :] * scale, sa_ref[r, :] * scale, sb_ref[r, :] * scale
        for j in range(ATT_WIDTH // LANES):
            tj = t[:, j * LANES:(j + 1) * LANES]
            rot = tj * cos + pltpu.roll(tj, LANES - half, 1) * sa + pltpu.roll(tj, half, 1) * sb
            o_ref[r, j * LANES:(j + 1) * LANES] = rot.astype(BF16)

    for c, r in enumerate(rows):
        u_ref[r, :] = proj(c, 0, o1).astype(BF16)
    for c, r in enumerate(rows):
        rope_store(proj(c, o1, o2), q_ref, r, q_scale)
    for c, r in enumerate(rows):
        rope_store(proj(c, o2, o3), k_ref, r, 1.0)
    for c, r in enumerate(rows):
        v_ref[r, :] = proj(c, o3, o4).astype(BF16)
    for c, r in enumerate(rows):
        ga_ref[r, :] = _sigmoid(proj(c, o4, o5)).astype(BF16)
    for c, r in enumerate(rows):
        gb_ref[r, :] = _sigmoid(proj(c, o5, N_IN)).astype(BF16)


def _inproj(x, mod3, w_in_b, cos, sa, sb):
    bsz, seq, _ = x.shape
    tm = min(IN_TILE, seq)
    tok = lambda n: pl.BlockSpec((None, tm, n), lambda b, i: (b, i, 0))
    rope = pl.BlockSpec((tm, LANES), lambda b, i: (i, 0))
    outs = [(S5_WIDTH, BF16), (ATT_WIDTH, BF16), (ATT_WIDTH, BF16), (ATT_WIDTH, BF16),
            (D_MODEL, BF16), (D_MODEL, BF16)]
    return pl.pallas_call(
        _in_kernel,
        grid=(bsz, seq // tm),
        in_specs=[tok(D_MODEL),
                  pl.BlockSpec((None, 6, D_MODEL), lambda b, i: (b, 0, 0)),
                  _const_spec((D_MODEL, N_IN)), rope, rope, rope],
        out_specs=[tok(n) for n, _ in outs],
        out_shape=[jax.ShapeDtypeStruct((bsz, seq, n), dt) for n, dt in outs],
        compiler_params=_cparams(("parallel", "parallel")),
        name="inproj",
    )(x, mod3, w_in_b, cos, sa, sb)


def _s5_weights(lam_re, lam_im, log_dt, b_re, b_im, c_re, c_im):
    n = S5_CHUNK
    per_dir = []
    for d in range(2):
        lr = lam_re[d].astype(F32)
        li = lam_im[d].astype(F32)
        dt = jnp.exp(log_dt[d].astype(F32))[:, None]
        mag = jnp.exp(lr * dt)
        lbr = mag * jnp.cos(li * dt)
        lbi = mag * jnp.sin(li * dt)
        den = lr * lr + li * li
        nr = lbr - 1.0
        cr = (nr * lr + lbi * li) / den
        ci = (lbi * lr - nr * li) / den
        br = b_re[d].astype(F32)
        bi = b_im[d].astype(F32)
        bbr = cr[..., None] * br - ci[..., None] * bi
        bbi = cr[..., None] * bi + ci[..., None] * br
        pr = [jnp.ones_like(lbr)]
        pi = [jnp.zeros_like(lbi)]
        for _ in range(n):
            pr.append(pr[-1] * lbr - pi[-1] * lbi)
            pi.append(pr[-2] * lbi + pi[-1] * lbr)
        pr = jnp.stack(pr)
        pi = jnp.stack(pi)
        cre = c_re[d].astype(F32)
        cim = c_im[d].astype(F32)
        cpr = cre[None] * pr[:, :, None, :] - cim[None] * pi[:, :, None, :]
        cpi = cre[None] * pi[:, :, None, :] + cim[None] * pr[:, :, None, :]
        kern = (jnp.einsum('ngcp,gpd->ngcd', cpr, bbr, precision=HIGHEST)
                - jnp.einsum('ngcp,gpd->ngcd', cpi, bbi, precision=HIGHEST))
        pbr = pr[..., None] * bbr[None] - pi[..., None] * bbi[None]
        pbi = pr[..., None] * bbi[None] + pi[..., None] * bbr[None]
        per_dir.append(dict(pr=pr, pi=pi, cpr=cpr, cpi=cpi, kern=kern, pbr=pbr, pbi=pbi))

    g, c, p = S5_GROUPS, S5_GROUP_CH, S5_STATE
    s_idx = jnp.arange(n)[:, None]
    t_idx = jnp.arange(n)[None, :]
    tau_f = t_idx - s_idx
    tau_b = s_idx - t_idx
    kf = per_dir[0]['kern'][jnp.clip(tau_f, 0, n)] * (tau_f >= 0)[..., None, None, None]
    kb = per_dir[1]['kern'][jnp.clip(tau_b, 0, n)] * (tau_b >= 0)[..., None, None, None]
    m = (kf + kb).transpose(2, 0, 4, 1, 3).reshape(g, S5_ROW, S5_ROW)

    pow_f = n - 1 - jnp.arange(n)
    pow_b = jnp.arange(n)

    def ws_part(key, d, pows):
        return per_dir[d][key][pows].transpose(1, 0, 3, 2).reshape(g, S5_ROW, p)

    ws = jnp.concatenate([ws_part('pbr', 0, pow_f), ws_part('pbr', 1, pow_b),
                          ws_part('pbi', 0, pow_f), ws_part('pbi', 1, pow_b)], axis=-1)

    out_f = jnp.arange(n) + 1
    out_b = n - jnp.arange(n)

    def wo_part(key, d, pows):
        return per_dir[d][key][pows].transpose(1, 3, 0, 2).reshape(g, p, S5_ROW)

    zero = jnp.zeros((g, p, S5_ROW), F32)
    wof = jnp.concatenate([wo_part('cpr', 0, out_f), zero, -wo_part('cpi', 0, out_f), zero], axis=1)
    wob = jnp.concatenate([zero, wo_part('cpr', 1, out_b), zero, -wo_part('cpi', 1, out_b)], axis=1)
    ar = jnp.concatenate([per_dir[0]['pr'][n], per_dir[1]['pr'][n]], axis=-1)[:, None, :]
    ai = jnp.concatenate([per_dir[0]['pi'][n], per_dir[1]['pi'][n]], axis=-1)[:, None, :]
    return m.astype(BF16), ws.astype(BF16), wof.astype(BF16), wob.astype(BF16), ar, ai


def _s5_kernel(u_ref, m_ref, ws_ref, wof_ref, wob_ref, ar_ref, ai_ref, y_ref,
               s_ref, hf_ref, hb_ref, *, nb, nchunk):
    u = u_ref[...]
    s_ref[...] = jnp.dot(u, ws_ref[...], preferred_element_type=F32)
    ar = ar_ref[...]
    ai = ai_ref[...]
    is_fwd = lax.broadcasted_iota(jnp.int32, (nb, LANES), 1) < S5_STATE

    def step(i, carry):
        h_re, h_im = carry
        rf = pl.multiple_of(i * nb, 8)
        rb = pl.multiple_of((nchunk - 1 - i) * nb, 8)
        hf_ref[pl.ds(rf, nb), 0:LANES] = h_re
        hf_ref[pl.ds(rf, nb), LANES:2 * LANES] = h_im
        hb_ref[pl.ds(rb, nb), 0:LANES] = h_re
        hb_ref[pl.ds(rb, nb), LANES:2 * LANES] = h_im
        s_re = jnp.where(is_fwd, s_ref[pl.ds(rf, nb), 0:LANES], s_ref[pl.ds(rb, nb), 0:LANES])
        s_im = jnp.where(is_fwd, s_ref[pl.ds(rf, nb), LANES:2 * LANES],
                         s_ref[pl.ds(rb, nb), LANES:2 * LANES])
        return (ar * h_re - ai * h_im + s_re, ar * h_im + ai * h_re + s_im)

    zero = jnp.zeros((nb, LANES), F32)
    lax.fori_loop(0, nchunk, step, (zero, zero))
    y = jnp.dot(u, m_ref[...], preferred_element_type=F32)
    y = y + jnp.dot(hf_ref[...].astype(BF16), wof_ref[...], preferred_element_type=F32)
    y = y + jnp.dot(hb_ref[...].astype(BF16), wob_ref[...], preferred_element_type=F32)
    y_ref[...] = y.astype(BF16)


def _s5(u, s5w):
    m, ws, wof, wob, ar, ai = s5w
    bsz, seq, _ = u.shape
    nchunk = seq // S5_CHUNK
    rows = nchunk * bsz
    u2 = (u.reshape(bsz, nchunk, S5_CHUNK, S5_GROUPS, S5_GROUP_CH)
          .transpose(3, 1, 0, 2, 4).reshape(S5_GROUPS, rows, S5_ROW))
    grp = lambda r, c: pl.BlockSpec((None, r, c), lambda g: (g, 0, 0))
    y2 = pl.pallas_call(
        functools.partial(_s5_kernel, nb=bsz, nchunk=nchunk),
        grid=(S5_GROUPS,),
        in_specs=[grp(rows, S5_ROW), grp(S5_ROW, S5_ROW), grp(S5_ROW, S5_ROW),
                  grp(S5_ROW, S5_ROW), grp(S5_ROW, S5_ROW), grp(1, LANES), grp(1, LANES)],
        out_specs=grp(rows, S5_ROW),
        out_shape=jax.ShapeDtypeStruct((S5_GROUPS, rows, S5_ROW), BF16),
        scratch_shapes=[pltpu.VMEM((rows, S5_ROW), F32)] * 3,
        compiler_params=_cparams(("parallel",)),
        name="s5",
    )(u2, m, ws, wof, wob, ar, ai)
    return (y2.reshape(S5_GROUPS, nchunk, bsz, S5_CHUNK, S5_GROUP_CH)
            .transpose(2, 1, 3, 0, 4).reshape(bsz, seq, S5_WIDTH))


def _attn_kernel(lam_ref, q_ref, k_ref, v_ref, g_ref, o_ref, vx_ref):
    @pl.when(pl.program_id(2) == 0)
    def _():
        vx_ref[:, 0:LANES] = v_ref[...]
        vx_ref[:, LANES:2 * LANES] = jnp.ones(v_ref.shape, BF16)

    k = k_ref[...]
    vx = vx_ref[...]
    rows_per_chunk = q_ref.shape[0] // ATT_ROW_CHUNKS

    def scores(qm):
        return lax.dot_general(qm, k, (((1,), (1,)), ((), ())), preferred_element_type=F32)

    def probs(s):
        return jnp.exp2(s - jnp.max(s, axis=-1, keepdims=True)).astype(BF16)

    def weighted(e):
        ol = jnp.dot(e, vx, preferred_element_type=F32)
        return ol[:, 0:LANES] / ol[:, LANES:2 * LANES]

    chains = []
    for c in range(ATT_ROW_CHUNKS):
        rows = slice(c * rows_per_chunk, (c + 1) * rows_per_chunk)
        q = q_ref[rows, :]
        first = lax.broadcasted_iota(jnp.int32, q.shape, 1) < ATT_HEAD_DIM
        zero = jnp.zeros_like(q)
        chains.append((rows, jnp.where(first, q, zero)))
        chains.append((rows, jnp.where(first, zero, q)))

    s_val, e_val, o_val = {}, {}, {}
    for t in range(len(chains) + 2):
        if t < len(chains):
            s_val[t] = scores(chains[t][1])
        if 0 <= t - 1 < len(chains):
            e_val[t - 1] = probs(s_val.pop(t - 1))
        if 0 <= t - 2 < len(chains):
            o_val[t - 2] = weighted(e_val.pop(t - 2))
            if (t - 2) % 2 == 1:
                o = o_val.pop(t - 3) - lam_ref[0] * o_val.pop(t - 2)
                o = o * lax.rsqrt(jnp.mean(o * o, axis=-1, keepdims=True) + LN_EPS) * g_ref[...]
                o_ref[chains[t - 2][0], :] = o.astype(BF16)


def _attention(lam, q, k, v, g_row):
    bsz, seq, _ = q.shape
    tq = min(ATT_TILE, seq)
    return pl.pallas_call(
        _attn_kernel,
        grid=(bsz, ATT_HEADS, seq // tq),
        in_specs=[pl.BlockSpec(memory_space=pltpu.SMEM),
                  pl.BlockSpec((None, tq, LANES), lambda b, h, i: (b, i, h)),
                  pl.BlockSpec((None, seq, LANES), lambda b, h, i: (b, 0, h)),
                  pl.BlockSpec((None, seq, LANES), lambda b, h, i: (b, 0, h)),
                  pl.BlockSpec((1, LANES), lambda b, h, i: (0, 0))],
        out_specs=pl.BlockSpec((None, tq, LANES), lambda b, h, i: (b, i, h)),
        out_shape=jax.ShapeDtypeStruct((bsz, seq, ATT_WIDTH), BF16),
        scratch_shapes=[pltpu.VMEM((seq, 2 * LANES), BF16)],
        compiler_params=_cparams(("parallel", "parallel", "arbitrary")),
        name="attn",
    )(lam, q, k, v, g_row)


def _merge_kernel(x_ref, u_ref, y_ref, o_ref, ga_ref, gb_ref, mod_ref, d_ref, wglu_ref, ws5_ref,
                  watt_ref, wo_ref, ln1g_ref, ln1b_ref, wq_ref, x1_ref, h2_ref, pq_ref):
    mod = mod_ref[...]
    rc = x_ref.shape[0] // MERGE_ROW_CHUNKS
    rows = [slice(c * rc, (c + 1) * rc) for c in range(MERGE_ROW_CHUNKS)]
    dot = lambda a, b: jnp.dot(a, b, preferred_element_type=F32)
    s = [_gelu_tanh(u_ref[r, :].astype(F32) * d_ref[...] + y_ref[r, :].astype(F32)) for r in rows]
    glu = [dot(v.astype(BF16), wglu_ref[...]) for v in s]
    s = [a * _sigmoid(b) for a, b in zip(s, glu)]
    br_b = [dot(o_ref[r, :], watt_ref[...]) for r in rows]
    br_a = [dot(v.astype(BF16), ws5_ref[...]) for v in s]
    merged = [ga_ref[r, :].astype(F32) * a + gb_ref[r, :].astype(F32) * b
              for r, a, b in zip(rows, br_a, br_b)]
    z = [dot(v.astype(BF16), wo_ref[...]) for v in merged]
    h2s = []
    for r, zc in zip(rows, z):
        x1 = _ln(DN_ALPHA * x_ref[r, :] + mod[2:3] * zc) * ln1g_ref[...] + ln1b_ref[...]
        x1_ref[r, :] = x1
        h2 = (_ln(x1) * (1.0 + mod[4:5]) + mod[3:4]).astype(BF16)
        h2_ref[r, :] = h2
        h2s.append(h2)
    for r, h2 in zip(rows, h2s):
        pq_ref[r, :] = dot(h2, wq_ref[...]).astype(BF16)


def _merge(x, u, y, o, ga, gb, mod3, wts):
    bsz, seq, _ = x.shape
    tm = min(MERGE_TILE, seq)
    tok = lambda n: pl.BlockSpec((None, tm, n), lambda b, i: (b, i, 0))
    row = lambda n: _const_spec((1, n))
    return pl.pallas_call(
        _merge_kernel,
        grid=(bsz, seq // tm),
        in_specs=[tok(D_MODEL), tok(S5_WIDTH), tok(S5_WIDTH), tok(ATT_WIDTH), tok(D_MODEL), tok(D_MODEL),
                  pl.BlockSpec((None, 6, D_MODEL), lambda b, i: (b, 0, 0)),
                  row(S5_WIDTH), _const_spec((S5_WIDTH, S5_WIDTH)), _const_spec((S5_WIDTH, D_MODEL)),
                  _const_spec((ATT_WIDTH, D_MODEL)), _const_spec((D_MODEL, D_MODEL)),
                  row(D_MODEL), row(D_MODEL), _const_spec((D_MODEL, PEER_NQ))],
        out_specs=[tok(D_MODEL), tok(D_MODEL), tok(PEER_NQ)],
        out_shape=[jax.ShapeDtypeStruct((bsz, seq, D_MODEL), F32),
                   jax.ShapeDtypeStruct((bsz, seq, D_MODEL), BF16),
                   jax.ShapeDtypeStruct((bsz, seq, PEER_NQ), BF16)],
        compiler_params=_cparams(("parallel", "parallel")),
        name="merge",
    )(x, u, y, o, ga, gb, mod3, wts['s5_d'], wts['w_glu'], wts['w_s5_out'], wts['w_attn_out'],
      wts['w_o'], wts['ln1_g'], wts['ln1_b'], wts['w_q'])


def _route_kernel(pq_ref, keys_ref, w_ref, a_sc, b_sc, wt_sc, ix_sc, wr_sc, ir_sc, dp_sc, e2_sc):
    tn = pq_ref.shape[0]
    row8 = lax.broadcasted_iota(jnp.int32, (8, tn), 0)
    row8_f = row8.astype(F32)
    nt_dims = (((1,), (1,)), ((), ()))
    nv = PEER_NKEYS // 8

    def sorted_slabs(st, with_index):
        vals = [st[8 * v:8 * v + 8, :] for v in range(nv)]
        idxs = [row8_f + float(8 * v) for v in range(nv)] if with_index else None
        for lo, hi in _SORT16_PAIRS:
            a, b = vals[lo], vals[hi]
            vals[lo], vals[hi] = jnp.maximum(a, b), jnp.minimum(a, b)
            if with_index:
                keep = a >= b
                ia, ib = idxs[lo], idxs[hi]
                idxs[lo], idxs[hi] = jnp.where(keep, ia, ib), jnp.where(keep, ib, ia)
        return vals, idxs

    def pop_heads(vals, idxs, pick, k):
        for d in range(min(len(vals) - 1, PEER_TOPK - 1 - k)):
            vals[d] = jnp.where(pick, vals[d + 1], vals[d])
            if idxs is not None:
                idxs[d] = jnp.where(pick, idxs[d + 1], idxs[d])

    for h in range(PEER_HEADS):
        q1 = pq_ref[:, (2 * h) * PEER_HALF:(2 * h + 1) * PEER_HALF]
        q2 = pq_ref[:, (2 * h + 1) * PEER_HALF:(2 * h + 2) * PEER_HALF]
        st1 = lax.dot_general(keys_ref[2 * h], q1, nt_dims, preferred_element_type=F32)
        st2 = lax.dot_general(keys_ref[2 * h + 1], q2, nt_dims, preferred_element_type=F32)
        s2 = st2.T
        e2_sc[h] = jnp.exp(s2 - jnp.max(s2, axis=-1, keepdims=True))

        vals, idxs = sorted_slabs(st1, True)
        a_vals = []
        for k in range(PEER_TOPK):
            m = jnp.max(vals[0], axis=0, keepdims=True)
            ix = jnp.min(jnp.where(vals[0] == m, idxs[0], float(PEER_NKEYS)), axis=0, keepdims=True)
            pop_heads(vals, idxs, idxs[0] == ix, k)
            a_vals.append(m)
            a_sc[k:k + 1, :] = m
            ix_sc[h * PEER_TOPK + k:h * PEER_TOPK + k + 1, :] = ix
        vals, _ = sorted_slabs(st2, False)
        for k in range(PEER_TOPK):
            m = jnp.max(vals[0], axis=0, keepdims=True)
            pop_heads(vals, None, vals[0] == m, k)
            b_sc[k:k + 1, :] = m
        b_lo = b_sc[0:8, :]
        b_hi = b_sc[8:16, :]
        b0 = b_sc[0:1, :]

        shift = lambda x, n: pltpu.roll(x, n, 0)
        cands = [
            a_vals[0] + b_lo,
            a_vals[0] + b_hi,
            a_vals[1] + b_lo,
            jnp.where(row8 < 5, a_vals[2] + b_lo, a_vals[4] + shift(b_lo, 5)),
            jnp.where(row8 < 4, a_vals[3] + b_lo,
                      jnp.where(row8 < 6, a_vals[5] + shift(b_lo, 4), a_vals[6] + shift(b_lo, 6))),
            jnp.where(row8 < 2, a_vals[7] + b_lo, a_sc[6:14, :] + b0),
            jnp.where(row8 < 2, shift(a_sc[8:16, :], 2) + b0, NEG_BIG),
        ]
        m0 = a_vals[0] + b0
        z = jnp.zeros_like(m0)
        tau = m0
        cands.append(jnp.full_like(cands[0], NEG_BIG))
        for lo, hi in _SORT8_PAIRS:
            a, b = cands[lo], cands[hi]
            cands[lo], cands[hi] = jnp.maximum(a, b), jnp.minimum(a, b)
        for k in range(PEER_TOPK):
            tau = jnp.max(cands[0], axis=0, keepdims=True)
            z = z + jnp.exp(tau - m0)
            pop_heads(cands, None, cands[0] == tau, k)
        inv_z = 1.0 / z
        for k in range(PEER_TOPK):
            j = h * PEER_TOPK + k
            wt_sc[j:j + 1, :] = jnp.exp(a_vals[k] - a_vals[0]) * inv_z

        depth = jnp.zeros_like(st2)
        for k in range(DEPTH_DIRECT):
            depth = depth + jnp.where(a_vals[k] + st2 >= tau, 1.0, 0.0)
        for l in range(PEER_TOPK // (DEPTH_DIRECT + 1)):
            b_l = b_sc[l:l + 1, :]
            extra = jnp.zeros_like(b_l)
            for k in range(DEPTH_DIRECT, PEER_TOPK // (l + 1)):
                extra = extra + jnp.where(a_vals[k] + b_l >= tau, 1.0, 0.0)
            depth = depth + jnp.where(st2 == b_l, extra, 0.0)
        dp_sc[h] = depth.T

    wr_sc[...] = wt_sc[...].T
    ir_sc[...] = ix_sc[...].T
    rank_j = (lax.broadcasted_iota(jnp.int32, (PEER_NKEYS, PEER_NKEYS), 0) % PEER_TOPK).astype(F32).astype(BF16)
    sub_j = lax.broadcasted_iota(jnp.int32, (PEER_NKEYS, PEER_NKEYS), 0).astype(F32).astype(BF16)
    zero_b = jnp.zeros((PEER_NKEYS, PEER_NKEYS), BF16)

    def token(t):
        def head_rows(ref):
            return jnp.concatenate(
                [jnp.broadcast_to(ref[h, pl.ds(t, 1), :], (PEER_TOPK, PEER_NKEYS)).astype(BF16)
                 for h in range(PEER_HEADS)], axis=0)

        def all_rows(ref):
            return jnp.broadcast_to(ref[pl.ds(t, 1), :], (PEER_NKEYS, PEER_NKEYS)).astype(BF16)

        r = jnp.where(rank_j < head_rows(dp_sc), head_rows(e2_sc), zero_b)
        pt = jnp.where(sub_j == all_rows(ir_sc), all_rows(wr_sc), zero_b)
        return jnp.dot(pt, r, preferred_element_type=F32)

    def token_group(g, carry):
        for pair in range(ROUTE_PAIRS_PER_TRIP):
            t0 = pl.multiple_of((g * ROUTE_PAIRS_PER_TRIP + pair) * 16, 16)
            halves = []
            for half in range(2):
                tiles = jnp.stack([token(t0 + 8 * half + u) for u in range(8)])
                halves.append(jnp.swapaxes(tiles, 0, 1))
            w_ref[:, pl.ds(t0, 16), :] = jnp.concatenate(halves, axis=1).astype(BF16)
        return carry

    lax.fori_loop(0, tn // (16 * ROUTE_PAIRS_PER_TRIP), token_group, 0)


def _route(pq, keys_b):
    ntok = pq.shape[0]
    tn = ROUTE_TILE
    nj = PEER_HEADS * PEER_TOPK
    return pl.pallas_call(
        _route_kernel,
        grid=(ntok // tn,),
        in_specs=[pl.BlockSpec((tn, PEER_NQ), lambda i: (i, 0)),
                  _const_spec((2 * PEER_HEADS, PEER_NKEYS, PEER_HALF))],
        out_specs=pl.BlockSpec((None, PEER_NKEYS, tn, PEER_NKEYS), lambda i: (i, 0, 0, 0)),
        out_shape=jax.ShapeDtypeStruct((ntok // tn, PEER_NKEYS, tn, PEER_NKEYS), BF16),
        scratch_shapes=[pltpu.VMEM((PEER_TOPK, tn), F32), pltpu.VMEM((PEER_TOPK, tn), F32),
                        pltpu.VMEM((nj, tn), F32), pltpu.VMEM((nj, tn), F32),
                        pltpu.VMEM((tn, nj), F32), pltpu.VMEM((tn, nj), F32),
                        pltpu.VMEM((PEER_HEADS, tn, PEER_NKEYS), F32),
                        pltpu.VMEM((PEER_HEADS, tn, PEER_NKEYS), F32)],
        compiler_params=_cparams(("parallel",)),
        name="route",
    )(pq, keys_b)


def _expert_kernel(h_ref, ut_ref, v_ref, w_ref, x1_ref, mod_ref, g_ref, b_ref, o_ref, acc_ref, gate_sc):
    e = pl.program_id(2)

    @pl.when(e == 0)
    def _():
        acc_ref[...] = jnp.zeros_like(acc_ref)

    act = jnp.dot(h_ref[...], ut_ref[...], preferred_element_type=F32)
    rt = w_ref.shape[2]
    for j in range(w_ref.shape[0]):
        for k in range(w_ref.shape[1]):
            a = act[j * rt:(j + 1) * rt, k * PEER_NKEYS:(k + 1) * PEER_NKEYS].astype(BF16)
            gate_sc[j * rt:(j + 1) * rt, k * PEER_NKEYS:(k + 1) * PEER_NKEYS] = (
                _gelu_tanh(a) * w_ref[j, k])
    acc_ref[...] += jnp.dot(gate_sc[...], v_ref[...], preferred_element_type=F32)

    @pl.when(e == pl.num_programs(2) - 1)
    def _():
        mod = mod_ref[...]
        o_ref[...] = _ln(DN_ALPHA * x1_ref[...] + mod[5:6] * acc_ref[...]) * g_ref[...] + b_ref[...]


def _experts(h2, w4, x1, mod3, ut_b, v_b, ln2_g, ln2_b):
    bsz, seq, _ = x1.shape
    tt = min(EXPERT_TOK_TILE, seq)
    ce = EXPERT_CHUNK
    rt = w4.shape[2]
    nblk = seq // tt
    tok = lambda n: pl.BlockSpec((None, tt, n), lambda b, i, e: (b, i, 0))
    return pl.pallas_call(
        _expert_kernel,
        grid=(bsz, nblk, PEER_EXPERTS // ce),
        in_specs=[tok(D_MODEL),
                  pl.BlockSpec((None, D_MODEL, ce), lambda b, i, e: (e, 0, 0)),
                  pl.BlockSpec((ce, D_MODEL), lambda b, i, e: (e, 0)),
                  pl.BlockSpec((tt // rt, ce // PEER_NKEYS, rt, PEER_NKEYS),
                               lambda b, i, e: (b * nblk + i, e, 0, 0)),
                  pl.BlockSpec((None, tt, D_MODEL), lambda b, i, e: (b, i, 0),
                               pipeline_mode=pl.Buffered(1)),
                  pl.BlockSpec((None, 6, D_MODEL), lambda b, i, e: (b, 0, 0)),
                  pl.BlockSpec((1, D_MODEL), lambda b, i, e: (0, 0)),
                  pl.BlockSpec((1, D_MODEL), lambda b, i, e: (0, 0))],
        out_specs=tok(D_MODEL),
        out_shape=jax.ShapeDtypeStruct((bsz, seq, D_MODEL), F32),
        scratch_shapes=[pltpu.VMEM((tt, D_MODEL), F32), pltpu.VMEM((tt, ce), BF16)],
        compiler_params=_cparams(("parallel", "parallel", "arbitrary")),
        name="expert",
    )(h2, ut_b, v_b, w4, x1, mod3, ln2_g, ln2_b)


def _rope_tables(seq):
    half = ATT_HEAD_DIM // 2
    inv_freq = ROPE_THETA ** (-jnp.arange(half, dtype=F32) * 2.0 / ATT_HEAD_DIM)
    ang = jnp.arange(seq, dtype=F32)[:, None] * inv_freq[None, :]
    cos = jnp.cos(ang)
    sin = jnp.sin(ang)
    zero = jnp.zeros_like(sin)
    reps = LANES // ATT_HEAD_DIM
    cos_t = jnp.tile(jnp.concatenate([cos, cos], axis=1), (1, reps))
    sa_t = jnp.tile(jnp.concatenate([-sin, zero], axis=1), (1, reps))
    sb_t = jnp.tile(jnp.concatenate([zero, sin], axis=1), (1, reps))
    return cos_t, sa_t, sb_t


def _prepare(layer_idx, p):
    lam_init = 0.8 - 0.6 * math.exp(-0.3 * layer_idx)
    lam = (jnp.exp(jnp.sum(p['attn_lambda_q1'].astype(F32) * p['attn_lambda_k1'].astype(F32)))
           - jnp.exp(jnp.sum(p['attn_lambda_q2'].astype(F32) * p['attn_lambda_k2'].astype(F32))) + lam_init)
    row = lambda a: a.astype(F32).reshape(1, -1)
    return dict(
        w_ada=p['w_ada'].astype(F32), b_ada=p['b_ada'].astype(F32),
        w_in=p['w_in'].astype(BF16),
        s5w=_s5_weights(p['s5_lambda_re'], p['s5_lambda_im'], p['s5_log_dt'], p['s5_b_re'],
                        p['s5_b_im'], p['s5_c_re'], p['s5_c_im']),
        s5_d=row(p['s5_d']), w_glu=p['s5_w_glu'].astype(BF16), w_s5_out=p['w_s5_out'].astype(BF16),
        lam=lam.reshape(1).astype(F32),
        subln=row(p['attn_subln_g']) * (1.0 - lam_init),
        w_attn_out=p['w_attn_out'].astype(BF16), w_o=p['w_o'].astype(BF16),
        ln1_g=row(p['ln1_g']), ln1_b=row(p['ln1_b']),
        w_q=p['peer_w_q'].astype(BF16),
        keys=p['peer_keys'].astype(BF16).reshape(2 * PEER_HEADS, PEER_NKEYS, PEER_HALF),
        ut=(p['peer_u'].astype(BF16).reshape(PEER_EXPERTS // EXPERT_CHUNK, EXPERT_CHUNK, D_MODEL)
            .transpose(0, 2, 1)),
        v=p['peer_v'].astype(BF16),
        ln2_g=row(p['ln2_g']), ln2_b=row(p['ln2_b']),
    )


def _encoder_layer(x, c, wts, rope):
    bsz, seq, _ = x.shape
    mod3 = _ada(c.astype(F32), wts['w_ada'], wts['b_ada']).reshape(bsz, 6, D_MODEL)
    u, q, k, v, ga, gb = _inproj(x.astype(F32), mod3, wts['w_in'], *rope)
    y = _s5(u, wts['s5w'])
    o = _attention(wts['lam'], q, k, v, wts['subln'])
    x1, h2, pq = _merge(x.astype(F32), u, y, o, ga, gb, mod3, wts)
    w4 = _route(pq.reshape(bsz * seq, PEER_NQ), wts['keys'])
    x2 = _experts(h2, w4, x1, mod3, wts['ut'], wts['v'], wts['ln2_g'], wts['ln2_b'])
    return x2.astype(x.dtype)


def kernel(x_prompt, x_sample, c_prompt, c_sample, w_ada, b_ada, w_in, s5_lambda_re, s5_lambda_im, s5_log_dt, s5_b_re, s5_b_im, s5_c_re, s5_c_im, s5_d, s5_w_glu, w_s5_out, attn_lambda_q1, attn_lambda_k1, attn_lambda_q2, attn_lambda_k2, attn_subln_g, w_attn_out, w_o, ln1_g, ln1_b, peer_w_q, peer_keys, peer_u, peer_v, ln2_g, ln2_b):
    params = dict(w_ada=w_ada, b_ada=b_ada, w_in=w_in, s5_lambda_re=s5_lambda_re,
                  s5_lambda_im=s5_lambda_im, s5_log_dt=s5_log_dt, s5_b_re=s5_b_re, s5_b_im=s5_b_im,
                  s5_c_re=s5_c_re, s5_c_im=s5_c_im, s5_d=s5_d, s5_w_glu=s5_w_glu, w_s5_out=w_s5_out,
                  attn_lambda_q1=attn_lambda_q1, attn_lambda_k1=attn_lambda_k1,
                  attn_lambda_q2=attn_lambda_q2, attn_lambda_k2=attn_lambda_k2,
                  attn_subln_g=attn_subln_g, w_attn_out=w_attn_out, w_o=w_o, ln1_g=ln1_g, ln1_b=ln1_b,
                  peer_w_q=peer_w_q, peer_keys=peer_keys, peer_u=peer_u, peer_v=peer_v,
                  ln2_g=ln2_g, ln2_b=ln2_b)
    y_prompt, y_sample = x_prompt, x_sample
    rope_p = _rope_tables(x_prompt.shape[1])
    rope_s = _rope_tables(x_sample.shape[1])
    for l in range(DEPTH):
        wts = _prepare(l, {name: a[l] for name, a in params.items()})
        y_prompt = _encoder_layer(y_prompt, c_prompt, wts, rope_p)
        y_sample = _encoder_layer(y_sample, c_sample, wts, rope_s)
    return (y_prompt, y_sample)
```

```python
import functools
import math

import jax
import jax.numpy as jnp
from jax import lax
from jax.experimental import pallas as pl
from jax.experimental.pallas import tpu as pltpu

F32 = jnp.float32
BF16 = jnp.bfloat16
HIGHEST = lax.Precision.HIGHEST

D_MODEL = 1024
DEPTH = 1
S5_WIDTH = D_MODEL // 2
S5_GROUP_CH = 16
S5_GROUPS = S5_WIDTH // S5_GROUP_CH
S5_STATE = 64
S5_CHUNK = 16
S5_ROW = S5_CHUNK * S5_GROUP_CH
ATT_HEADS = 8
ATT_HEAD_DIM = D_MODEL // (2 * ATT_HEADS)
ATT_WIDTH = ATT_HEADS * 2 * ATT_HEAD_DIM
ROPE_THETA = 10000.0
N_IN = S5_WIDTH + 3 * ATT_WIDTH + 2 * D_MODEL
PEER_HEADS = 8
PEER_NKEYS = 128
PEER_EXPERTS = PEER_NKEYS * PEER_NKEYS
PEER_TOPK = 16
PEER_QDIM = 256
PEER_HALF = PEER_QDIM // 2
PEER_NQ = PEER_HEADS * PEER_QDIM
DN_ALPHA = (2 * DEPTH) ** 0.25
LN_EPS = 1e-5

LANES = 128
SUBLANES = 8
VMEM_LIMIT_BYTES = 56 * 1024 * 1024
NEG_BIG = -3.0e38

IN_TILE = 512
IN_ROW_CHUNKS = 2
ATT_TILE = 2048
ATT_ROW_CHUNKS = 4
MERGE_TILE = 512
MERGE_ROW_CHUNKS = 2
ROUTE_TILE = 128
DEPTH_DIRECT = 4
ROUTE_PAIRS_PER_TRIP = 2
EXPERT_TOK_TILE = 1024
EXPERT_CHUNK = 1024


def _merge_exchange_pairs(n):
    pairs = []
    p = 1
    while p < n:
        k = p
        while k >= 1:
            for j in range(k % p, n - k, 2 * k):
                for i in range(min(k, n - j - k)):
                    if (i + j) // (2 * p) == (i + j + k) // (2 * p):
                        pairs.append((i + j, i + j + k))
            k //= 2
        p *= 2
    return pairs


_SLAB_SORT_PAIRS = _merge_exchange_pairs(PEER_NKEYS // SUBLANES)
_CAND_SORT_PAIRS = _merge_exchange_pairs(8)


def _cparams(sem):
    return pltpu.CompilerParams(dimension_semantics=sem, vmem_limit_bytes=VMEM_LIMIT_BYTES)


def _const_spec(shape):
    nd = len(shape)
    return pl.BlockSpec(shape, lambda *_: (0,) * nd, pipeline_mode=pl.Buffered(1))


def _ln(x):
    mu = jnp.mean(x, axis=-1, keepdims=True)
    xc = x - mu
    var = jnp.mean(xc * xc, axis=-1, keepdims=True)
    return xc * lax.rsqrt(var + LN_EPS)


def _gelu_tanh(x):
    return 0.5 * x * (1.0 + jnp.tanh(math.sqrt(2.0 / math.pi) * (x + 0.044715 * (x * x * x))))


def _sigmoid(x):
    return 1.0 / (1.0 + jnp.exp(-x))


def _ada_kernel(c_ref, w_ref, b_ref, o_ref):
    c = c_ref[...]
    s = c * _sigmoid(c)
    o_ref[...] = jnp.dot(s, w_ref[...], precision=HIGHEST, preferred_element_type=F32) + b_ref[...]


def _ada(c, w_ada, b_ada):
    bsz = c.shape[0]
    nblk = w_ada.shape[1] // D_MODEL
    return pl.pallas_call(
        _ada_kernel,
        grid=(nblk,),
        in_specs=[
            pl.BlockSpec((bsz, D_MODEL), lambda j: (0, 0)),
            pl.BlockSpec((D_MODEL, D_MODEL), lambda j: (0, j)),
            pl.BlockSpec((1, D_MODEL), lambda j: (0, j)),
        ],
        out_specs=pl.BlockSpec((bsz, D_MODEL), lambda j: (0, j)),
        out_shape=jax.ShapeDtypeStruct((bsz, nblk * D_MODEL), F32),
        compiler_params=_cparams(("arbitrary",)),
        name="ada",
    )(c, w_ada, b_ada.reshape(1, -1))


def _in_kernel(x_ref, mod_ref, w_ref, cos_ref, sa_ref, sb_ref,
               u_ref, q_ref, k_ref, v_ref, ga_ref, gb_ref):
    mod = mod_ref[...]
    rc = x_ref.shape[0] // IN_ROW_CHUNKS
    rows = [slice(c * rc, (c + 1) * rc) for c in range(IN_ROW_CHUNKS)]
    hbs = [(_ln(x_ref[r, :]) * (1.0 + mod[1:2]) + mod[0:1]).astype(BF16) for r in rows]

    def proj(c, a, b):
        return jnp.dot(hbs[c], w_ref[:, a:b], preferred_element_type=F32)

    o1 = S5_WIDTH
    o2 = o1 + ATT_WIDTH
    o3 = o2 + ATT_WIDTH
    o4 = o3 + ATT_WIDTH
    o5 = o4 + D_MODEL
    half = ATT_HEAD_DIM // 2
    q_scale = ATT_HEAD_DIM ** -0.5 * math.log2(math.e)

    def rope_store(t, o_ref, r, scale):
        cos, sa, sb = cos_ref[r, :] * scale, sa_ref[r, :] * scale, sb_ref[r, :] * scale
        for j in range(ATT_WIDTH // LANES):
            tj = t[:, j * LANES:(j + 1) * LANES]
            rot = tj * cos + pltpu.roll(tj, LANES - half, 1) * sa + pltpu.roll(tj, half, 1) * sb
            o_ref[r, j * LANES:(j + 1) * LANES] = rot.astype(BF16)

    for c, r in enumerate(rows):
        u_ref[r, :] = proj(c, 0, o1).astype(BF16)
    for c, r in enumerate(rows):
        rope_store(proj(c, o1, o2), q_ref, r, q_scale)
    for c, r in enumerate(rows):
        rope_store(proj(c, o2, o3), k_ref, r, 1.0)
    for c, r in enumerate(rows):
        v_ref[r, :] = proj(c, o3, o4).astype(BF16)
    for c, r in enumerate(rows):
        ga_ref[r, :] = _sigmoid(proj(c, o4, o5)).astype(BF16)
    for c, r in enumerate(rows):
        gb_ref[r, :] = _sigmoid(proj(c, o5, N_IN)).astype(BF16)


def _inproj(x, mod3, w_in_b, cos, sa, sb):
    bsz, seq, _ = x.shape
    tm = min(IN_TILE, seq)
    tok = lambda n: pl.BlockSpec((None, tm, n), lambda b, i: (b, i, 0))
    rope = pl.BlockSpec((tm, LANES), lambda b, i: (i, 0))
    outs = [(S5_WIDTH, BF16), (ATT_WIDTH, BF16), (ATT_WIDTH, BF16), (ATT_WIDTH, BF16),
            (D_MODEL, BF16), (D_MODEL, BF16)]
    return pl.pallas_call(
        _in_kernel,
        grid=(bsz, seq // tm),
        in_specs=[tok(D_MODEL),
                  pl.BlockSpec((None, 6, D_MODEL), lambda b, i: (b, 0, 0)),
                  _const_spec((D_MODEL, N_IN)), rope, rope, rope],
        out_specs=[tok(n) for n, _ in outs],
        out_shape=[jax.ShapeDtypeStruct((bsz, seq, n), dt) for n, dt in outs],
        compiler_params=_cparams(("parallel", "parallel")),
        name="inproj",
    )(x, mod3, w_in_b, cos, sa, sb)


def _s5_weights(lam_re, lam_im, log_dt, b_re, b_im, c_re, c_im):
    n = S5_CHUNK
    per_dir = []
    for d in range(2):
        lr = lam_re[d].astype(F32)
        li = lam_im[d].astype(F32)
        dt = jnp.exp(log_dt[d].astype(F32))[:, None]
        mag = jnp.exp(lr * dt)
        lbr = mag * jnp.cos(li * dt)
        lbi = mag * jnp.sin(li * dt)
        den = lr * lr + li * li
        nr = lbr - 1.0
        cr = (nr * lr + lbi * li) / den
        ci = (lbi * lr - nr * li) / den
        br = b_re[d].astype(F32)
        bi = b_im[d].astype(F32)
        bbr = cr[..., None] * br - ci[..., None] * bi
        bbi = cr[..., None] * bi + ci[..., None] * br
        pr = [jnp.ones_like(lbr)]
        pi = [jnp.zeros_like(lbi)]
        for _ in range(n):
            pr.append(pr[-1] * lbr - pi[-1] * lbi)
            pi.append(pr[-2] * lbi + pi[-1] * lbr)
        pr = jnp.stack(pr)
        pi = jnp.stack(pi)
        cre = c_re[d].astype(F32)
        cim = c_im[d].astype(F32)
        cpr = cre[None] * pr[:, :, None, :] - cim[None] * pi[:, :, None, :]
        cpi = cre[None] * pi[:, :, None, :] + cim[None] * pr[:, :, None, :]
        kern = (jnp.einsum('ngcp,gpd->ngcd', cpr, bbr, precision=HIGHEST)
                - jnp.einsum('ngcp,gpd->ngcd', cpi, bbi, precision=HIGHEST))
        pbr = pr[..., None] * bbr[None] - pi[..., None] * bbi[None]
        pbi = pr[..., None] * bbi[None] + pi[..., None] * bbr[None]
        per_dir.append(dict(pr=pr, pi=pi, cpr=cpr, cpi=cpi, kern=kern, pbr=pbr, pbi=pbi))

    g, c, p = S5_GROUPS, S5_GROUP_CH, S5_STATE
    s_idx = jnp.arange(n)[:, None]
    t_idx = jnp.arange(n)[None, :]
    tau_f = t_idx - s_idx
    tau_b = s_idx - t_idx
    kf = per_dir[0]['kern'][jnp.clip(tau_f, 0, n)] * (tau_f >= 0)[..., None, None, None]
    kb = per_dir[1]['kern'][jnp.clip(tau_b, 0, n)] * (tau_b >= 0)[..., None, None, None]
    m = (kf + kb).transpose(2, 0, 4, 1, 3).reshape(g, S5_ROW, S5_ROW)

    pow_f = n - 1 - jnp.arange(n)
    pow_b = jnp.arange(n)

    def ws_part(key, d, pows):
        return per_dir[d][key][pows].transpose(1, 0, 3, 2).reshape(g, S5_ROW, p)

    ws = jnp.concatenate([ws_part('pbr', 0, pow_f), ws_part('pbr', 1, pow_b),
                          ws_part('pbi', 0, pow_f), ws_part('pbi', 1, pow_b)], axis=-1)

    out_f = jnp.arange(n) + 1
    out_b = n - jnp.arange(n)

    def wo_part(key, d, pows):
        return per_dir[d][key][pows].transpose(1, 3, 0, 2).reshape(g, p, S5_ROW)

    zero = jnp.zeros((g, p, S5_ROW), F32)
    wof = jnp.concatenate([wo_part('cpr', 0, out_f), zero, -wo_part('cpi', 0, out_f), zero], axis=1)
    wob = jnp.concatenate([zero, wo_part('cpr', 1, out_b), zero, -wo_part('cpi', 1, out_b)], axis=1)
    ar = jnp.concatenate([per_dir[0]['pr'][n], per_dir[1]['pr'][n]], axis=-1)[:, None, :]
    ai = jnp.concatenate([per_dir[0]['pi'][n], per_dir[1]['pi'][n]], axis=-1)[:, None, :]
    return m.astype(BF16), ws.astype(BF16), wof.astype(BF16), wob.astype(BF16), ar, ai


def _s5_kernel(u_ref, m_ref, ws_ref, wof_ref, wob_ref, ar_ref, ai_ref, y_ref,
               s_ref, hf_ref, hb_ref, *, nb, nchunk):
    u = u_ref[...]
    s_ref[...] = jnp.dot(u, ws_ref[...], preferred_element_type=F32)
    ar = ar_ref[...]
    ai = ai_ref[...]
    is_fwd = lax.broadcasted_iota(jnp.int32, (nb, LANES), 1) < S5_STATE

    def step(i, carry):
        h_re, h_im = carry
        rf = pl.multiple_of(i * nb, 8)
        rb = pl.multiple_of((nchunk - 1 - i) * nb, 8)
        hf_ref[pl.ds(rf, nb), 0:LANES] = h_re
        hf_ref[pl.ds(rf, nb), LANES:2 * LANES] = h_im
        hb_ref[pl.ds(rb, nb), 0:LANES] = h_re
        hb_ref[pl.ds(rb, nb), LANES:2 * LANES] = h_im
        s_re = jnp.where(is_fwd, s_ref[pl.ds(rf, nb), 0:LANES], s_ref[pl.ds(rb, nb), 0:LANES])
        s_im = jnp.where(is_fwd, s_ref[pl.ds(rf, nb), LANES:2 * LANES],
                         s_ref[pl.ds(rb, nb), LANES:2 * LANES])
        return (ar * h_re - ai * h_im + s_re, ar * h_im + ai * h_re + s_im)

    zero = jnp.zeros((nb, LANES), F32)
    lax.fori_loop(0, nchunk, step, (zero, zero))
    y = jnp.dot(u, m_ref[...], preferred_element_type=F32)
    y = y + jnp.dot(hf_ref[...].astype(BF16), wof_ref[...], preferred_element_type=F32)
    y = y + jnp.dot(hb_ref[...].astype(BF16), wob_ref[...], preferred_element_type=F32)
    y_ref[...] = y.astype(BF16)


def _s5(u, s5w):
    m, ws, wof, wob, ar, ai = s5w
    bsz, seq, _ = u.shape
    nchunk = seq // S5_CHUNK
    rows = nchunk * bsz
    u2 = (u.reshape(bsz, nchunk, S5_CHUNK, S5_GROUPS, S5_GROUP_CH)
          .transpose(3, 1, 0, 2, 4).reshape(S5_GROUPS, rows, S5_ROW))
    grp = lambda r, c: pl.BlockSpec((None, r, c), lambda g: (g, 0, 0))
    y2 = pl.pallas_call(
        functools.partial(_s5_kernel, nb=bsz, nchunk=nchunk),
        grid=(S5_GROUPS,),
        in_specs=[grp(rows, S5_ROW), grp(S5_ROW, S5_ROW), grp(S5_ROW, S5_ROW),
                  grp(S5_ROW, S5_ROW), grp(S5_ROW, S5_ROW), grp(1, LANES), grp(1, LANES)],
        out_specs=grp(rows, S5_ROW),
        out_shape=jax.ShapeDtypeStruct((S5_GROUPS, rows, S5_ROW), BF16),
        scratch_shapes=[pltpu.VMEM((rows, S5_ROW), F32)] * 3,
        compiler_params=_cparams(("parallel",)),
        name="s5",
    )(u2, m, ws, wof, wob, ar, ai)
    return (y2.reshape(S5_GROUPS, nchunk, bsz, S5_CHUNK, S5_GROUP_CH)
            .transpose(2, 1, 3, 0, 4).reshape(bsz, seq, S5_WIDTH))


def _attn_kernel(lam_ref, q_ref, k_ref, v_ref, g_ref, o_ref, vx_ref):
    @pl.when(pl.program_id(2) == 0)
    def _():
        vx_ref[:, 0:LANES] = v_ref[...]
        vx_ref[:, LANES:2 * LANES] = jnp.ones(v_ref.shape, BF16)

    k = k_ref[...]
    vx = vx_ref[...]
    rows_per_chunk = q_ref.shape[0] // ATT_ROW_CHUNKS

    def scores(qm):
        return lax.dot_general(qm, k, (((1,), (1,)), ((), ())), preferred_element_type=F32)

    def probs(s):
        return jnp.exp2(s - jnp.max(s, axis=-1, keepdims=True)).astype(BF16)

    def weighted(e):
        ol = jnp.dot(e, vx, preferred_element_type=F32)
        return ol[:, 0:LANES] / ol[:, LANES:2 * LANES]

    chains = []
    for c in range(ATT_ROW_CHUNKS):
        rows = slice(c * rows_per_chunk, (c + 1) * rows_per_chunk)
        q = q_ref[rows, :]
        first = lax.broadcasted_iota(jnp.int32, q.shape, 1) < ATT_HEAD_DIM
        zero = jnp.zeros_like(q)
        chains.append((rows, jnp.where(first, q, zero)))
        chains.append((rows, jnp.where(first, zero, q)))

    s_val, e_val, o_val = {}, {}, {}
    for t in range(len(chains) + 2):
        if t < len(chains):
            s_val[t] = scores(chains[t][1])
        if 0 <= t - 1 < len(chains):
            e_val[t - 1] = probs(s_val.pop(t - 1))
        if 0 <= t - 2 < len(chains):
            o_val[t - 2] = weighted(e_val.pop(t - 2))
            if (t - 2) % 2 == 1:
                o = o_val.pop(t - 3) - lam_ref[0] * o_val.pop(t - 2)
                o = o * lax.rsqrt(jnp.mean(o * o, axis=-1, keepdims=True) + LN_EPS) * g_ref[...]
                o_ref[chains[t - 2][0], :] = o.astype(BF16)


def _attention(lam, q, k, v, g_row):
    bsz, seq, _ = q.shape
    tq = min(ATT_TILE, seq)
    return pl.pallas_call(
        _attn_kernel,
        grid=(bsz, ATT_HEADS, seq // tq),
        in_specs=[pl.BlockSpec(memory_space=pltpu.SMEM),
                  pl.BlockSpec((None, tq, LANES), lambda b, h, i: (b, i, h)),
                  pl.BlockSpec((None, seq, LANES), lambda b, h, i: (b, 0, h)),
                  pl.BlockSpec((None, seq, LANES), lambda b, h, i: (b, 0, h)),
                  pl.BlockSpec((1, LANES), lambda b, h, i: (0, 0))],
        out_specs=pl.BlockSpec((None, tq, LANES), lambda b, h, i: (b, i, h)),
        out_shape=jax.ShapeDtypeStruct((bsz, seq, ATT_WIDTH), BF16),
        scratch_shapes=[pltpu.VMEM((seq, 2 * LANES), BF16)],
        compiler_params=_cparams(("parallel", "parallel", "arbitrary")),
        name="attn",
    )(lam, q, k, v, g_row)


def _merge_kernel(x_ref, u_ref, y_ref, o_ref, ga_ref, gb_ref, mod_ref, d_ref, wglu_ref, ws5_ref,
                  watt_ref, wo_ref, ln1g_ref, ln1b_ref, wq_ref, x1_ref, h2_ref, pq_ref):
    mod = mod_ref[...]
    rc = x_ref.shape[0] // MERGE_ROW_CHUNKS
    rows = [slice(c * rc, (c + 1) * rc) for c in range(MERGE_ROW_CHUNKS)]
    dot = lambda a, b: jnp.dot(a, b, preferred_element_type=F32)
    s = [_gelu_tanh(u_ref[r, :].astype(F32) * d_ref[...] + y_ref[r, :].astype(F32)) for r in rows]
    glu = [dot(v.astype(BF16), wglu_ref[...]) for v in s]
    s = [a * _sigmoid(b) for a, b in zip(s, glu)]
    br_b = [dot(o_ref[r, :], watt_ref[...]) for r in rows]
    br_a = [dot(v.astype(BF16), ws5_ref[...]) for v in s]
    merged = [ga_ref[r, :].astype(F32) * a + gb_ref[r, :].astype(F32) * b
              for r, a, b in zip(rows, br_a, br_b)]
    z = [dot(v.astype(BF16), wo_ref[...]) for v in merged]
    h2s = []
    for r, zc in zip(rows, z):
        x1 = _ln(DN_ALPHA * x_ref[r, :] + mod[2:3] * zc) * ln1g_ref[...] + ln1b_ref[...]
        x1_ref[r, :] = x1
        h2 = (_ln(x1) * (1.0 + mod[4:5]) + mod[3:4]).astype(BF16)
        h2_ref[r, :] = h2
        h2s.append(h2)
    for r, h2 in zip(rows, h2s):
        pq_ref[r, :] = dot(h2, wq_ref[...]).astype(BF16)


def _merge(x, u, y, o, ga, gb, mod3, wts):
    bsz, seq, _ = x.shape
    tm = min(MERGE_TILE, seq)
    tok = lambda n: pl.BlockSpec((None, tm, n), lambda b, i: (b, i, 0))
    row = lambda n: _const_spec((1, n))
    return pl.pallas_call(
        _merge_kernel,
        grid=(bsz, seq // tm),
        in_specs=[tok(D_MODEL), tok(S5_WIDTH), tok(S5_WIDTH), tok(ATT_WIDTH), tok(D_MODEL), tok(D_MODEL),
                  pl.BlockSpec((None, 6, D_MODEL), lambda b, i: (b, 0, 0)),
                  row(S5_WIDTH), _const_spec((S5_WIDTH, S5_WIDTH)), _const_spec((S5_WIDTH, D_MODEL)),
                  _const_spec((ATT_WIDTH, D_MODEL)), _const_spec((D_MODEL, D_MODEL)),
                  row(D_MODEL), row(D_MODEL), _const_spec((D_MODEL, PEER_NQ))],
        out_specs=[tok(D_MODEL), tok(D_MODEL), tok(PEER_NQ)],
        out_shape=[jax.ShapeDtypeStruct((bsz, seq, D_MODEL), F32),
                   jax.ShapeDtypeStruct((bsz, seq, D_MODEL), BF16),
                   jax.ShapeDtypeStruct((bsz, seq, PEER_NQ), BF16)],
        compiler_params=_cparams(("parallel", "parallel")),
        name="merge",
    )(x, u, y, o, ga, gb, mod3, wts['s5_d'], wts['w_glu'], wts['w_s5_out'], wts['w_attn_out'],
      wts['w_o'], wts['ln1_g'], wts['ln1_b'], wts['w_q'])


def _route_kernel(pq_ref, keys_ref, w_ref, a_sc, b_sc, wt_sc, ix_sc, wr_sc, ir_sc, dp_sc, e2_sc):
    tn = pq_ref.shape[0]
    assert PEER_TOPK == 16 and SUBLANES == 8
    row8 = lax.broadcasted_iota(jnp.int32, (SUBLANES, tn), 0)
    row8_f = row8.astype(F32)
    nt_dims = (((1,), (1,)), ((), ()))
    nv = PEER_NKEYS // SUBLANES

    def sorted_slabs(st, with_index):
        vals = [st[SUBLANES * v:SUBLANES * (v + 1), :] for v in range(nv)]
        idxs = [row8_f + float(SUBLANES * v) for v in range(nv)] if with_index else None
        for lo, hi in _SLAB_SORT_PAIRS:
            a, b = vals[lo], vals[hi]
            vals[lo], vals[hi] = jnp.maximum(a, b), jnp.minimum(a, b)
            if with_index:
                keep = a >= b
                ia, ib = idxs[lo], idxs[hi]
                idxs[lo], idxs[hi] = jnp.where(keep, ia, ib), jnp.where(keep, ib, ia)
        return vals, idxs

    def pop_heads(vals, idxs, pick, k):
        for d in range(min(len(vals) - 1, PEER_TOPK - 1 - k)):
            vals[d] = jnp.where(pick, vals[d + 1], vals[d])
            if idxs is not None:
                idxs[d] = jnp.where(pick, idxs[d + 1], idxs[d])

    for h in range(PEER_HEADS):
        q1 = pq_ref[:, (2 * h) * PEER_HALF:(2 * h + 1) * PEER_HALF]
        q2 = pq_ref[:, (2 * h + 1) * PEER_HALF:(2 * h + 2) * PEER_HALF]
        st1 = lax.dot_general(keys_ref[2 * h], q1, nt_dims, preferred_element_type=F32)
        st2 = lax.dot_general(keys_ref[2 * h + 1], q2, nt_dims, preferred_element_type=F32)
        s2 = st2.T
        e2_sc[h] = jnp.exp(s2 - jnp.max(s2, axis=-1, keepdims=True))

        vals, idxs = sorted_slabs(st1, True)
        a_vals = []
        for k in range(PEER_TOPK):
            m = jnp.max(vals[0], axis=0, keepdims=True)
            ix = jnp.min(jnp.where(vals[0] == m, idxs[0], float(PEER_NKEYS)), axis=0, keepdims=True)
            pop_heads(vals, idxs, idxs[0] == ix, k)
            a_vals.append(m)
            a_sc[k:k + 1, :] = m
            ix_sc[h * PEER_TOPK + k:h * PEER_TOPK + k + 1, :] = ix
        vals, _ = sorted_slabs(st2, False)
        for k in range(PEER_TOPK):
            m = jnp.max(vals[0], axis=0, keepdims=True)
            pop_heads(vals, None, vals[0] == m, k)
            b_sc[k:k + 1, :] = m
        b_lo = b_sc[0:8, :]
        b_hi = b_sc[8:16, :]
        b0 = b_sc[0:1, :]

        shift = lambda x, n: pltpu.roll(x, n, 0)
        cands = [
            a_vals[0] + b_lo,
            a_vals[0] + b_hi,
            a_vals[1] + b_lo,
            jnp.where(row8 < 5, a_vals[2] + b_lo, a_vals[4] + shift(b_lo, 5)),
            jnp.where(row8 < 4, a_vals[3] + b_lo,
                      jnp.where(row8 < 6, a_vals[5] + shift(b_lo, 4), a_vals[6] + shift(b_lo, 6))),
            jnp.where(row8 < 2, a_vals[7] + b_lo, a_sc[6:14, :] + b0),
            jnp.where(row8 < 2, shift(a_sc[8:16, :], 2) + b0, NEG_BIG),
        ]
        m0 = a_vals[0] + b0
        z = jnp.zeros_like(m0)
        tau = m0
        cands.append(jnp.full_like(cands[0], NEG_BIG))
        for lo, hi in _CAND_SORT_PAIRS:
            a, b = cands[lo], cands[hi]
            cands[lo], cands[hi] = jnp.maximum(a, b), jnp.minimum(a, b)
        for k in range(PEER_TOPK):
            tau = jnp.max(cands[0], axis=0, keepdims=True)
            z = z + jnp.exp(tau - m0)
            pop_heads(cands, None, cands[0] == tau, k)
        inv_z = 1.0 / z
        for k in range(PEER_TOPK):
            j = h * PEER_TOPK + k
            wt_sc[j:j + 1, :] = jnp.exp(a_vals[k] - a_vals[0]) * inv_z

        depth = jnp.zeros_like(st2)
        for k in range(DEPTH_DIRECT):
            depth = depth + jnp.where(a_vals[k] + st2 >= tau, 1.0, 0.0)
        for l in range(PEER_TOPK // (DEPTH_DIRECT + 1)):
            b_l = b_sc[l:l + 1, :]
            extra = jnp.zeros_like(b_l)
            for k in range(DEPTH_DIRECT, PEER_TOPK // (l + 1)):
                extra = extra + jnp.where(a_vals[k] + b_l >= tau, 1.0, 0.0)
            depth = depth + jnp.where(st2 == b_l, extra, 0.0)
        dp_sc[h] = depth.T

    wr_sc[...] = wt_sc[...].T
    ir_sc[...] = ix_sc[...].T
    rank_j = (lax.broadcasted_iota(jnp.int32, (PEER_NKEYS, PEER_NKEYS), 0) % PEER_TOPK).astype(F32).astype(BF16)
    sub_j = lax.broadcasted_iota(jnp.int32, (PEER_NKEYS, PEER_NKEYS), 0).astype(F32).astype(BF16)
    zero_b = jnp.zeros((PEER_NKEYS, PEER_NKEYS), BF16)

    def token(t):
        def head_rows(ref):
            return jnp.concatenate(
                [jnp.broadcast_to(ref[h, pl.ds(t, 1), :], (PEER_TOPK, PEER_NKEYS)).astype(BF16)
                 for h in range(PEER_HEADS)], axis=0)

        def all_rows(ref):
            return jnp.broadcast_to(ref[pl.ds(t, 1), :], (PEER_NKEYS, PEER_NKEYS)).astype(BF16)

        r = jnp.where(rank_j < head_rows(dp_sc), head_rows(e2_sc), zero_b)
        pt = jnp.where(sub_j == all_rows(ir_sc), all_rows(wr_sc), zero_b)
        return jnp.dot(pt, r, preferred_element_type=F32)

    def token_group(g, carry):
        for pair in range(ROUTE_PAIRS_PER_TRIP):
            t0 = pl.multiple_of((g * ROUTE_PAIRS_PER_TRIP + pair) * 16, 16)
            halves = []
            for half in range(2):
                tiles = jnp.stack([token(t0 + 8 * half + u) for u in range(8)])
                halves.append(jnp.swapaxes(tiles, 0, 1))
            w_ref[:, pl.ds(t0, 16), :] = jnp.concatenate(halves, axis=1).astype(BF16)
        return carry

    lax.fori_loop(0, tn // (16 * ROUTE_PAIRS_PER_TRIP), token_group, 0)


def _route(pq, keys_b):
    ntok = pq.shape[0]
    tn = ROUTE_TILE
    nj = PEER_HEADS * PEER_TOPK
    return pl.pallas_call(
        _route_kernel,
        grid=(ntok // tn,),
        in_specs=[pl.BlockSpec((tn, PEER_NQ), lambda i: (i, 0)),
                  _const_spec((2 * PEER_HEADS, PEER_NKEYS, PEER_HALF))],
        out_specs=pl.BlockSpec((None, PEER_NKEYS, tn, PEER_NKEYS), lambda i: (i, 0, 0, 0)),
        out_shape=jax.ShapeDtypeStruct((ntok // tn, PEER_NKEYS, tn, PEER_NKEYS), BF16),
        scratch_shapes=[pltpu.VMEM((PEER_TOPK, tn), F32), pltpu.VMEM((PEER_TOPK, tn), F32),
                        pltpu.VMEM((nj, tn), F32), pltpu.VMEM((nj, tn), F32),
                        pltpu.VMEM((tn, nj), F32), pltpu.VMEM((tn, nj), F32),
                        pltpu.VMEM((PEER_HEADS, tn, PEER_NKEYS), F32),
                        pltpu.VMEM((PEER_HEADS, tn, PEER_NKEYS), F32)],
        compiler_params=_cparams(("parallel",)),
        name="route",
    )(pq, keys_b)


def _expert_kernel(h_ref, ut_ref, v_ref, w_ref, x1_ref, mod_ref, g_ref, b_ref, o_ref, acc_ref, gate_sc):
    e = pl.program_id(2)

    @pl.when(e == 0)
    def _():
        acc_ref[...] = jnp.zeros_like(acc_ref)

    act = jnp.dot(h_ref[...], ut_ref[...], preferred_element_type=F32)
    rt = w_ref.shape[2]
    for j in range(w_ref.shape[0]):
        for k in range(w_ref.shape[1]):
            a = act[j * rt:(j + 1) * rt, k * PEER_NKEYS:(k + 1) * PEER_NKEYS].astype(BF16)
            gate_sc[j * rt:(j + 1) * rt, k * PEER_NKEYS:(k + 1) * PEER_NKEYS] = (
                _gelu_tanh(a) * w_ref[j, k])
    acc_ref[...] += jnp.dot(gate_sc[...], v_ref[...], preferred_element_type=F32)

    @pl.when(e == pl.num_programs(2) - 1)
    def _():
        mod = mod_ref[...]
        o_ref[...] = _ln(DN_ALPHA * x1_ref[...] + mod[5:6] * acc_ref[...]) * g_ref[...] + b_ref[...]


def _experts(h2, w4, x1, mod3, ut_b, v_b, ln2_g, ln2_b):
    bsz, seq, _ = x1.shape
    tt = min(EXPERT_TOK_TILE, seq)
    ce = EXPERT_CHUNK
    rt = w4.shape[2]
    nblk = seq // tt
    tok = lambda n: pl.BlockSpec((None, tt, n), lambda b, i, e: (b, i, 0))
    return pl.pallas_call(
        _expert_kernel,
        grid=(bsz, nblk, PEER_EXPERTS // ce),
        in_specs=[tok(D_MODEL),
                  pl.BlockSpec((None, D_MODEL, ce), lambda b, i, e: (e, 0, 0)),
                  pl.BlockSpec((ce, D_MODEL), lambda b, i, e: (e, 0)),
                  pl.BlockSpec((tt // rt, ce // PEER_NKEYS, rt, PEER_NKEYS),
                               lambda b, i, e: (b * nblk + i, e, 0, 0)),
                  tok(D_MODEL),
                  pl.BlockSpec((None, 6, D_MODEL), lambda b, i, e: (b, 0, 0)),
                  pl.BlockSpec((1, D_MODEL), lambda b, i, e: (0, 0)),
                  pl.BlockSpec((1, D_MODEL), lambda b, i, e: (0, 0))],
        out_specs=tok(D_MODEL),
        out_shape=jax.ShapeDtypeStruct((bsz, seq, D_MODEL), F32),
        scratch_shapes=[pltpu.VMEM((tt, D_MODEL), F32), pltpu.VMEM((tt, ce), BF16)],
        compiler_params=_cparams(("parallel", "parallel", "arbitrary")),
        name="expert",
    )(h2, ut_b, v_b, w4, x1, mod3, ln2_g, ln2_b)


def _rope_tables(seq):
    half = ATT_HEAD_DIM // 2
    inv_freq = ROPE_THETA ** (-jnp.arange(half, dtype=F32) * 2.0 / ATT_HEAD_DIM)
    ang = jnp.arange(seq, dtype=F32)[:, None] * inv_freq[None, :]
    cos = jnp.cos(ang)
    sin = jnp.sin(ang)
    zero = jnp.zeros_like(sin)
    reps = LANES // ATT_HEAD_DIM
    cos_t = jnp.tile(jnp.concatenate([cos, cos], axis=1), (1, reps))
    sa_t = jnp.tile(jnp.concatenate([-sin, zero], axis=1), (1, reps))
    sb_t = jnp.tile(jnp.concatenate([zero, sin], axis=1), (1, reps))
    return cos_t, sa_t, sb_t


def _prepare(layer_idx, p):
    lam_init = 0.8 - 0.6 * math.exp(-0.3 * layer_idx)
    lam = (jnp.exp(jnp.sum(p['attn_lambda_q1'].astype(F32) * p['attn_lambda_k1'].astype(F32)))
           - jnp.exp(jnp.sum(p['attn_lambda_q2'].astype(F32) * p['attn_lambda_k2'].astype(F32))) + lam_init)
    row = lambda a: a.astype(F32).reshape(1, -1)
    return dict(
        w_ada=p['w_ada'].astype(F32), b_ada=p['b_ada'].astype(F32),
        w_in=p['w_in'].astype(BF16),
        s5w=_s5_weights(p['s5_lambda_re'], p['s5_lambda_im'], p['s5_log_dt'], p['s5_b_re'],
                        p['s5_b_im'], p['s5_c_re'], p['s5_c_im']),
        s5_d=row(p['s5_d']), w_glu=p['s5_w_glu'].astype(BF16), w_s5_out=p['w_s5_out'].astype(BF16),
        lam=lam.reshape(1).astype(F32),
        subln=row(p['attn_subln_g']) * (1.0 - lam_init),
        w_attn_out=p['w_attn_out'].astype(BF16), w_o=p['w_o'].astype(BF16),
        ln1_g=row(p['ln1_g']), ln1_b=row(p['ln1_b']),
        w_q=p['peer_w_q'].astype(BF16),
        keys=p['peer_keys'].astype(BF16).reshape(2 * PEER_HEADS, PEER_NKEYS, PEER_HALF),
        ut=(p['peer_u'].astype(BF16).reshape(PEER_EXPERTS // EXPERT_CHUNK, EXPERT_CHUNK, D_MODEL)
            .transpose(0, 2, 1)),
        v=p['peer_v'].astype(BF16),
        ln2_g=row(p['ln2_g']), ln2_b=row(p['ln2_b']),
    )


def _encoder_layer(x, c, wts, rope):
    bsz, seq, _ = x.shape
    mod3 = _ada(c.astype(F32), wts['w_ada'], wts['b_ada']).reshape(bsz, 6, D_MODEL)
    u, q, k, v, ga, gb = _inproj(x.astype(F32), mod3, wts['w_in'], *rope)
    y = _s5(u, wts['s5w'])
    o = _attention(wts['lam'], q, k, v, wts['subln'])
    x1, h2, pq = _merge(x.astype(F32), u, y, o, ga, gb, mod3, wts)
    w4 = _route(pq.reshape(bsz * seq, PEER_NQ), wts['keys'])
    x2 = _experts(h2, w4, x1, mod3, wts['ut'], wts['v'], wts['ln2_g'], wts['ln2_b'])
    return x2.astype(x.dtype)


def kernel(x_prompt, x_sample, c_prompt, c_sample, w_ada, b_ada, w_in, s5_lambda_re, s5_lambda_im, s5_log_dt, s5_b_re, s5_b_im, s5_c_re, s5_c_im, s5_d, s5_w_glu, w_s5_out, attn_lambda_q1, attn_lambda_k1, attn_lambda_q2, attn_lambda_k2, attn_subln_g, w_attn_out, w_o, ln1_g, ln1_b, peer_w_q, peer_keys, peer_u, peer_v, ln2_g, ln2_b):
    params = dict(w_ada=w_ada, b_ada=b_ada, w_in=w_in, s5_lambda_re=s5_lambda_re,
                  s5_lambda_im=s5_lambda_im, s5_log_dt=s5_log_dt, s5_b_re=s5_b_re, s5_b_im=s5_b_im,
                  s5_c_re=s5_c_re, s5_c_im=s5_c_im, s5_d=s5_d, s5_w_glu=s5_w_glu, w_s5_out=w_s5_out,
                  attn_lambda_q1=attn_lambda_q1, attn_lambda_k1=attn_lambda_k1,
                  attn_lambda_q2=attn_lambda_q2, attn_lambda_k2=attn_lambda_k2,
                  attn_subln_g=attn_subln_g, w_attn_out=w_attn_out, w_o=w_o, ln1_g=ln1_g, ln1_b=ln1_b,
                  peer_w_q=peer_w_q, peer_keys=peer_keys, peer_u=peer_u, peer_v=peer_v,
                  ln2_g=ln2_g, ln2_b=ln2_b)
    y_prompt, y_sample = x_prompt, x_sample
    rope_p = _rope_tables(x_prompt.shape[1])
    rope_s = _rope_tables(x_sample.shape[1])
    for l in range(DEPTH):
        wts = _prepare(l, {name: a[l] for name, a in params.items()})
        y_prompt = _encoder_layer(y_prompt, c_prompt, wts, rope_p)
        y_sample = _encoder_layer(y_sample, c_sample, wts, rope_s)
    return (y_prompt, y_sample)
```

```python
import functools
import math

import jax
import jax.numpy as jnp
from jax import lax
from jax.experimental import pallas as pl
from jax.experimental.pallas import tpu as pltpu

F32 = jnp.float32
BF16 = jnp.bfloat16
HIGHEST = lax.Precision.HIGHEST

D_MODEL = 1024
DEPTH = 1
S5_WIDTH = D_MODEL // 2
S5_GROUP_CH = 16
S5_GROUPS = S5_WIDTH // S5_GROUP_CH
S5_STATE = 64
S5_CHUNK = 16
S5_ROW = S5_CHUNK * S5_GROUP_CH
ATT_HEADS = 8
ATT_HEAD_DIM = D_MODEL // (2 * ATT_HEADS)
ATT_WIDTH = ATT_HEADS * 2 * ATT_HEAD_DIM
ROPE_THETA = 10000.0
N_IN = S5_WIDTH + 3 * ATT_WIDTH + 2 * D_MODEL
PEER_HEADS = 8
PEER_NKEYS = 128
PEER_EXPERTS = PEER_NKEYS * PEER_NKEYS
PEER_TOPK = 16
PEER_QDIM = 256
PEER_HALF = PEER_QDIM // 2
PEER_NQ = PEER_HEADS * PEER_QDIM
DN_ALPHA = (2 * DEPTH) ** 0.25
LN_EPS = 1e-5

LANES = 128
SUBLANES = 8
VMEM_LIMIT_BYTES = 56 * 1024 * 1024
NEG_BIG = -3.0e38

IN_TILE = 512
IN_ROW_CHUNKS = 2
ATT_TILE = 2048
ATT_ROW_CHUNKS = 4
MERGE_TILE = 512
MERGE_ROW_CHUNKS = 2
ROUTE_TILE = 128
DEPTH_DIRECT = 4
ROUTE_PAIRS_PER_TRIP = 2
EXPERT_TOK_TILE = 1024
EXPERT_CHUNK = 1024


def _merge_exchange_pairs(n):
    pairs = []
    p = 1
    while p < n:
        k = p
        while k >= 1:
            for j in range(k % p, n - k, 2 * k):
                for i in range(min(k, n - j - k)):
                    if (i + j) // (2 * p) == (i + j + k) // (2 * p):
                        pairs.append((i + j, i + j + k))
            k //= 2
        p *= 2
    return pairs


_SLAB_SORT_PAIRS = _merge_exchange_pairs(PEER_NKEYS // SUBLANES)
_CAND_SORT_PAIRS = _merge_exchange_pairs(8)


def _cparams(sem):
    return pltpu.CompilerParams(dimension_semantics=sem, vmem_limit_bytes=VMEM_LIMIT_BYTES)


def _const_spec(shape):
    nd = len(shape)
    return pl.BlockSpec(shape, lambda *_: (0,) * nd, pipeline_mode=pl.Buffered(1))


def _ln(x):
    mu = jnp.mean(x, axis=-1, keepdims=True)
    xc = x - mu
    var = jnp.mean(xc * xc, axis=-1, keepdims=True)
    return xc * lax.rsqrt(var + LN_EPS)


def _gelu_tanh(x):
    return 0.5 * x * (1.0 + jnp.tanh(math.sqrt(2.0 / math.pi) * (x + 0.044715 * (x * x * x))))


def _sigmoid(x):
    return 1.0 / (1.0 + jnp.exp(-x))


def _ada_kernel(c_ref, w_ref, b_ref, o_ref):
    c = c_ref[...]
    s = c * _sigmoid(c)
    o_ref[...] = jnp.dot(s, w_ref[...], precision=HIGHEST, preferred_element_type=F32) + b_ref[...]


def _ada(c, w_ada, b_ada):
    bsz = c.shape[0]
    nblk = w_ada.shape[1] // D_MODEL
    return pl.pallas_call(
        _ada_kernel,
        grid=(nblk,),
        in_specs=[
            pl.BlockSpec((bsz, D_MODEL), lambda j: (0, 0)),
            pl.BlockSpec((D_MODEL, D_MODEL), lambda j: (0, j)),
            pl.BlockSpec((1, D_MODEL), lambda j: (0, j)),
        ],
        out_specs=pl.BlockSpec((bsz, D_MODEL), lambda j: (0, j)),
        out_shape=jax.ShapeDtypeStruct((bsz, nblk * D_MODEL), F32),
        compiler_params=_cparams(("arbitrary",)),
        name="ada",
    )(c, w_ada, b_ada.reshape(1, -1))


def _in_kernel(x_ref, mod_ref, w_ref, cos_ref, sa_ref, sb_ref,
               u_ref, q_ref, k_ref, v_ref, ga_ref, gb_ref):
    mod = mod_ref[...]
    rc = x_ref.shape[0] // IN_ROW_CHUNKS
    rows = [slice(c * rc, (c + 1) * rc) for c in range(IN_ROW_CHUNKS)]
    hbs = [(_ln(x_ref[r, :]) * (1.0 + mod[1:2]) + mod[0:1]).astype(BF16) for r in rows]

    def proj(c, a, b):
        return jnp.dot(hbs[c], w_ref[:, a:b], preferred_element_type=F32)

    o1 = S5_WIDTH
    o2 = o1 + ATT_WIDTH
    o3 = o2 + ATT_WIDTH
    o4 = o3 + ATT_WIDTH
    o5 = o4 + D_MODEL
    half = ATT_HEAD_DIM // 2
    q_scale = ATT_HEAD_DIM ** -0.5 * math.log2(math.e)

    def rope_store(t, o_ref, r, scale):
        cos, sa, sb = cos_ref[r, :] * scale, sa_ref[r, :] * scale, sb_ref[r, :] * scale
        for j in range(ATT_WIDTH // LANES):
            tj = t[:, j * LANES:(j + 1) * LANES]
            rot = tj * cos + pltpu.roll(tj, LANES - half, 1) * sa + pltpu.roll(tj, half, 1) * sb
            o_ref[r, j * LANES:(j + 1) * LANES] = rot.astype(BF16)

    for c, r in enumerate(rows):
        u_ref[r, :] = proj(c, 0, o1).astype(BF16)
    for c, r in enumerate(rows):
        rope_store(proj(c, o1, o2), q_ref, r, q_scale)
    for c, r in enumerate(rows):
        rope_store(proj(c, o2, o3), k_ref, r, 1.0)
    for c, r in enumerate(rows):
        v_ref[r, :] = proj(c, o3, o4).astype(BF16)
    for c, r in enumerate(rows):
        ga_ref[r, :] = _sigmoid(proj(c, o4, o5)).astype(BF16)
    for c, r in enumerate(rows):
        gb_ref[r, :] = _sigmoid(proj(c, o5, N_IN)).astype(BF16)


def _inproj(x, mod3, w_in_b, cos, sa, sb):
    bsz, seq, _ = x.shape
    tm = min(IN_TILE, seq)
    tok = lambda n: pl.BlockSpec((None, tm, n), lambda b, i: (b, i, 0))
    rope = pl.BlockSpec((tm, LANES), lambda b, i: (i, 0))
    outs = [(S5_WIDTH, BF16), (ATT_WIDTH, BF16), (ATT_WIDTH, BF16), (ATT_WIDTH, BF16),
            (D_MODEL, BF16), (D_MODEL, BF16)]
    return pl.pallas_call(
        _in_kernel,
        grid=(bsz, seq // tm),
        in_specs=[tok(D_MODEL),
                  pl.BlockSpec((None, 6, D_MODEL), lambda b, i: (b, 0, 0)),
                  _const_spec((D_MODEL, N_IN)), rope, rope, rope],
        out_specs=[tok(n) for n, _ in outs],
        out_shape=[jax.ShapeDtypeStruct((bsz, seq, n), dt) for n, dt in outs],
        compiler_params=_cparams(("parallel", "parallel")),
        name="inproj",
    )(x, mod3, w_in_b, cos, sa, sb)


def _s5_weights(lam_re, lam_im, log_dt, b_re, b_im, c_re, c_im):
    n = S5_CHUNK
    per_dir = []
    for d in range(2):
        lr = lam_re[d].astype(F32)
        li = lam_im[d].astype(F32)
        dt = jnp.exp(log_dt[d].astype(F32))[:, None]
        mag = jnp.exp(lr * dt)
        lbr = mag * jnp.cos(li * dt)
        lbi = mag * jnp.sin(li * dt)
        den = lr * lr + li * li
        nr = lbr - 1.0
        cr = (nr * lr + lbi * li) / den
        ci = (lbi * lr - nr * li) / den
        br = b_re[d].astype(F32)
        bi = b_im[d].astype(F32)
        bbr = cr[..., None] * br - ci[..., None] * bi
        bbi = cr[..., None] * bi + ci[..., None] * br
        pr = [jnp.ones_like(lbr)]
        pi = [jnp.zeros_like(lbi)]
        for _ in range(n):
            pr.append(pr[-1] * lbr - pi[-1] * lbi)
            pi.append(pr[-2] * lbi + pi[-1] * lbr)
        pr = jnp.stack(pr)
        pi = jnp.stack(pi)
        cre = c_re[d].astype(F32)
        cim = c_im[d].astype(F32)
        cpr = cre[None] * pr[:, :, None, :] - cim[None] * pi[:, :, None, :]
        cpi = cre[None] * pi[:, :, None, :] + cim[None] * pr[:, :, None, :]
        kern = (jnp.einsum('ngcp,gpd->ngcd', cpr, bbr, precision=HIGHEST)
                - jnp.einsum('ngcp,gpd->ngcd', cpi, bbi, precision=HIGHEST))
        pbr = pr[..., None] * bbr[None] - pi[..., None] * bbi[None]
        pbi = pr[..., None] * bbi[None] + pi[..., None] * bbr[None]
        per_dir.append(dict(pr=pr, pi=pi, cpr=cpr, cpi=cpi, kern=kern, pbr=pbr, pbi=pbi))

    g, c, p = S5_GROUPS, S5_GROUP_CH, S5_STATE
    s_idx = jnp.arange(n)[:, None]
    t_idx = jnp.arange(n)[None, :]
    tau_f = t_idx - s_idx
    tau_b = s_idx - t_idx
    kf = per_dir[0]['kern'][jnp.clip(tau_f, 0, n)] * (tau_f >= 0)[..., None, None, None]
    kb = per_dir[1]['kern'][jnp.clip(tau_b, 0, n)] * (tau_b >= 0)[..., None, None, None]
    m = (kf + kb).transpose(2, 0, 4, 1, 3).reshape(g, S5_ROW, S5_ROW)

    pow_f = n - 1 - jnp.arange(n)
    pow_b = jnp.arange(n)

    def ws_part(key, d, pows):
        return per_dir[d][key][pows].transpose(1, 0, 3, 2).reshape(g, S5_ROW, p)

    ws = jnp.concatenate([ws_part('pbr', 0, pow_f), ws_part('pbr', 1, pow_b),
                          ws_part('pbi', 0, pow_f), ws_part('pbi', 1, pow_b)], axis=-1)

    out_f = jnp.arange(n) + 1
    out_b = n - jnp.arange(n)

    def wo_part(key, d, pows):
        return per_dir[d][key][pows].transpose(1, 3, 0, 2).reshape(g, p, S5_ROW)

    zero = jnp.zeros((g, p, S5_ROW), F32)
    wof = jnp.concatenate([wo_part('cpr', 0, out_f), zero, -wo_part('cpi', 0, out_f), zero], axis=1)
    wob = jnp.concatenate([zero, wo_part('cpr', 1, out_b), zero, -wo_part('cpi', 1, out_b)], axis=1)
    ar = jnp.concatenate([per_dir[0]['pr'][n], per_dir[1]['pr'][n]], axis=-1)[:, None, :]
    ai = jnp.concatenate([per_dir[0]['pi'][n], per_dir[1]['pi'][n]], axis=-1)[:, None, :]
    return m.astype(BF16), ws.astype(BF16), wof.astype(BF16), wob.astype(BF16), ar, ai


def _s5_kernel(u_ref, m_ref, ws_ref, wof_ref, wob_ref, ar_ref, ai_ref, y_ref,
               s_ref, hf_ref, hb_ref, *, nb, nchunk):
    u = u_ref[...].T
    s_ref[...] = jnp.dot(u, ws_ref[...], preferred_element_type=F32)
    ar = ar_ref[...]
    ai = ai_ref[...]
    is_fwd = lax.broadcasted_iota(jnp.int32, (nb, LANES), 1) < S5_STATE

    def step(i, carry):
        h_re, h_im = carry
        rf = pl.multiple_of(i * nb, 8)
        rb = pl.multiple_of((nchunk - 1 - i) * nb, 8)
        hf_ref[pl.ds(rf, nb), 0:LANES] = h_re
        hf_ref[pl.ds(rf, nb), LANES:2 * LANES] = h_im
        hb_ref[pl.ds(rb, nb), 0:LANES] = h_re
        hb_ref[pl.ds(rb, nb), LANES:2 * LANES] = h_im
        s_re = jnp.where(is_fwd, s_ref[pl.ds(rf, nb), 0:LANES], s_ref[pl.ds(rb, nb), 0:LANES])
        s_im = jnp.where(is_fwd, s_ref[pl.ds(rf, nb), LANES:2 * LANES],
                         s_ref[pl.ds(rb, nb), LANES:2 * LANES])
        return (ar * h_re - ai * h_im + s_re, ar * h_im + ai * h_re + s_im)

    zero = jnp.zeros((nb, LANES), F32)
    lax.fori_loop(0, nchunk, step, (zero, zero))
    y = jnp.dot(u, m_ref[...], preferred_element_type=F32)
    y = y + jnp.dot(hf_ref[...].astype(BF16), wof_ref[...], preferred_element_type=F32)
    y = y + jnp.dot(hb_ref[...].astype(BF16), wob_ref[...], preferred_element_type=F32)
    y_ref[...] = y.T.astype(BF16)


def _s5(u, s5w):
    m, ws, wof, wob, ar, ai = s5w
    bsz, seq, _ = u.shape
    nchunk = seq // S5_CHUNK
    rows = nchunk * bsz
    u2 = (u.reshape(bsz, nchunk, S5_CHUNK, S5_GROUPS, S5_GROUP_CH)
          .transpose(3, 2, 4, 1, 0).reshape(S5_GROUPS, S5_ROW, rows))
    grp = lambda r, c: pl.BlockSpec((None, r, c), lambda g: (g, 0, 0))
    y2 = pl.pallas_call(
        functools.partial(_s5_kernel, nb=bsz, nchunk=nchunk),
        grid=(S5_GROUPS,),
        in_specs=[grp(S5_ROW, rows), grp(S5_ROW, S5_ROW), grp(S5_ROW, S5_ROW),
                  grp(S5_ROW, S5_ROW), grp(S5_ROW, S5_ROW), grp(1, LANES), grp(1, LANES)],
        out_specs=grp(S5_ROW, rows),
        out_shape=jax.ShapeDtypeStruct((S5_GROUPS, S5_ROW, rows), BF16),
        scratch_shapes=[pltpu.VMEM((rows, S5_ROW), F32)] * 3,
        compiler_params=_cparams(("parallel",)),
        name="s5",
    )(u2, m, ws, wof, wob, ar, ai)
    return (y2.reshape(S5_GROUPS, S5_CHUNK, S5_GROUP_CH, nchunk, bsz)
            .transpose(4, 3, 1, 0, 2).reshape(bsz, seq, S5_WIDTH))


def _attn_kernel(lam_ref, q_ref, k_ref, v_ref, g_ref, o_ref, vx_ref):
    @pl.when(pl.program_id(2) == 0)
    def _():
        vx_ref[:, 0:LANES] = v_ref[...]
        vx_ref[:, LANES:2 * LANES] = jnp.ones(v_ref.shape, BF16)

    k = k_ref[...]
    vx = vx_ref[...]
    rows_per_chunk = q_ref.shape[0] // ATT_ROW_CHUNKS

    def scores(qm):
        return lax.dot_general(qm, k, (((1,), (1,)), ((), ())), preferred_element_type=F32)

    def probs(s):
        return jnp.exp2(s - jnp.max(s, axis=-1, keepdims=True)).astype(BF16)

    def weighted(e):
        ol = jnp.dot(e, vx, preferred_element_type=F32)
        return ol[:, 0:LANES] / ol[:, LANES:2 * LANES]

    chains = []
    for c in range(ATT_ROW_CHUNKS):
        rows = slice(c * rows_per_chunk, (c + 1) * rows_per_chunk)
        q = q_ref[rows, :]
        first = lax.broadcasted_iota(jnp.int32, q.shape, 1) < ATT_HEAD_DIM
        zero = jnp.zeros_like(q)
        chains.append((rows, jnp.where(first, q, zero)))
        chains.append((rows, jnp.where(first, zero, q)))

    s_val, e_val, o_val = {}, {}, {}
    for t in range(len(chains) + 2):
        if t < len(chains):
            s_val[t] = scores(chains[t][1])
        if 0 <= t - 1 < len(chains):
            e_val[t - 1] = probs(s_val.pop(t - 1))
        if 0 <= t - 2 < len(chains):
            o_val[t - 2] = weighted(e_val.pop(t - 2))
            if (t - 2) % 2 == 1:
                o = o_val.pop(t - 3) - lam_ref[0] * o_val.pop(t - 2)
                o = o * lax.rsqrt(jnp.mean(o * o, axis=-1, keepdims=True) + LN_EPS) * g_ref[...]
                o_ref[chains[t - 2][0], :] = o.astype(BF16)


def _attention(lam, q, k, v, g_row):
    bsz, seq, _ = q.shape
    tq = min(ATT_TILE, seq)
    return pl.pallas_call(
        _attn_kernel,
        grid=(bsz, ATT_HEADS, seq // tq),
        in_specs=[pl.BlockSpec(memory_space=pltpu.SMEM),
                  pl.BlockSpec((None, tq, LANES), lambda b, h, i: (b, i, h)),
                  pl.BlockSpec((None, seq, LANES), lambda b, h, i: (b, 0, h)),
                  pl.BlockSpec((None, seq, LANES), lambda b, h, i: (b, 0, h)),
                  pl.BlockSpec((1, LANES), lambda b, h, i: (0, 0))],
        out_specs=pl.BlockSpec((None, tq, LANES), lambda b, h, i: (b, i, h)),
        out_shape=jax.ShapeDtypeStruct((bsz, seq, ATT_WIDTH), BF16),
        scratch_shapes=[pltpu.VMEM((seq, 2 * LANES), BF16)],
        compiler_params=_cparams(("parallel", "parallel", "arbitrary")),
        name="attn",
    )(lam, q, k, v, g_row)


def _merge_kernel(x_ref, u_ref, y_ref, o_ref, ga_ref, gb_ref, mod_ref, d_ref, wglu_ref, ws5_ref,
                  watt_ref, wo_ref, ln1g_ref, ln1b_ref, wq_ref, x1_ref, h2_ref, pq_ref):
    mod = mod_ref[...]
    rc = x_ref.shape[0] // MERGE_ROW_CHUNKS
    rows = [slice(c * rc, (c + 1) * rc) for c in range(MERGE_ROW_CHUNKS)]
    dot = lambda a, b: jnp.dot(a, b, preferred_element_type=F32)
    s = [_gelu_tanh(u_ref[r, :].astype(F32) * d_ref[...] + y_ref[r, :].astype(F32)) for r in rows]
    glu = [dot(v.astype(BF16), wglu_ref[...]) for v in s]
    s = [a * _sigmoid(b) for a, b in zip(s, glu)]
    br_b = [dot(o_ref[r, :], watt_ref[...]) for r in rows]
    br_a = [dot(v.astype(BF16), ws5_ref[...]) for v in s]
    merged = [ga_ref[r, :].astype(F32) * a + gb_ref[r, :].astype(F32) * b
              for r, a, b in zip(rows, br_a, br_b)]
    z = [dot(v.astype(BF16), wo_ref[...]) for v in merged]
    h2s = []
    for r, zc in zip(rows, z):
        x1 = _ln(DN_ALPHA * x_ref[r, :] + mod[2:3] * zc) * ln1g_ref[...] + ln1b_ref[...]
        x1_ref[r, :] = x1
        h2 = (_ln(x1) * (1.0 + mod[4:5]) + mod[3:4]).astype(BF16)
        h2_ref[r, :] = h2
        h2s.append(h2)
    for r, h2 in zip(rows, h2s):
        pq_ref[r, :] = dot(h2, wq_ref[...]).astype(BF16)


def _merge(x, u, y, o, ga, gb, mod3, wts):
    bsz, seq, _ = x.shape
    tm = min(MERGE_TILE, seq)
    tok = lambda n: pl.BlockSpec((None, tm, n), lambda b, i: (b, i, 0))
    row = lambda n: _const_spec((1, n))
    return pl.pallas_call(
        _merge_kernel,
        grid=(bsz, seq // tm),
        in_specs=[tok(D_MODEL), tok(S5_WIDTH), tok(S5_WIDTH), tok(ATT_WIDTH), tok(D_MODEL), tok(D_MODEL),
                  pl.BlockSpec((None, 6, D_MODEL), lambda b, i: (b, 0, 0)),
                  row(S5_WIDTH), _const_spec((S5_WIDTH, S5_WIDTH)), _const_spec((S5_WIDTH, D_MODEL)),
                  _const_spec((ATT_WIDTH, D_MODEL)), _const_spec((D_MODEL, D_MODEL)),
                  row(D_MODEL), row(D_MODEL), _const_spec((D_MODEL, PEER_NQ))],
        out_specs=[tok(D_MODEL), tok(D_MODEL), tok(PEER_NQ)],
        out_shape=[jax.ShapeDtypeStruct((bsz, seq, D_MODEL), F32),
                   jax.ShapeDtypeStruct((bsz, seq, D_MODEL), BF16),
                   jax.ShapeDtypeStruct((bsz, seq, PEER_NQ), BF16)],
        compiler_params=_cparams(("parallel", "parallel")),
        name="merge",
    )(x, u, y, o, ga, gb, mod3, wts['s5_d'], wts['w_glu'], wts['w_s5_out'], wts['w_attn_out'],
      wts['w_o'], wts['ln1_g'], wts['ln1_b'], wts['w_q'])


def _route_kernel(pq_ref, keys_ref, w_ref, a_sc, b_sc, wt_sc, ix_sc, wr_sc, ir_sc, dp_sc, e2_sc):
    tn = pq_ref.shape[0]
    assert PEER_TOPK == 16 and SUBLANES == 8
    row8 = lax.broadcasted_iota(jnp.int32, (SUBLANES, tn), 0)
    row8_f = row8.astype(F32)
    nt_dims = (((1,), (1,)), ((), ()))
    nv = PEER_NKEYS // SUBLANES

    def sorted_slabs(st, with_index):
        vals = [st[SUBLANES * v:SUBLANES * (v + 1), :] for v in range(nv)]
        idxs = [row8_f + float(SUBLANES * v) for v in range(nv)] if with_index else None
        for lo, hi in _SLAB_SORT_PAIRS:
            a, b = vals[lo], vals[hi]
            vals[lo], vals[hi] = jnp.maximum(a, b), jnp.minimum(a, b)
            if with_index:
                keep = a >= b
                ia, ib = idxs[lo], idxs[hi]
                idxs[lo], idxs[hi] = jnp.where(keep, ia, ib), jnp.where(keep, ib, ia)
        return vals, idxs

    def pop_heads(vals, idxs, pick, k):
        for d in range(min(len(vals) - 1, PEER_TOPK - 1 - k)):
            vals[d] = jnp.where(pick, vals[d + 1], vals[d])
            if idxs is not None:
                idxs[d] = jnp.where(pick, idxs[d + 1], idxs[d])

    for h in range(PEER_HEADS):
        q1 = pq_ref[:, (2 * h) * PEER_HALF:(2 * h + 1) * PEER_HALF]
        q2 = pq_ref[:, (2 * h + 1) * PEER_HALF:(2 * h + 2) * PEER_HALF]
        st1 = lax.dot_general(keys_ref[2 * h], q1, nt_dims, preferred_element_type=F32)
        st2 = lax.dot_general(keys_ref[2 * h + 1], q2, nt_dims, preferred_element_type=F32)
        s2 = st2.T
        e2_sc[h] = jnp.exp(s2 - jnp.max(s2, axis=-1, keepdims=True))

        vals, idxs = sorted_slabs(st1, True)
        a_vals = []
        for k in range(PEER_TOPK):
            m = jnp.max(vals[0], axis=0, keepdims=True)
            ix = jnp.min(jnp.where(vals[0] == m, idxs[0], float(PEER_NKEYS)), axis=0, keepdims=True)
            pop_heads(vals, idxs, idxs[0] == ix, k)
            a_vals.append(m)
            a_sc[k:k + 1, :] = m
            ix_sc[h * PEER_TOPK + k:h * PEER_TOPK + k + 1, :] = ix
        vals, _ = sorted_slabs(st2, False)
        for k in range(PEER_TOPK):
            m = jnp.max(vals[0], axis=0, keepdims=True)
            pop_heads(vals, None, vals[0] == m, k)
            b_sc[k:k + 1, :] = m
        b_lo = b_sc[0:8, :]
        b_hi = b_sc[8:16, :]
        b0 = b_sc[0:1, :]

        shift = lambda x, n: pltpu.roll(x, n, 0)
        cands = [
            a_vals[0] + b_lo,
            a_vals[0] + b_hi,
            a_vals[1] + b_lo,
            jnp.where(row8 < 5, a_vals[2] + b_lo, a_vals[4] + shift(b_lo, 5)),
            jnp.where(row8 < 4, a_vals[3] + b_lo,
                      jnp.where(row8 < 6, a_vals[5] + shift(b_lo, 4), a_vals[6] + shift(b_lo, 6))),
            jnp.where(row8 < 2, a_vals[7] + b_lo, a_sc[6:14, :] + b0),
            jnp.where(row8 < 2, shift(a_sc[8:16, :], 2) + b0, NEG_BIG),
        ]
        m0 = a_vals[0] + b0
        z = jnp.zeros_like(m0)
        tau = m0
        cands.append(jnp.full_like(cands[0], NEG_BIG))
        for lo, hi in _CAND_SORT_PAIRS:
            a, b = cands[lo], cands[hi]
            cands[lo], cands[hi] = jnp.maximum(a, b), jnp.minimum(a, b)
        for k in range(PEER_TOPK):
            tau = jnp.max(cands[0], axis=0, keepdims=True)
            z = z + jnp.exp(tau - m0)
            pop_heads(cands, None, cands[0] == tau, k)
        inv_z = 1.0 / z
        for k in range(PEER_TOPK):
            j = h * PEER_TOPK + k
            wt_sc[j:j + 1, :] = jnp.exp(a_vals[k] - a_vals[0]) * inv_z

        depth = jnp.zeros_like(st2)
        for k in range(DEPTH_DIRECT):
            depth = depth + jnp.where(a_vals[k] + st2 >= tau, 1.0, 0.0)
        for l in range(PEER_TOPK // (DEPTH_DIRECT + 1)):
            b_l = b_sc[l:l + 1, :]
            extra = jnp.zeros_like(b_l)
            for k in range(DEPTH_DIRECT, PEER_TOPK // (l + 1)):
                extra = extra + jnp.where(a_vals[k] + b_l >= tau, 1.0, 0.0)
            depth = depth + jnp.where(st2 == b_l, extra, 0.0)
        dp_sc[h] = depth.T

    wr_sc[...] = wt_sc[...].T
    ir_sc[...] = ix_sc[...].T
    rank_j = (lax.broadcasted_iota(jnp.int32, (PEER_NKEYS, PEER_NKEYS), 0) % PEER_TOPK).astype(F32).astype(BF16)
    sub_j = lax.broadcasted_iota(jnp.int32, (PEER_NKEYS, PEER_NKEYS), 0).astype(F32).astype(BF16)
    zero_b = jnp.zeros((PEER_NKEYS, PEER_NKEYS), BF16)

    def token(t):
        def head_rows(ref):
            return jnp.concatenate(
                [jnp.broadcast_to(ref[h, pl.ds(t, 1), :], (PEER_TOPK, PEER_NKEYS)).astype(BF16)
                 for h in range(PEER_HEADS)], axis=0)

        def all_rows(ref):
            return jnp.broadcast_to(ref[pl.ds(t, 1), :], (PEER_NKEYS, PEER_NKEYS)).astype(BF16)

        r = jnp.where(rank_j < head_rows(dp_sc), head_rows(e2_sc), zero_b)
        pt = jnp.where(sub_j == all_rows(ir_sc), all_rows(wr_sc), zero_b)
        return jnp.dot(pt, r, preferred_element_type=F32)

    def token_group(g, carry):
        for pair in range(ROUTE_PAIRS_PER_TRIP):
            t0 = pl.multiple_of((g * ROUTE_PAIRS_PER_TRIP + pair) * 16, 16)
            halves = []
            for half in range(2):
                tiles = jnp.stack([token(t0 + 8 * half + u) for u in range(8)])
                halves.append(jnp.swapaxes(tiles, 0, 1))
            w_ref[:, pl.ds(t0, 16), :] = jnp.concatenate(halves, axis=1).astype(BF16)
        return carry

    lax.fori_loop(0, tn // (16 * ROUTE_PAIRS_PER_TRIP), token_group, 0)


def _route(pq, keys_b):
    ntok = pq.shape[0]
    tn = ROUTE_TILE
    nj = PEER_HEADS * PEER_TOPK
    return pl.pallas_call(
        _route_kernel,
        grid=(ntok // tn,),
        in_specs=[pl.BlockSpec((tn, PEER_NQ), lambda i: (i, 0)),
                  _const_spec((2 * PEER_HEADS, PEER_NKEYS, PEER_HALF))],
        out_specs=pl.BlockSpec((None, PEER_NKEYS, tn, PEER_NKEYS), lambda i: (i, 0, 0, 0)),
        out_shape=jax.ShapeDtypeStruct((ntok // tn, PEER_NKEYS, tn, PEER_NKEYS), BF16),
        scratch_shapes=[pltpu.VMEM((PEER_TOPK, tn), F32), pltpu.VMEM((PEER_TOPK, tn), F32),
                        pltpu.VMEM((nj, tn), F32), pltpu.VMEM((nj, tn), F32),
                        pltpu.VMEM((tn, nj), F32), pltpu.VMEM((tn, nj), F32),
                        pltpu.VMEM((PEER_HEADS, tn, PEER_NKEYS), F32),
                        pltpu.VMEM((PEER_HEADS, tn, PEER_NKEYS), F32)],
        compiler_params=_cparams(("parallel",)),
        name="route",
    )(pq, keys_b)


def _expert_kernel(h_ref, ut_ref, v_ref, w_ref, x1_ref, mod_ref, g_ref, b_ref, o_ref, acc_ref, gate_sc):
    e = pl.program_id(2)

    @pl.when(e == 0)
    def _():
        acc_ref[...] = jnp.zeros_like(acc_ref)

    act = jnp.dot(h_ref[...], ut_ref[...], preferred_element_type=F32)
    rt = w_ref.shape[2]
    for j in range(w_ref.shape[0]):
        for k in range(w_ref.shape[1]):
            a = act[j * rt:(j + 1) * rt, k * PEER_NKEYS:(k + 1) * PEER_NKEYS].astype(BF16)
            gate_sc[j * rt:(j + 1) * rt, k * PEER_NKEYS:(k + 1) * PEER_NKEYS] = (
                _gelu_tanh(a) * w_ref[j, k])
    acc_ref[...] += jnp.dot(gate_sc[...], v_ref[...], preferred_element_type=F32)

    @pl.when(e == pl.num_programs(2) - 1)
    def _():
        mod = mod_ref[...]
        o_ref[...] = _ln(DN_ALPHA * x1_ref[...] + mod[5:6] * acc_ref[...]) * g_ref[...] + b_ref[...]


def _experts(h2, w4, x1, mod3, ut_b, v_b, ln2_g, ln2_b):
    bsz, seq, _ = x1.shape
    tt = min(EXPERT_TOK_TILE, seq)
    ce = EXPERT_CHUNK
    rt = w4.shape[2]
    nblk = seq // tt
    tok = lambda n: pl.BlockSpec((None, tt, n), lambda b, i, e: (b, i, 0))
    return pl.pallas_call(
        _expert_kernel,
        grid=(bsz, nblk, PEER_EXPERTS // ce),
        in_specs=[tok(D_MODEL),
                  pl.BlockSpec((None, D_MODEL, ce), lambda b, i, e: (e, 0, 0)),
                  pl.BlockSpec((ce, D_MODEL), lambda b, i, e: (e, 0)),
                  pl.BlockSpec((tt // rt, ce // PEER_NKEYS, rt, PEER_NKEYS),
                               lambda b, i, e: (b * nblk + i, e, 0, 0)),
                  tok(D_MODEL),
                  pl.BlockSpec((None, 6, D_MODEL), lambda b, i, e: (b, 0, 0)),
                  pl.BlockSpec((1, D_MODEL), lambda b, i, e: (0, 0)),
                  pl.BlockSpec((1, D_MODEL), lambda b, i, e: (0, 0))],
        out_specs=tok(D_MODEL),
        out_shape=jax.ShapeDtypeStruct((bsz, seq, D_MODEL), F32),
        scratch_shapes=[pltpu.VMEM((tt, D_MODEL), F32), pltpu.VMEM((tt, ce), BF16)],
        compiler_params=_cparams(("parallel", "parallel", "arbitrary")),
        name="expert",
    )(h2, ut_b, v_b, w4, x1, mod3, ln2_g, ln2_b)


def _rope_tables(seq):
    half = ATT_HEAD_DIM // 2
    inv_freq = ROPE_THETA ** (-jnp.arange(half, dtype=F32) * 2.0 / ATT_HEAD_DIM)
    ang = jnp.arange(seq, dtype=F32)[:, None] * inv_freq[None, :]
    cos = jnp.cos(ang)
    sin = jnp.sin(ang)
    zero = jnp.zeros_like(sin)
    reps = LANES // ATT_HEAD_DIM
    cos_t = jnp.tile(jnp.concatenate([cos, cos], axis=1), (1, reps))
    sa_t = jnp.tile(jnp.concatenate([-sin, zero], axis=1), (1, reps))
    sb_t = jnp.tile(jnp.concatenate([zero, sin], axis=1), (1, reps))
    return cos_t, sa_t, sb_t


def _prepare(layer_idx, p):
    lam_init = 0.8 - 0.6 * math.exp(-0.3 * layer_idx)
    lam = (jnp.exp(jnp.sum(p['attn_lambda_q1'].astype(F32) * p['attn_lambda_k1'].astype(F32)))
           - jnp.exp(jnp.sum(p['attn_lambda_q2'].astype(F32) * p['attn_lambda_k2'].astype(F32))) + lam_init)
    row = lambda a: a.astype(F32).reshape(1, -1)
    return dict(
        w_ada=p['w_ada'].astype(F32), b_ada=p['b_ada'].astype(F32),
        w_in=p['w_in'].astype(BF16),
        s5w=_s5_weights(p['s5_lambda_re'], p['s5_lambda_im'], p['s5_log_dt'], p['s5_b_re'],
                        p['s5_b_im'], p['s5_c_re'], p['s5_c_im']),
        s5_d=row(p['s5_d']), w_glu=p['s5_w_glu'].astype(BF16), w_s5_out=p['w_s5_out'].astype(BF16),
        lam=lam.reshape(1).astype(F32),
        subln=row(p['attn_subln_g']) * (1.0 - lam_init),
        w_attn_out=p['w_attn_out'].astype(BF16), w_o=p['w_o'].astype(BF16),
        ln1_g=row(p['ln1_g']), ln1_b=row(p['ln1_b']),
        w_q=p['peer_w_q'].astype(BF16),
        keys=p['peer_keys'].astype(BF16).reshape(2 * PEER_HEADS, PEER_NKEYS, PEER_HALF),
        ut=(p['peer_u'].astype(BF16).reshape(PEER_EXPERTS // EXPERT_CHUNK, EXPERT_CHUNK, D_MODEL)
            .transpose(0, 2, 1)),
        v=p['peer_v'].astype(BF16),
        ln2_g=row(p['ln2_g']), ln2_b=row(p['ln2_b']),
    )


def _encoder_layer(x, c, wts, rope):
    bsz, seq, _ = x.shape
    mod3 = _ada(c.astype(F32), wts['w_ada'], wts['b_ada']).reshape(bsz, 6, D_MODEL)
    u, q, k, v, ga, gb = _inproj(x.astype(F32), mod3, wts['w_in'], *rope)
    y = _s5(u, wts['s5w'])
    o = _attention(wts['lam'], q, k, v, wts['subln'])
    x1, h2, pq = _merge(x.astype(F32), u, y, o, ga, gb, mod3, wts)
    w4 = _route(pq.reshape(bsz * seq, PEER_NQ), wts['keys'])
    x2 = _experts(h2, w4, x1, mod3, wts['ut'], wts['v'], wts['ln2_g'], wts['ln2_b'])
    return x2.astype(x.dtype)


def kernel(x_prompt, x_sample, c_prompt, c_sample, w_ada, b_ada, w_in, s5_lambda_re, s5_lambda_im, s5_log_dt, s5_b_re, s5_b_im, s5_c_re, s5_c_im, s5_d, s5_w_glu, w_s5_out, attn_lambda_q1, attn_lambda_k1, attn_lambda_q2, attn_lambda_k2, attn_subln_g, w_attn_out, w_o, ln1_g, ln1_b, peer_w_q, peer_keys, peer_u, peer_v, ln2_g, ln2_b):
    params = dict(w_ada=w_ada, b_ada=b_ada, w_in=w_in, s5_lambda_re=s5_lambda_re,
                  s5_lambda_im=s5_lambda_im, s5_log_dt=s5_log_dt, s5_b_re=s5_b_re, s5_b_im=s5_b_im,
                  s5_c_re=s5_c_re, s5_c_im=s5_c_im, s5_d=s5_d, s5_w_glu=s5_w_glu, w_s5_out=w_s5_out,
                  attn_lambda_q1=attn_lambda_q1, attn_lambda_k1=attn_lambda_k1,
                  attn_lambda_q2=attn_lambda_q2, attn_lambda_k2=attn_lambda_k2,
                  attn_subln_g=attn_subln_g, w_attn_out=w_attn_out, w_o=w_o, ln1_g=ln1_g, ln1_b=ln1_b,
                  peer_w_q=peer_w_q, peer_keys=peer_keys, peer_u=peer_u, peer_v=peer_v,
                  ln2_g=ln2_g, ln2_b=ln2_b)
    y_prompt, y_sample = x_prompt, x_sample
    rope_p = _rope_tables(x_prompt.shape[1])
    rope_s = _rope_tables(x_sample.shape[1])
    for l in range(DEPTH):
        wts = _prepare(l, {name: a[l] for name, a in params.items()})
        y_prompt = _encoder_layer(y_prompt, c_prompt, wts, rope_p)
        y_sample = _encoder_layer(y_sample, c_sample, wts, rope_s)
    return (y_prompt, y_sample)
```

```python
import functools
import math

import jax
import jax.numpy as jnp
from jax import lax
from jax.experimental import pallas as pl
from jax.experimental.pallas import tpu as pltpu

F32 = jnp.float32
BF16 = jnp.bfloat16
HIGHEST = lax.Precision.HIGHEST

D_MODEL = 1024
DEPTH = 1
S5_WIDTH = D_MODEL // 2
S5_GROUP_CH = 16
S5_GROUPS = S5_WIDTH // S5_GROUP_CH
S5_STATE = 64
S5_CHUNK = 16
S5_ROW = S5_CHUNK * S5_GROUP_CH
ATT_HEADS = 8
ATT_HEAD_DIM = D_MODEL // (2 * ATT_HEADS)
ATT_WIDTH = ATT_HEADS * 2 * ATT_HEAD_DIM
ROPE_THETA = 10000.0
N_IN = S5_WIDTH + 3 * ATT_WIDTH + 2 * D_MODEL
PEER_HEADS = 8
PEER_NKEYS = 128
PEER_EXPERTS = PEER_NKEYS * PEER_NKEYS
PEER_TOPK = 16
PEER_QDIM = 256
PEER_HALF = PEER_QDIM // 2
PEER_NQ = PEER_HEADS * PEER_QDIM
DN_ALPHA = (2 * DEPTH) ** 0.25
LN_EPS = 1e-5

LANES = 128
SUBLANES = 8
VMEM_LIMIT_BYTES = 56 * 1024 * 1024
NEG_BIG = -3.0e38

IN_TILE = 512
IN_ROW_CHUNKS = 2
ATT_TILE = 2048
ATT_ROW_CHUNKS = 4
MERGE_TILE = 512
MERGE_ROW_CHUNKS = 2
ROUTE_TILE = 128
DEPTH_DIRECT = 4
ROUTE_PAIRS_PER_TRIP = 2
EXPERT_TOK_TILE = 1024
EXPERT_CHUNK = 1024


def _merge_exchange_pairs(n):
    pairs = []
    p = 1
    while p < n:
        k = p
        while k >= 1:
            for j in range(k % p, n - k, 2 * k):
                for i in range(min(k, n - j - k)):
                    if (i + j) // (2 * p) == (i + j + k) // (2 * p):
                        pairs.append((i + j, i + j + k))
            k //= 2
        p *= 2
    return pairs


_SLAB_SORT_PAIRS = _merge_exchange_pairs(PEER_NKEYS // SUBLANES)
_CAND_SORT_PAIRS = _merge_exchange_pairs(8)


def _cparams(sem):
    return pltpu.CompilerParams(dimension_semantics=sem, vmem_limit_bytes=VMEM_LIMIT_BYTES)


def _const_spec(shape):
    nd = len(shape)
    return pl.BlockSpec(shape, lambda *_: (0,) * nd, pipeline_mode=pl.Buffered(1))


def _ln(x):
    mu = jnp.mean(x, axis=-1, keepdims=True)
    xc = x - mu
    var = jnp.mean(xc * xc, axis=-1, keepdims=True)
    return xc * lax.rsqrt(var + LN_EPS)


def _gelu_tanh(x):
    return 0.5 * x * (1.0 + jnp.tanh(math.sqrt(2.0 / math.pi) * (x + 0.044715 * (x * x * x))))


def _sigmoid(x):
    return 1.0 / (1.0 + jnp.exp(-x))


def _ada_kernel(c_ref, w_ref, b_ref, o_ref):
    c = c_ref[...]
    s = c * _sigmoid(c)
    o_ref[...] = jnp.dot(s, w_ref[...], precision=HIGHEST, preferred_element_type=F32) + b_ref[...]


def _ada(c, w_ada, b_ada):
    bsz = c.shape[0]
    nblk = w_ada.shape[1] // D_MODEL
    return pl.pallas_call(
        _ada_kernel,
        grid=(nblk,),
        in_specs=[
            pl.BlockSpec((bsz, D_MODEL), lambda j: (0, 0)),
            pl.BlockSpec((D_MODEL, D_MODEL), lambda j: (0, j)),
            pl.BlockSpec((1, D_MODEL), lambda j: (0, j)),
        ],
        out_specs=pl.BlockSpec((bsz, D_MODEL), lambda j: (0, j)),
        out_shape=jax.ShapeDtypeStruct((bsz, nblk * D_MODEL), F32),
        compiler_params=_cparams(("arbitrary",)),
        name="ada",
    )(c, w_ada, b_ada.reshape(1, -1))


def _in_kernel(x_ref, mod_ref, w_ref, cos_ref, sa_ref, sb_ref,
               u_ref, q_ref, k_ref, v_ref, ga_ref, gb_ref):
    mod = mod_ref[...]
    rc = x_ref.shape[0] // IN_ROW_CHUNKS
    rows = [slice(c * rc, (c + 1) * rc) for c in range(IN_ROW_CHUNKS)]
    hbs = [(_ln(x_ref[r, :]) * (1.0 + mod[1:2]) + mod[0:1]).astype(BF16) for r in rows]

    def proj(c, a, b):
        return jnp.dot(hbs[c], w_ref[:, a:b], preferred_element_type=F32)

    o1 = S5_WIDTH
    o2 = o1 + ATT_WIDTH
    o3 = o2 + ATT_WIDTH
    o4 = o3 + ATT_WIDTH
    o5 = o4 + D_MODEL
    half = ATT_HEAD_DIM // 2
    q_scale = ATT_HEAD_DIM ** -0.5 * math.log2(math.e)

    def rope_store(t, o_ref, r, scale):
        cos, sa, sb = cos_ref[r, :] * scale, sa_ref[r, :] * scale, sb_ref[r, :] * scale
        for j in range(ATT_WIDTH // LANES):
            tj = t[:, j * LANES:(j + 1) * LANES]
            rot = tj * cos + pltpu.roll(tj, LANES - half, 1) * sa + pltpu.roll(tj, half, 1) * sb
            o_ref[r, j * LANES:(j + 1) * LANES] = rot.astype(BF16)

    for c, r in enumerate(rows):
        u_ref[r, :] = proj(c, 0, o1).astype(BF16)
    for c, r in enumerate(rows):
        rope_store(proj(c, o1, o2), q_ref, r, q_scale)
    for c, r in enumerate(rows):
        rope_store(proj(c, o2, o3), k_ref, r, 1.0)
    for c, r in enumerate(rows):
        v_ref[r, :] = proj(c, o3, o4).astype(BF16)
    for c, r in enumerate(rows):
        ga_ref[r, :] = _sigmoid(proj(c, o4, o5)).astype(BF16)
    for c, r in enumerate(rows):
        gb_ref[r, :] = _sigmoid(proj(c, o5, N_IN)).astype(BF16)


def _inproj(x, mod3, w_in_b, cos, sa, sb):
    bsz, seq, _ = x.shape
    tm = min(IN_TILE, seq)
    tok = lambda n: pl.BlockSpec((None, tm, n), lambda b, i: (b, i, 0))
    rope = pl.BlockSpec((tm, LANES), lambda b, i: (i, 0))
    outs = [(S5_WIDTH, BF16), (ATT_WIDTH, BF16), (ATT_WIDTH, BF16), (ATT_WIDTH, BF16),
            (D_MODEL, BF16), (D_MODEL, BF16)]
    return pl.pallas_call(
        _in_kernel,
        grid=(bsz, seq // tm),
        in_specs=[tok(D_MODEL),
                  pl.BlockSpec((None, 6, D_MODEL), lambda b, i: (b, 0, 0)),
                  _const_spec((D_MODEL, N_IN)), rope, rope, rope],
        out_specs=[tok(n) for n, _ in outs],
        out_shape=[jax.ShapeDtypeStruct((bsz, seq, n), dt) for n, dt in outs],
        compiler_params=_cparams(("parallel", "parallel")),
        name="inproj",
    )(x, mod3, w_in_b, cos, sa, sb)


def _s5_weights(lam_re, lam_im, log_dt, b_re, b_im, c_re, c_im):
    n = S5_CHUNK
    per_dir = []
    for d in range(2):
        lr = lam_re[d].astype(F32)
        li = lam_im[d].astype(F32)
        dt = jnp.exp(log_dt[d].astype(F32))[:, None]
        mag = jnp.exp(lr * dt)
        lbr = mag * jnp.cos(li * dt)
        lbi = mag * jnp.sin(li * dt)
        den = lr * lr + li * li
        nr = lbr - 1.0
        cr = (nr * lr + lbi * li) / den
        ci = (lbi * lr - nr * li) / den
        br = b_re[d].astype(F32)
        bi = b_im[d].astype(F32)
        bbr = cr[..., None] * br - ci[..., None] * bi
        bbi = cr[..., None] * bi + ci[..., None] * br
        pr = [jnp.ones_like(lbr)]
        pi = [jnp.zeros_like(lbi)]
        for _ in range(n):
            pr.append(pr[-1] * lbr - pi[-1] * lbi)
            pi.append(pr[-2] * lbi + pi[-1] * lbr)
        pr = jnp.stack(pr)
        pi = jnp.stack(pi)
        cre = c_re[d].astype(F32)
        cim = c_im[d].astype(F32)
        cpr = cre[None] * pr[:, :, None, :] - cim[None] * pi[:, :, None, :]
        cpi = cre[None] * pi[:, :, None, :] + cim[None] * pr[:, :, None, :]
        kern = (jnp.einsum('ngcp,gpd->ngcd', cpr, bbr, precision=HIGHEST)
                - jnp.einsum('ngcp,gpd->ngcd', cpi, bbi, precision=HIGHEST))
        pbr = pr[..., None] * bbr[None] - pi[..., None] * bbi[None]
        pbi = pr[..., None] * bbi[None] + pi[..., None] * bbr[None]
        per_dir.append(dict(pr=pr, pi=pi, cpr=cpr, cpi=cpi, kern=kern, pbr=pbr, pbi=pbi))

    g, c, p = S5_GROUPS, S5_GROUP_CH, S5_STATE
    s_idx = jnp.arange(n)[:, None]
    t_idx = jnp.arange(n)[None, :]
    tau_f = t_idx - s_idx
    tau_b = s_idx - t_idx
    kf = per_dir[0]['kern'][jnp.clip(tau_f, 0, n)] * (tau_f >= 0)[..., None, None, None]
    kb = per_dir[1]['kern'][jnp.clip(tau_b, 0, n)] * (tau_b >= 0)[..., None, None, None]
    m = (kf + kb).transpose(2, 0, 4, 1, 3).reshape(g, S5_ROW, S5_ROW)

    pow_f = n - 1 - jnp.arange(n)
    pow_b = jnp.arange(n)

    def ws_part(key, d, pows):
        return per_dir[d][key][pows].transpose(1, 0, 3, 2).reshape(g, S5_ROW, p)

    ws = jnp.concatenate([ws_part('pbr', 0, pow_f), ws_part('pbr', 1, pow_b),
                          ws_part('pbi', 0, pow_f), ws_part('pbi', 1, pow_b)], axis=-1)

    out_f = jnp.arange(n) + 1
    out_b = n - jnp.arange(n)

    def wo_part(key, d, pows):
        return per_dir[d][key][pows].transpose(1, 3, 0, 2).reshape(g, p, S5_ROW)

    zero = jnp.zeros((g, p, S5_ROW), F32)
    wof = jnp.concatenate([wo_part('cpr', 0, out_f), zero, -wo_part('cpi', 0, out_f), zero], axis=1)
    wob = jnp.concatenate([zero, wo_part('cpr', 1, out_b), zero, -wo_part('cpi', 1, out_b)], axis=1)
    ar = jnp.concatenate([per_dir[0]['pr'][n], per_dir[1]['pr'][n]], axis=-1)[:, None, :]
    ai = jnp.concatenate([per_dir[0]['pi'][n], per_dir[1]['pi'][n]], axis=-1)[:, None, :]
    return m.astype(BF16), ws.astype(BF16), wof.astype(BF16), wob.astype(BF16), ar, ai


def _s5_kernel(u_ref, m_ref, ws_ref, wof_ref, wob_ref, ar_ref, ai_ref, y_ref,
               s_ref, hf_ref, hb_ref, *, nb, nchunk):
    u = u_ref[...].T
    rows = nb * nchunk

    def chunk_major(x):
        return jnp.swapaxes(x.reshape(nb, nchunk, x.shape[-1]), 0, 1).reshape(rows, x.shape[-1])

    def batch_major(x):
        return jnp.swapaxes(x.reshape(nchunk, nb, x.shape[-1]), 0, 1).reshape(rows, x.shape[-1])

    s_ref[...] = chunk_major(jnp.dot(u, ws_ref[...], preferred_element_type=F32))
    ar = ar_ref[...]
    ai = ai_ref[...]
    is_fwd = lax.broadcasted_iota(jnp.int32, (nb, LANES), 1) < S5_STATE

    def step(i, carry):
        h_re, h_im = carry
        rf = pl.multiple_of(i * nb, 8)
        rb = pl.multiple_of((nchunk - 1 - i) * nb, 8)
        hf_ref[pl.ds(rf, nb), 0:LANES] = h_re
        hf_ref[pl.ds(rf, nb), LANES:2 * LANES] = h_im
        hb_ref[pl.ds(rb, nb), 0:LANES] = h_re
        hb_ref[pl.ds(rb, nb), LANES:2 * LANES] = h_im
        s_re = jnp.where(is_fwd, s_ref[pl.ds(rf, nb), 0:LANES], s_ref[pl.ds(rb, nb), 0:LANES])
        s_im = jnp.where(is_fwd, s_ref[pl.ds(rf, nb), LANES:2 * LANES],
                         s_ref[pl.ds(rb, nb), LANES:2 * LANES])
        return (ar * h_re - ai * h_im + s_re, ar * h_im + ai * h_re + s_im)

    zero = jnp.zeros((nb, LANES), F32)
    lax.fori_loop(0, nchunk, step, (zero, zero))
    y = jnp.dot(u, m_ref[...], preferred_element_type=F32)
    y = y + jnp.dot(batch_major(hf_ref[...]).astype(BF16), wof_ref[...], preferred_element_type=F32)
    y = y + jnp.dot(batch_major(hb_ref[...]).astype(BF16), wob_ref[...], preferred_element_type=F32)
    y_ref[...] = y.T.astype(BF16)


def _s5(u, s5w):
    m, ws, wof, wob, ar, ai = s5w
    bsz, seq, _ = u.shape
    nchunk = seq // S5_CHUNK
    rows = nchunk * bsz
    u2 = (u.reshape(bsz, nchunk, S5_CHUNK, S5_GROUPS, S5_GROUP_CH)
          .transpose(3, 2, 4, 0, 1).reshape(S5_GROUPS, S5_ROW, rows))
    grp = lambda r, c: pl.BlockSpec((None, r, c), lambda g: (g, 0, 0))
    y2 = pl.pallas_call(
        functools.partial(_s5_kernel, nb=bsz, nchunk=nchunk),
        grid=(S5_GROUPS,),
        in_specs=[grp(S5_ROW, rows), grp(S5_ROW, S5_ROW), grp(S5_ROW, S5_ROW),
                  grp(S5_ROW, S5_ROW), grp(S5_ROW, S5_ROW), grp(1, LANES), grp(1, LANES)],
        out_specs=grp(S5_ROW, rows),
        out_shape=jax.ShapeDtypeStruct((S5_GROUPS, S5_ROW, rows), BF16),
        scratch_shapes=[pltpu.VMEM((rows, S5_ROW), F32)] * 3,
        compiler_params=_cparams(("parallel",)),
        name="s5",
    )(u2, m, ws, wof, wob, ar, ai)
    return (y2.reshape(S5_GROUPS, S5_CHUNK, S5_GROUP_CH, bsz, nchunk)
            .transpose(3, 4, 1, 0, 2).reshape(bsz, seq, S5_WIDTH))


def _attn_kernel(lam_ref, q_ref, k_ref, v_ref, g_ref, o_ref, vx_ref):
    @pl.when(pl.program_id(2) == 0)
    def _():
        vx_ref[:, 0:LANES] = v_ref[...]
        vx_ref[:, LANES:2 * LANES] = jnp.ones(v_ref.shape, BF16)

    k = k_ref[...]
    vx = vx_ref[...]
    rows_per_chunk = q_ref.shape[0] // ATT_ROW_CHUNKS

    def scores(qm):
        return lax.dot_general(qm, k, (((1,), (1,)), ((), ())), preferred_element_type=F32)

    def probs(s):
        return jnp.exp2(s - jnp.max(s, axis=-1, keepdims=True)).astype(BF16)

    def weighted(e):
        ol = jnp.dot(e, vx, preferred_element_type=F32)
        return ol[:, 0:LANES] / ol[:, LANES:2 * LANES]

    chains = []
    for c in range(ATT_ROW_CHUNKS):
        rows = slice(c * rows_per_chunk, (c + 1) * rows_per_chunk)
        q = q_ref[rows, :]
        first = lax.broadcasted_iota(jnp.int32, q.shape, 1) < ATT_HEAD_DIM
        zero = jnp.zeros_like(q)
        chains.append((rows, jnp.where(first, q, zero)))
        chains.append((rows, jnp.where(first, zero, q)))

    s_val, e_val, o_val = {}, {}, {}
    for t in range(len(chains) + 2):
        if t < len(chains):
            s_val[t] = scores(chains[t][1])
        if 0 <= t - 1 < len(chains):
            e_val[t - 1] = probs(s_val.pop(t - 1))
        if 0 <= t - 2 < len(chains):
            o_val[t - 2] = weighted(e_val.pop(t - 2))
            if (t - 2) % 2 == 1:
                o = o_val.pop(t - 3) - lam_ref[0] * o_val.pop(t - 2)
                o = o * lax.rsqrt(jnp.mean(o * o, axis=-1, keepdims=True) + LN_EPS) * g_ref[...]
                o_ref[chains[t - 2][0], :] = o.astype(BF16)


def _attention(lam, q, k, v, g_row):
    bsz, seq, _ = q.shape
    tq = min(ATT_TILE, seq)
    return pl.pallas_call(
        _attn_kernel,
        grid=(bsz, ATT_HEADS, seq // tq),
        in_specs=[pl.BlockSpec(memory_space=pltpu.SMEM),
                  pl.BlockSpec((None, tq, LANES), lambda b, h, i: (b, i, h)),
                  pl.BlockSpec((None, seq, LANES), lambda b, h, i: (b, 0, h)),
                  pl.BlockSpec((None, seq, LANES), lambda b, h, i: (b, 0, h)),
                  pl.BlockSpec((1, LANES), lambda b, h, i: (0, 0))],
        out_specs=pl.BlockSpec((None, tq, LANES), lambda b, h, i: (b, i, h)),
        out_shape=jax.ShapeDtypeStruct((bsz, seq, ATT_WIDTH), BF16),
        scratch_shapes=[pltpu.VMEM((seq, 2 * LANES), BF16)],
        compiler_params=_cparams(("parallel", "parallel", "arbitrary")),
        name="attn",
    )(lam, q, k, v, g_row)


def _merge_kernel(x_ref, u_ref, y_ref, o_ref, ga_ref, gb_ref, mod_ref, d_ref, wglu_ref, ws5_ref,
                  watt_ref, wo_ref, ln1g_ref, ln1b_ref, wq_ref, x1_ref, h2_ref, pq_ref):
    mod = mod_ref[...]
    rc = x_ref.shape[0] // MERGE_ROW_CHUNKS
    rows = [slice(c * rc, (c + 1) * rc) for c in range(MERGE_ROW_CHUNKS)]
    dot = lambda a, b: jnp.dot(a, b, preferred_element_type=F32)
    s = [_gelu_tanh(u_ref[r, :].astype(F32) * d_ref[...] + y_ref[r, :].astype(F32)) for r in rows]
    glu = [dot(v.astype(BF16), wglu_ref[...]) for v in s]
    s = [a * _sigmoid(b) for a, b in zip(s, glu)]
    br_b = [dot(o_ref[r, :], watt_ref[...]) for r in rows]
    br_a = [dot(v.astype(BF16), ws5_ref[...]) for v in s]
    merged = [ga_ref[r, :].astype(F32) * a + gb_ref[r, :].astype(F32) * b
              for r, a, b in zip(rows, br_a, br_b)]
    z = [dot(v.astype(BF16), wo_ref[...]) for v in merged]
    h2s = []
    for r, zc in zip(rows, z):
        x1 = _ln(DN_ALPHA * x_ref[r, :] + mod[2:3] * zc) * ln1g_ref[...] + ln1b_ref[...]
        x1_ref[r, :] = x1
        h2 = (_ln(x1) * (1.0 + mod[4:5]) + mod[3:4]).astype(BF16)
        h2_ref[r, :] = h2
        h2s.append(h2)
    for r, h2 in zip(rows, h2s):
        pq_ref[r, :] = dot(h2, wq_ref[...]).astype(BF16)


def _merge(x, u, y, o, ga, gb, mod3, wts):
    bsz, seq, _ = x.shape
    tm = min(MERGE_TILE, seq)
    tok = lambda n: pl.BlockSpec((None, tm, n), lambda b, i: (b, i, 0))
    row = lambda n: _const_spec((1, n))
    return pl.pallas_call(
        _merge_kernel,
        grid=(bsz, seq // tm),
        in_specs=[tok(D_MODEL), tok(S5_WIDTH), tok(S5_WIDTH), tok(ATT_WIDTH), tok(D_MODEL), tok(D_MODEL),
                  pl.BlockSpec((None, 6, D_MODEL), lambda b, i: (b, 0, 0)),
                  row(S5_WIDTH), _const_spec((S5_WIDTH, S5_WIDTH)), _const_spec((S5_WIDTH, D_MODEL)),
                  _const_spec((ATT_WIDTH, D_MODEL)), _const_spec((D_MODEL, D_MODEL)),
                  row(D_MODEL), row(D_MODEL), _const_spec((D_MODEL, PEER_NQ))],
        out_specs=[tok(D_MODEL), tok(D_MODEL), tok(PEER_NQ)],
        out_shape=[jax.ShapeDtypeStruct((bsz, seq, D_MODEL), F32),
                   jax.ShapeDtypeStruct((bsz, seq, D_MODEL), BF16),
                   jax.ShapeDtypeStruct((bsz, seq, PEER_NQ), BF16)],
        compiler_params=_cparams(("parallel", "parallel")),
        name="merge",
    )(x, u, y, o, ga, gb, mod3, wts['s5_d'], wts['w_glu'], wts['w_s5_out'], wts['w_attn_out'],
      wts['w_o'], wts['ln1_g'], wts['ln1_b'], wts['w_q'])


def _route_kernel(pq_ref, keys_ref, w_ref, a_sc, b_sc, wt_sc, ix_sc, wr_sc, ir_sc, dp_sc, e2_sc):
    tn = pq_ref.shape[0]
    assert PEER_TOPK == 16 and SUBLANES == 8
    row8 = lax.broadcasted_iota(jnp.int32, (SUBLANES, tn), 0)
    row8_f = row8.astype(F32)
    nt_dims = (((1,), (1,)), ((), ()))
    nv = PEER_NKEYS // SUBLANES

    def sorted_slabs(st, with_index):
        vals = [st[SUBLANES * v:SUBLANES * (v + 1), :] for v in range(nv)]
        idxs = [row8_f + float(SUBLANES * v) for v in range(nv)] if with_index else None
        for lo, hi in _SLAB_SORT_PAIRS:
            a, b = vals[lo], vals[hi]
            vals[lo], vals[hi] = jnp.maximum(a, b), jnp.minimum(a, b)
            if with_index:
                keep = a >= b
                ia, ib = idxs[lo], idxs[hi]
                idxs[lo], idxs[hi] = jnp.where(keep, ia, ib), jnp.where(keep, ib, ia)
        return vals, idxs

    def pop_heads(vals, idxs, pick, k):
        for d in range(min(len(vals) - 1, PEER_TOPK - 1 - k)):
            vals[d] = jnp.where(pick, vals[d + 1], vals[d])
            if idxs is not None:
                idxs[d] = jnp.where(pick, idxs[d + 1], idxs[d])

    for h in range(PEER_HEADS):
        q1 = pq_ref[:, (2 * h) * PEER_HALF:(2 * h + 1) * PEER_HALF]
        q2 = pq_ref[:, (2 * h + 1) * PEER_HALF:(2 * h + 2) * PEER_HALF]
        st1 = lax.dot_general(keys_ref[2 * h], q1, nt_dims, preferred_element_type=F32)
        st2 = lax.dot_general(keys_ref[2 * h + 1], q2, nt_dims, preferred_element_type=F32)
        s2 = st2.T
        e2_sc[h] = jnp.exp(s2 - jnp.max(s2, axis=-1, keepdims=True))

        vals, idxs = sorted_slabs(st1, True)
        a_vals = []
        for k in range(PEER_TOPK):
            m = jnp.max(vals[0], axis=0, keepdims=True)
            ix = jnp.min(jnp.where(vals[0] == m, idxs[0], float(PEER_NKEYS)), axis=0, keepdims=True)
            pop_heads(vals, idxs, idxs[0] == ix, k)
            a_vals.append(m)
            a_sc[k:k + 1, :] = m
            ix_sc[h * PEER_TOPK + k:h * PEER_TOPK + k + 1, :] = ix
        vals, _ = sorted_slabs(st2, False)
        for k in range(PEER_TOPK):
            m = jnp.max(vals[0], axis=0, keepdims=True)
            pop_heads(vals, None, vals[0] == m, k)
            b_sc[k:k + 1, :] = m
        b_lo = b_sc[0:8, :]
        b_hi = b_sc[8:16, :]
        b0 = b_sc[0:1, :]

        shift = lambda x, n: pltpu.roll(x, n, 0)
        cands = [
            a_vals[0] + b_lo,
            a_vals[0] + b_hi,
            a_vals[1] + b_lo,
            jnp.where(row8 < 5, a_vals[2] + b_lo, a_vals[4] + shift(b_lo, 5)),
            jnp.where(row8 < 4, a_vals[3] + b_lo,
                      jnp.where(row8 < 6, a_vals[5] + shift(b_lo, 4), a_vals[6] + shift(b_lo, 6))),
            jnp.where(row8 < 2, a_vals[7] + b_lo, a_sc[6:14, :] + b0),
            jnp.where(row8 < 2, shift(a_sc[8:16, :], 2) + b0, NEG_BIG),
        ]
        m0 = a_vals[0] + b0
        z = jnp.zeros_like(m0)
        tau = m0
        cands.append(jnp.full_like(cands[0], NEG_BIG))
        for lo, hi in _CAND_SORT_PAIRS:
            a, b = cands[lo], cands[hi]
            cands[lo], cands[hi] = jnp.maximum(a, b), jnp.minimum(a, b)
        for k in range(PEER_TOPK):
            tau = jnp.max(cands[0], axis=0, keepdims=True)
            z = z + jnp.exp(tau - m0)
            pop_heads(cands, None, cands[0] == tau, k)
        inv_z = 1.0 / z
        for k in range(PEER_TOPK):
            j = h * PEER_TOPK + k
            wt_sc[j:j + 1, :] = jnp.exp(a_vals[k] - a_vals[0]) * inv_z

        depth = jnp.zeros_like(st2)
        for k in range(DEPTH_DIRECT):
            depth = depth + jnp.where(a_vals[k] + st2 >= tau, 1.0, 0.0)
        for l in range(PEER_TOPK // (DEPTH_DIRECT + 1)):
            b_l = b_sc[l:l + 1, :]
            extra = jnp.zeros_like(b_l)
            for k in range(DEPTH_DIRECT, PEER_TOPK // (l + 1)):
                extra = extra + jnp.where(a_vals[k] + b_l >= tau, 1.0, 0.0)
            depth = depth + jnp.where(st2 == b_l, extra, 0.0)
        dp_sc[h] = depth.T

    wr_sc[...] = wt_sc[...].T
    ir_sc[...] = ix_sc[...].T
    rank_j = (lax.broadcasted_iota(jnp.int32, (PEER_NKEYS, PEER_NKEYS), 0) % PEER_TOPK).astype(F32).astype(BF16)
    sub_j = lax.broadcasted_iota(jnp.int32, (PEER_NKEYS, PEER_NKEYS), 0).astype(F32).astype(BF16)
    zero_b = jnp.zeros((PEER_NKEYS, PEER_NKEYS), BF16)

    def token(t):
        def head_rows(ref):
            return jnp.concatenate(
                [jnp.broadcast_to(ref[h, pl.ds(t, 1), :], (PEER_TOPK, PEER_NKEYS)).astype(BF16)
                 for h in range(PEER_HEADS)], axis=0)

        def all_rows(ref):
            return jnp.broadcast_to(ref[pl.ds(t, 1), :], (PEER_NKEYS, PEER_NKEYS)).astype(BF16)

        r = jnp.where(rank_j < head_rows(dp_sc), head_rows(e2_sc), zero_b)
        pt = jnp.where(sub_j == all_rows(ir_sc), all_rows(wr_sc), zero_b)
        return jnp.dot(pt, r, preferred_element_type=F32)

    def token_group(g, carry):
        for pair in range(ROUTE_PAIRS_PER_TRIP):
            t0 = pl.multiple_of((g * ROUTE_PAIRS_PER_TRIP + pair) * 16, 16)
            halves = []
            for half in range(2):
                tiles = jnp.stack([token(t0 + 8 * half + u) for u in range(8)])
                halves.append(jnp.swapaxes(tiles, 0, 1))
            w_ref[:, pl.ds(t0, 16), :] = jnp.concatenate(halves, axis=1).astype(BF16)
        return carry

    lax.fori_loop(0, tn // (16 * ROUTE_PAIRS_PER_TRIP), token_group, 0)


def _route(pq, keys_b):
    ntok = pq.shape[0]
    tn = ROUTE_TILE
    nj = PEER_HEADS * PEER_TOPK
    return pl.pallas_call(
        _route_kernel,
        grid=(ntok // tn,),
        in_specs=[pl.BlockSpec((tn, PEER_NQ), lambda i: (i, 0)),
                  _const_spec((2 * PEER_HEADS, PEER_NKEYS, PEER_HALF))],
        out_specs=pl.BlockSpec((None, PEER_NKEYS, tn, PEER_NKEYS), lambda i: (i, 0, 0, 0)),
        out_shape=jax.ShapeDtypeStruct((ntok // tn, PEER_NKEYS, tn, PEER_NKEYS), BF16),
        scratch_shapes=[pltpu.VMEM((PEER_TOPK, tn), F32), pltpu.VMEM((PEER_TOPK, tn), F32),
                        pltpu.VMEM((nj, tn), F32), pltpu.VMEM((nj, tn), F32),
                        pltpu.VMEM((tn, nj), F32), pltpu.VMEM((tn, nj), F32),
                        pltpu.VMEM((PEER_HEADS, tn, PEER_NKEYS), F32),
                        pltpu.VMEM((PEER_HEADS, tn, PEER_NKEYS), F32)],
        compiler_params=_cparams(("parallel",)),
        name="route",
    )(pq, keys_b)


def _expert_kernel(h_ref, ut_ref, v_ref, w_ref, x1_ref, mod_ref, g_ref, b_ref, o_ref, acc_ref, gate_sc):
    e = pl.program_id(2)

    @pl.when(e == 0)
    def _():
        acc_ref[...] = jnp.zeros_like(acc_ref)

    act = jnp.dot(h_ref[...], ut_ref[...], preferred_element_type=F32)
    rt = w_ref.shape[2]
    for j in range(w_ref.shape[0]):
        for k in range(w_ref.shape[1]):
            a = act[j * rt:(j + 1) * rt, k * PEER_NKEYS:(k + 1) * PEER_NKEYS].astype(BF16)
            gate_sc[j * rt:(j + 1) * rt, k * PEER_NKEYS:(k + 1) * PEER_NKEYS] = (
                _gelu_tanh(a) * w_ref[j, k])
    acc_ref[...] += jnp.dot(gate_sc[...], v_ref[...], preferred_element_type=F32)

    @pl.when(e == pl.num_programs(2) - 1)
    def _():
        mod = mod_ref[...]
        o_ref[...] = _ln(DN_ALPHA * x1_ref[...] + mod[5:6] * acc_ref[...]) * g_ref[...] + b_ref[...]


def _experts(h2, w4, x1, mod3, ut_b, v_b, ln2_g, ln2_b):
    bsz, seq, _ = x1.shape
    tt = min(EXPERT_TOK_TILE, seq)
    ce = EXPERT_CHUNK
    rt = w4.shape[2]
    nblk = seq // tt
    tok = lambda n: pl.BlockSpec((None, tt, n), lambda b, i, e: (b, i, 0))
    return pl.pallas_call(
        _expert_kernel,
        grid=(bsz, nblk, PEER_EXPERTS // ce),
        in_specs=[tok(D_MODEL),
                  pl.BlockSpec((None, D_MODEL, ce), lambda b, i, e: (e, 0, 0)),
                  pl.BlockSpec((ce, D_MODEL), lambda b, i, e: (e, 0)),
                  pl.BlockSpec((tt // rt, ce // PEER_NKEYS, rt, PEER_NKEYS),
                               lambda b, i, e: (b * nblk + i, e, 0, 0)),
                  tok(D_MODEL),
                  pl.BlockSpec((None, 6, D_MODEL), lambda b, i, e: (b, 0, 0)),
                  pl.BlockSpec((1, D_MODEL), lambda b, i, e: (0, 0)),
                  pl.BlockSpec((1, D_MODEL), lambda b, i, e: (0, 0))],
        out_specs=tok(D_MODEL),
        out_shape=jax.ShapeDtypeStruct((bsz, seq, D_MODEL), F32),
        scratch_shapes=[pltpu.VMEM((tt, D_MODEL), F32), pltpu.VMEM((tt, ce), BF16)],
        compiler_params=_cparams(("parallel", "parallel", "arbitrary")),
        name="expert",
    )(h2, ut_b, v_b, w4, x1, mod3, ln2_g, ln2_b)


def _rope_tables(seq):
    half = ATT_HEAD_DIM // 2
    inv_freq = ROPE_THETA ** (-jnp.arange(half, dtype=F32) * 2.0 / ATT_HEAD_DIM)
    ang = jnp.arange(seq, dtype=F32)[:, None] * inv_freq[None, :]
    cos = jnp.cos(ang)
    sin = jnp.sin(ang)
    zero = jnp.zeros_like(sin)
    reps = LANES // ATT_HEAD_DIM
    cos_t = jnp.tile(jnp.concatenate([cos, cos], axis=1), (1, reps))
    sa_t = jnp.tile(jnp.concatenate([-sin, zero], axis=1), (1, reps))
    sb_t = jnp.tile(jnp.concatenate([zero, sin], axis=1), (1, reps))
    return cos_t, sa_t, sb_t


def _prepare(layer_idx, p):
    lam_init = 0.8 - 0.6 * math.exp(-0.3 * layer_idx)
    lam = (jnp.exp(jnp.sum(p['attn_lambda_q1'].astype(F32) * p['attn_lambda_k1'].astype(F32)))
           - jnp.exp(jnp.sum(p['attn_lambda_q2'].astype(F32) * p['attn_lambda_k2'].astype(F32))) + lam_init)
    row = lambda a: a.astype(F32).reshape(1, -1)
    return dict(
        w_ada=p['w_ada'].astype(F32), b_ada=p['b_ada'].astype(F32),
        w_in=p['w_in'].astype(BF16),
        s5w=_s5_weights(p['s5_lambda_re'], p['s5_lambda_im'], p['s5_log_dt'], p['s5_b_re'],
                        p['s5_b_im'], p['s5_c_re'], p['s5_c_im']),
        s5_d=row(p['s5_d']), w_glu=p['s5_w_glu'].astype(BF16), w_s5_out=p['w_s5_out'].astype(BF16),
        lam=lam.reshape(1).astype(F32),
        subln=row(p['attn_subln_g']) * (1.0 - lam_init),
        w_attn_out=p['w_attn_out'].astype(BF16), w_o=p['w_o'].astype(BF16),
        ln1_g=row(p['ln1_g']), ln1_b=row(p['ln1_b']),
        w_q=p['peer_w_q'].astype(BF16),
        keys=p['peer_keys'].astype(BF16).reshape(2 * PEER_HEADS, PEER_NKEYS, PEER_HALF),
        ut=(p['peer_u'].astype(BF16).reshape(PEER_EXPERTS // EXPERT_CHUNK, EXPERT_CHUNK, D_MODEL)
            .transpose(0, 2, 1)),
        v=p['peer_v'].astype(BF16),
        ln2_g=row(p['ln2_g']), ln2_b=row(p['ln2_b']),
    )


def _encoder_layer(x, c, wts, rope):
    bsz, seq, _ = x.shape
    mod3 = _ada(c.astype(F32), wts['w_ada'], wts['b_ada']).reshape(bsz, 6, D_MODEL)
    u, q, k, v, ga, gb = _inproj(x.astype(F32), mod3, wts['w_in'], *rope)
    y = _s5(u, wts['s5w'])
    o = _attention(wts['lam'], q, k, v, wts['subln'])
    x1, h2, pq = _merge(x.astype(F32), u, y, o, ga, gb, mod3, wts)
    w4 = _route(pq.reshape(bsz * seq, PEER_NQ), wts['keys'])
    x2 = _experts(h2, w4, x1, mod3, wts['ut'], wts['v'], wts['ln2_g'], wts['ln2_b'])
    return x2.astype(x.dtype)


def kernel(x_prompt, x_sample, c_prompt, c_sample, w_ada, b_ada, w_in, s5_lambda_re, s5_lambda_im, s5_log_dt, s5_b_re, s5_b_im, s5_c_re, s5_c_im, s5_d, s5_w_glu, w_s5_out, attn_lambda_q1, attn_lambda_k1, attn_lambda_q2, attn_lambda_k2, attn_subln_g, w_attn_out, w_o, ln1_g, ln1_b, peer_w_q, peer_keys, peer_u, peer_v, ln2_g, ln2_b):
    params = dict(w_ada=w_ada, b_ada=b_ada, w_in=w_in, s5_lambda_re=s5_lambda_re,
                  s5_lambda_im=s5_lambda_im, s5_log_dt=s5_log_dt, s5_b_re=s5_b_re, s5_b_im=s5_b_im,
                  s5_c_re=s5_c_re, s5_c_im=s5_c_im, s5_d=s5_d, s5_w_glu=s5_w_glu, w_s5_out=w_s5_out,
                  attn_lambda_q1=attn_lambda_q1, attn_lambda_k1=attn_lambda_k1,
                  attn_lambda_q2=attn_lambda_q2, attn_lambda_k2=attn_lambda_k2,
                  attn_subln_g=attn_subln_g, w_attn_out=w_attn_out, w_o=w_o, ln1_g=ln1_g, ln1_b=ln1_b,
                  peer_w_q=peer_w_q, peer_keys=peer_keys, peer_u=peer_u, peer_v=peer_v,
                  ln2_g=ln2_g, ln2_b=ln2_b)
    y_prompt, y_sample = x_prompt, x_sample
    rope_p = _rope_tables(x_prompt.shape[1])
    rope_s = _rope_tables(x_sample.shape[1])
    for l in range(DEPTH):
        wts = _prepare(l, {name: a[l] for name, a in params.items()})
        y_prompt = _encoder_layer(y_prompt, c_prompt, wts, rope_p)
        y_sample = _encoder_layer(y_sample, c_sample, wts, rope_s)
    return (y_prompt, y_sample)
```

```python
import functools
import math

import jax
import jax.numpy as jnp
from jax import lax
from jax.experimental import pallas as pl
from jax.experimental.pallas import tpu as pltpu

F32 = jnp.float32
BF16 = jnp.bfloat16
HIGHEST = lax.Precision.HIGHEST

D_MODEL = 1024
DEPTH = 1
S5_WIDTH = D_MODEL // 2
S5_GROUP_CH = 16
S5_GROUPS = S5_WIDTH // S5_GROUP_CH
S5_STATE = 64
S5_CHUNK = 16
S5_ROW = S5_CHUNK * S5_GROUP_CH
ATT_HEADS = 8
ATT_HEAD_DIM = D_MODEL // (2 * ATT_HEADS)
ATT_WIDTH = ATT_HEADS * 2 * ATT_HEAD_DIM
ROPE_THETA = 10000.0
N_IN = S5_WIDTH + 3 * ATT_WIDTH + 2 * D_MODEL
PEER_HEADS = 8
PEER_NKEYS = 128
PEER_EXPERTS = PEER_NKEYS * PEER_NKEYS
PEER_TOPK = 16
PEER_QDIM = 256
PEER_HALF = PEER_QDIM // 2
PEER_NQ = PEER_HEADS * PEER_QDIM
DN_ALPHA = (2 * DEPTH) ** 0.25
LN_EPS = 1e-5

LANES = 128
SUBLANES = 8
VMEM_LIMIT_BYTES = 56 * 1024 * 1024
NEG_BIG = -3.0e38

IN_TILE = 512
IN_ROW_CHUNKS = 2
ATT_TILE = 2048
ATT_ROW_CHUNKS = 4
MERGE_TILE = 512
MERGE_ROW_CHUNKS = 2
ROUTE_TILE = 128
DEPTH_DIRECT = 4
ROUTE_PAIRS_PER_TRIP = 2
EXPERT_TOK_TILE = 1024
EXPERT_CHUNK = 1024


def _merge_exchange_pairs(n):
    pairs = []
    p = 1
    while p < n:
        k = p
        while k >= 1:
            for j in range(k % p, n - k, 2 * k):
                for i in range(min(k, n - j - k)):
                    if (i + j) // (2 * p) == (i + j + k) // (2 * p):
                        pairs.append((i + j, i + j + k))
            k //= 2
        p *= 2
    return pairs


_SLAB_SORT_PAIRS = _merge_exchange_pairs(PEER_NKEYS // SUBLANES)
_CAND_SORT_PAIRS = _merge_exchange_pairs(8)


def _cparams(sem):
    return pltpu.CompilerParams(dimension_semantics=sem, vmem_limit_bytes=VMEM_LIMIT_BYTES)


def _const_spec(shape):
    nd = len(shape)
    return pl.BlockSpec(shape, lambda *_: (0,) * nd, pipeline_mode=pl.Buffered(1))


def _ln(x):
    mu = jnp.mean(x, axis=-1, keepdims=True)
    xc = x - mu
    var = jnp.mean(xc * xc, axis=-1, keepdims=True)
    return xc * lax.rsqrt(var + LN_EPS)


def _gelu_tanh(x):
    return 0.5 * x * (1.0 + jnp.tanh(math.sqrt(2.0 / math.pi) * (x + 0.044715 * (x * x * x))))


def _sigmoid(x):
    return 1.0 / (1.0 + jnp.exp(-x))


def _ada_kernel(c_ref, w_ref, b_ref, o_ref):
    c = c_ref[...]
    s = c * _sigmoid(c)
    o_ref[...] = jnp.dot(s, w_ref[...], precision=HIGHEST, preferred_element_type=F32) + b_ref[...]


def _ada(c, w_ada, b_ada):
    bsz = c.shape[0]
    nblk = w_ada.shape[1] // D_MODEL
    return pl.pallas_call(
        _ada_kernel,
        grid=(nblk,),
        in_specs=[
            pl.BlockSpec((bsz, D_MODEL), lambda j: (0, 0)),
            pl.BlockSpec((D_MODEL, D_MODEL), lambda j: (0, j)),
            pl.BlockSpec((1, D_MODEL), lambda j: (0, j)),
        ],
        out_specs=pl.BlockSpec((bsz, D_MODEL), lambda j: (0, j)),
        out_shape=jax.ShapeDtypeStruct((bsz, nblk * D_MODEL), F32),
        compiler_params=_cparams(("arbitrary",)),
        name="ada",
    )(c, w_ada, b_ada.reshape(1, -1))


def _in_kernel(x_ref, mod_ref, w_ref, cos_ref, sa_ref, sb_ref,
               u_ref, q_ref, k_ref, v_ref, ga_ref, gb_ref):
    mod = mod_ref[...]
    rc = x_ref.shape[0] // IN_ROW_CHUNKS
    rows = [slice(c * rc, (c + 1) * rc) for c in range(IN_ROW_CHUNKS)]
    hbs = [(_ln(x_ref[r, :]) * (1.0 + mod[1:2]) + mod[0:1]).astype(BF16) for r in rows]

    def proj(c, a, b):
        return jnp.dot(hbs[c], w_ref[:, a:b], preferred_element_type=F32)

    o1 = S5_WIDTH
    o2 = o1 + ATT_WIDTH
    o3 = o2 + ATT_WIDTH
    o4 = o3 + ATT_WIDTH
    o5 = o4 + D_MODEL
    half = ATT_HEAD_DIM // 2
    q_scale = ATT_HEAD_DIM ** -0.5 * math.log2(math.e)

    def rope_store(t, o_ref, r, scale):
        cos, sa, sb = cos_ref[r, :] * scale, sa_ref[r, :] * scale, sb_ref[r, :] * scale
        for j in range(ATT_WIDTH // LANES):
            tj = t[:, j * LANES:(j + 1) * LANES]
            rot = tj * cos + pltpu.roll(tj, LANES - half, 1) * sa + pltpu.roll(tj, half, 1) * sb
            o_ref[r, j * LANES:(j + 1) * LANES] = rot.astype(BF16)

    for c, r in enumerate(rows):
        u_ref[r, :] = proj(c, 0, o1).astype(BF16)
    for c, r in enumerate(rows):
        rope_store(proj(c, o1, o2), q_ref, r, q_scale)
    for c, r in enumerate(rows):
        rope_store(proj(c, o2, o3), k_ref, r, 1.0)
    for c, r in enumerate(rows):
        v_ref[r, :] = proj(c, o3, o4).astype(BF16)
    for c, r in enumerate(rows):
        ga_ref[r, :] = _sigmoid(proj(c, o4, o5)).astype(BF16)
    for c, r in enumerate(rows):
        gb_ref[r, :] = _sigmoid(proj(c, o5, N_IN)).astype(BF16)


def _inproj(x, mod3, w_in_b, cos, sa, sb):
    bsz, seq, _ = x.shape
    tm = min(IN_TILE, seq)
    tok = lambda n: pl.BlockSpec((None, tm, n), lambda b, i: (b, i, 0))
    rope = pl.BlockSpec((tm, LANES), lambda b, i: (i, 0))
    outs = [(S5_WIDTH, BF16), (ATT_WIDTH, BF16), (ATT_WIDTH, BF16), (ATT_WIDTH, BF16),
            (D_MODEL, BF16), (D_MODEL, BF16)]
    return pl.pallas_call(
        _in_kernel,
        grid=(bsz, seq // tm),
        in_specs=[tok(D_MODEL),
                  pl.BlockSpec((None, 6, D_MODEL), lambda b, i: (b, 0, 0)),
                  _const_spec((D_MODEL, N_IN)), rope, rope, rope],
        out_specs=[tok(n) for n, _ in outs],
        out_shape=[jax.ShapeDtypeStruct((bsz, seq, n), dt) for n, dt in outs],
        compiler_params=_cparams(("parallel", "parallel")),
        name="inproj",
    )(x, mod3, w_in_b, cos, sa, sb)


def _s5_weights(lam_re, lam_im, log_dt, b_re, b_im, c_re, c_im):
    n = S5_CHUNK
    per_dir = []
    for d in range(2):
        lr = lam_re[d].astype(F32)
        li = lam_im[d].astype(F32)
        dt = jnp.exp(log_dt[d].astype(F32))[:, None]
        mag = jnp.exp(lr * dt)
        lbr = mag * jnp.cos(li * dt)
        lbi = mag * jnp.sin(li * dt)
        den = lr * lr + li * li
        nr = lbr - 1.0
        cr = (nr * lr + lbi * li) / den
        ci = (lbi * lr - nr * li) / den
        br = b_re[d].astype(F32)
        bi = b_im[d].astype(F32)
        bbr = cr[..., None] * br - ci[..., None] * bi
        bbi = cr[..., None] * bi + ci[..., None] * br
        pr = [jnp.ones_like(lbr)]
        pi = [jnp.zeros_like(lbi)]
        for _ in range(n):
            pr.append(pr[-1] * lbr - pi[-1] * lbi)
            pi.append(pr[-2] * lbi + pi[-1] * lbr)
        pr = jnp.stack(pr)
        pi = jnp.stack(pi)
        cre = c_re[d].astype(F32)
        cim = c_im[d].astype(F32)
        cpr = cre[None] * pr[:, :, None, :] - cim[None] * pi[:, :, None, :]
        cpi = cre[None] * pi[:, :, None, :] + cim[None] * pr[:, :, None, :]
        kern = (jnp.einsum('ngcp,gpd->ngcd', cpr, bbr, precision=HIGHEST)
                - jnp.einsum('ngcp,gpd->ngcd', cpi, bbi, precision=HIGHEST))
        pbr = pr[..., None] * bbr[None] - pi[..., None] * bbi[None]
        pbi = pr[..., None] * bbi[None] + pi[..., None] * bbr[None]
        per_dir.append(dict(pr=pr, pi=pi, cpr=cpr, cpi=cpi, kern=kern, pbr=pbr, pbi=pbi))

    g, c, p = S5_GROUPS, S5_GROUP_CH, S5_STATE
    s_idx = jnp.arange(n)[:, None]
    t_idx = jnp.arange(n)[None, :]
    tau_f = t_idx - s_idx
    tau_b = s_idx - t_idx
    kf = per_dir[0]['kern'][jnp.clip(tau_f, 0, n)] * (tau_f >= 0)[..., None, None, None]
    kb = per_dir[1]['kern'][jnp.clip(tau_b, 0, n)] * (tau_b >= 0)[..., None, None, None]
    m = (kf + kb).transpose(2, 0, 4, 1, 3).reshape(g, S5_ROW, S5_ROW)

    pow_f = n - 1 - jnp.arange(n)
    pow_b = jnp.arange(n)

    def ws_part(key, d, pows):
        return per_dir[d][key][pows].transpose(1, 0, 3, 2).reshape(g, S5_ROW, p)

    ws = jnp.concatenate([ws_part('pbr', 0, pow_f), ws_part('pbr', 1, pow_b),
                          ws_part('pbi', 0, pow_f), ws_part('pbi', 1, pow_b)], axis=-1)

    out_f = jnp.arange(n) + 1
    out_b = n - jnp.arange(n)

    def wo_part(key, d, pows):
        return per_dir[d][key][pows].transpose(1, 3, 0, 2).reshape(g, p, S5_ROW)

    zero = jnp.zeros((g, p, S5_ROW), F32)
    wof = jnp.concatenate([wo_part('cpr', 0, out_f), zero, -wo_part('cpi', 0, out_f), zero], axis=1)
    wob = jnp.concatenate([zero, wo_part('cpr', 1, out_b), zero, -wo_part('cpi', 1, out_b)], axis=1)
    ar = jnp.concatenate([per_dir[0]['pr'][n], per_dir[1]['pr'][n]], axis=-1)[:, None, :]
    ai = jnp.concatenate([per_dir[0]['pi'][n], per_dir[1]['pi'][n]], axis=-1)[:, None, :]
    return m.astype(BF16), ws.astype(BF16), wof.astype(BF16), wob.astype(BF16), ar, ai


def _s5_kernel(u_ref, m_ref, ws_ref, wof_ref, wob_ref, ar_ref, ai_ref, y_ref,
               s_ref, hf_ref, hb_ref, *, nb, nchunk):
    u = u_ref[...].T
    rows = nb * nchunk

    def chunk_major(x):
        return jnp.swapaxes(x.reshape(nb, nchunk, x.shape[-1]), 0, 1).reshape(rows, x.shape[-1])

    def batch_major(x):
        return jnp.swapaxes(x.reshape(nchunk, nb, x.shape[-1]), 0, 1).reshape(rows, x.shape[-1])

    s_ref[...] = chunk_major(jnp.dot(u, ws_ref[...], preferred_element_type=F32))
    ar = ar_ref[...]
    ai = ai_ref[...]
    is_fwd = lax.broadcasted_iota(jnp.int32, (nb, LANES), 1) < S5_STATE

    def step(i, carry):
        h_re, h_im = carry
        rf = pl.multiple_of(i * nb, 8)
        rb = pl.multiple_of((nchunk - 1 - i) * nb, 8)
        hf_ref[pl.ds(rf, nb), 0:LANES] = h_re
        hf_ref[pl.ds(rf, nb), LANES:2 * LANES] = h_im
        hb_ref[pl.ds(rb, nb), 0:LANES] = h_re
        hb_ref[pl.ds(rb, nb), LANES:2 * LANES] = h_im
        s_re = jnp.where(is_fwd, s_ref[pl.ds(rf, nb), 0:LANES], s_ref[pl.ds(rb, nb), 0:LANES])
        s_im = jnp.where(is_fwd, s_ref[pl.ds(rf, nb), LANES:2 * LANES],
                         s_ref[pl.ds(rb, nb), LANES:2 * LANES])
        return (ar * h_re - ai * h_im + s_re, ar * h_im + ai * h_re + s_im)

    zero = jnp.zeros((nb, LANES), F32)
    lax.fori_loop(0, nchunk, step, (zero, zero))
    y = jnp.dot(u, m_ref[...], preferred_element_type=F32)
    y = y + jnp.dot(batch_major(hf_ref[...]).astype(BF16), wof_ref[...], preferred_element_type=F32)
    y = y + jnp.dot(batch_major(hb_ref[...]).astype(BF16), wob_ref[...], preferred_element_type=F32)
    y_ref[...] = y.T.astype(BF16)


def _s5(u, s5w):
    m, ws, wof, wob, ar, ai = s5w
    bsz, seq, _ = u.shape
    nchunk = seq // S5_CHUNK
    rows = nchunk * bsz
    u2 = (u.reshape(bsz, nchunk, S5_CHUNK, S5_GROUPS, S5_GROUP_CH)
          .transpose(3, 2, 4, 0, 1).reshape(S5_GROUPS, S5_ROW, rows))
    grp = lambda r, c: pl.BlockSpec((None, r, c), lambda g: (g, 0, 0))
    y2 = pl.pallas_call(
        functools.partial(_s5_kernel, nb=bsz, nchunk=nchunk),
        grid=(S5_GROUPS,),
        in_specs=[grp(S5_ROW, rows), grp(S5_ROW, S5_ROW), grp(S5_ROW, S5_ROW),
                  grp(S5_ROW, S5_ROW), grp(S5_ROW, S5_ROW), grp(1, LANES), grp(1, LANES)],
        out_specs=grp(S5_ROW, rows),
        out_shape=jax.ShapeDtypeStruct((S5_GROUPS, S5_ROW, rows), BF16),
        scratch_shapes=[pltpu.VMEM((rows, S5_ROW), F32)] * 3,
        compiler_params=_cparams(("parallel",)),
        name="s5",
    )(u2, m, ws, wof, wob, ar, ai)
    return (y2.reshape(S5_GROUPS, S5_CHUNK, S5_GROUP_CH, bsz, nchunk)
            .transpose(3, 4, 1, 0, 2).reshape(bsz, seq, S5_WIDTH))


def _attn_kernel(lam_ref, q_ref, k_ref, v_ref, g_ref, o_ref, vx_ref):
    @pl.when(pl.program_id(2) == 0)
    def _():
        vx_ref[:, 0:LANES] = v_ref[...]
        vx_ref[:, LANES:2 * LANES] = jnp.ones(v_ref.shape, BF16)

    k = k_ref[...]
    vx = vx_ref[...]
    rows_per_chunk = q_ref.shape[0] // ATT_ROW_CHUNKS

    def scores(qm):
        return lax.dot_general(qm, k, (((1,), (1,)), ((), ())), preferred_element_type=F32)

    def probs(s):
        return jnp.exp2(s - jnp.max(s, axis=-1, keepdims=True)).astype(BF16)

    def weighted(e):
        ol = jnp.dot(e, vx, preferred_element_type=F32)
        return ol[:, 0:LANES] / ol[:, LANES:2 * LANES]

    chains = []
    for c in range(ATT_ROW_CHUNKS):
        rows = slice(c * rows_per_chunk, (c + 1) * rows_per_chunk)
        q = q_ref[rows, :]
        first = lax.broadcasted_iota(jnp.int32, q.shape, 1) < ATT_HEAD_DIM
        zero = jnp.zeros_like(q)
        chains.append((rows, jnp.where(first, q, zero)))
        chains.append((rows, jnp.where(first, zero, q)))

    s_val, e_val, o_val = {}, {}, {}
    for t in range(len(chains) + 2):
        if t < len(chains):
            s_val[t] = scores(chains[t][1])
        if 0 <= t - 1 < len(chains):
            e_val[t - 1] = probs(s_val.pop(t - 1))
        if 0 <= t - 2 < len(chains):
            o_val[t - 2] = weighted(e_val.pop(t - 2))
            if (t - 2) % 2 == 1:
                o = o_val.pop(t - 3) - lam_ref[0] * o_val.pop(t - 2)
                o = o * lax.rsqrt(jnp.mean(o * o, axis=-1, keepdims=True) + LN_EPS) * g_ref[...]
                o_ref[chains[t - 2][0], :] = o.astype(BF16)


def _attention(lam, q, k, v, g_row):
    bsz, seq, _ = q.shape
    tq = min(ATT_TILE, seq)
    return pl.pallas_call(
        _attn_kernel,
        grid=(bsz, ATT_HEADS, seq // tq),
        in_specs=[pl.BlockSpec(memory_space=pltpu.SMEM),
                  pl.BlockSpec((None, tq, LANES), lambda b, h, i: (b, i, h)),
                  pl.BlockSpec((None, seq, LANES), lambda b, h, i: (b, 0, h)),
                  pl.BlockSpec((None, seq, LANES), lambda b, h, i: (b, 0, h)),
                  pl.BlockSpec((1, LANES), lambda b, h, i: (0, 0))],
        out_specs=pl.BlockSpec((None, tq, LANES), lambda b, h, i: (b, i, h)),
        out_shape=jax.ShapeDtypeStruct((bsz, seq, ATT_WIDTH), BF16),
        scratch_shapes=[pltpu.VMEM((seq, 2 * LANES), BF16)],
        compiler_params=_cparams(("parallel", "parallel", "arbitrary")),
        name="attn",
    )(lam, q, k, v, g_row)


def _merge_kernel(x_ref, u_ref, y_ref, o_ref, ga_ref, gb_ref, mod_ref, d_ref, wglu_ref, ws5_ref,
                  watt_ref, wo_ref, ln1g_ref, ln1b_ref, x1_ref, h2_ref):
    mod = mod_ref[...]
    rc = x_ref.shape[0] // MERGE_ROW_CHUNKS
    rows = [slice(c * rc, (c + 1) * rc) for c in range(MERGE_ROW_CHUNKS)]
    dot = lambda a, b: jnp.dot(a, b, preferred_element_type=F32)
    s = [_gelu_tanh(u_ref[r, :].astype(F32) * d_ref[...] + y_ref[r, :].astype(F32)) for r in rows]
    glu = [dot(v.astype(BF16), wglu_ref[...]) for v in s]
    s = [a * _sigmoid(b) for a, b in zip(s, glu)]
    br_b = [dot(o_ref[r, :], watt_ref[...]) for r in rows]
    br_a = [dot(v.astype(BF16), ws5_ref[...]) for v in s]
    merged = [ga_ref[r, :].astype(F32) * a + gb_ref[r, :].astype(F32) * b
              for r, a, b in zip(rows, br_a, br_b)]
    z = [dot(v.astype(BF16), wo_ref[...]) for v in merged]
    for r, zc in zip(rows, z):
        x1 = _ln(DN_ALPHA * x_ref[r, :] + mod[2:3] * zc) * ln1g_ref[...] + ln1b_ref[...]
        x1_ref[r, :] = x1
        h2_ref[r, :] = (_ln(x1) * (1.0 + mod[4:5]) + mod[3:4]).astype(BF16)


def _merge(x, u, y, y_off, o, ga, gb, mod3, wts):
    bsz, seq, _ = x.shape
    tm = min(MERGE_TILE, seq)
    tok = lambda n: pl.BlockSpec((None, tm, n), lambda b, i: (b, i, 0))
    y_spec = pl.BlockSpec((None, tm, S5_WIDTH), lambda b, i: (b + y_off, i, 0))
    row = lambda n: _const_spec((1, n))
    return pl.pallas_call(
        _merge_kernel,
        grid=(bsz, seq // tm),
        in_specs=[tok(D_MODEL), tok(S5_WIDTH), y_spec, tok(ATT_WIDTH), tok(D_MODEL), tok(D_MODEL),
                  pl.BlockSpec((None, 6, D_MODEL), lambda b, i: (b, 0, 0)),
                  row(S5_WIDTH), _const_spec((S5_WIDTH, S5_WIDTH)), _const_spec((S5_WIDTH, D_MODEL)),
                  _const_spec((ATT_WIDTH, D_MODEL)), _const_spec((D_MODEL, D_MODEL)),
                  row(D_MODEL), row(D_MODEL)],
        out_specs=[tok(D_MODEL), tok(D_MODEL)],
        out_shape=[jax.ShapeDtypeStruct((bsz, seq, D_MODEL), F32),
                   jax.ShapeDtypeStruct((bsz, seq, D_MODEL), BF16)],
        compiler_params=_cparams(("parallel", "parallel")),
        name="merge",
    )(x, u, y, o, ga, gb, mod3, wts['s5_d'], wts['w_glu'], wts['w_s5_out'], wts['w_attn_out'],
      wts['w_o'], wts['ln1_g'], wts['ln1_b'])


def _route_kernel(h_ref, wq_ref, keys_ref, w_ref, a_sc, b_sc, wt_sc, ix_sc, wr_sc, ir_sc, dp_sc, e2_sc):
    tn = h_ref.shape[0]
    h2 = h_ref[...]
    assert PEER_TOPK == 16 and SUBLANES == 8
    row8 = lax.broadcasted_iota(jnp.int32, (SUBLANES, tn), 0)
    row8_f = row8.astype(F32)
    nt_dims = (((1,), (1,)), ((), ()))
    nv = PEER_NKEYS // SUBLANES

    def sorted_slabs(st, with_index):
        vals = [st[SUBLANES * v:SUBLANES * (v + 1), :] for v in range(nv)]
        idxs = [row8_f + float(SUBLANES * v) for v in range(nv)] if with_index else None
        for lo, hi in _SLAB_SORT_PAIRS:
            a, b = vals[lo], vals[hi]
            vals[lo], vals[hi] = jnp.maximum(a, b), jnp.minimum(a, b)
            if with_index:
                keep = a >= b
                ia, ib = idxs[lo], idxs[hi]
                idxs[lo], idxs[hi] = jnp.where(keep, ia, ib), jnp.where(keep, ib, ia)
        return vals, idxs

    def pop_heads(vals, idxs, pick, k):
        for d in range(min(len(vals) - 1, PEER_TOPK - 1 - k)):
            vals[d] = jnp.where(pick, vals[d + 1], vals[d])
            if idxs is not None:
                idxs[d] = jnp.where(pick, idxs[d + 1], idxs[d])

    def head_scores(h):
        pq = jnp.dot(h2, wq_ref[:, h * PEER_QDIM:(h + 1) * PEER_QDIM],
                     preferred_element_type=F32).astype(BF16)
        q1 = pq[:, 0:PEER_HALF]
        q2 = pq[:, PEER_HALF:PEER_QDIM]
        return (lax.dot_general(keys_ref[2 * h], q1, nt_dims, preferred_element_type=F32),
                lax.dot_general(keys_ref[2 * h + 1], q2, nt_dims, preferred_element_type=F32))

    upcoming = head_scores(0)
    for h in range(PEER_HEADS):
        st1, st2 = upcoming
        if h + 1 < PEER_HEADS:
            upcoming = head_scores(h + 1)
        s2 = st2.T
        e2_sc[h] = jnp.exp(s2 - jnp.max(s2, axis=-1, keepdims=True))

        vals, idxs = sorted_slabs(st1, True)
        a_vals = []
        for k in range(PEER_TOPK):
            m = jnp.max(vals[0], axis=0, keepdims=True)
            ix = jnp.min(jnp.where(vals[0] == m, idxs[0], float(PEER_NKEYS)), axis=0, keepdims=True)
            pop_heads(vals, idxs, idxs[0] == ix, k)
            a_vals.append(m)
            a_sc[k:k + 1, :] = m
            ix_sc[h * PEER_TOPK + k:h * PEER_TOPK + k + 1, :] = ix
        vals, _ = sorted_slabs(st2, False)
        for k in range(PEER_TOPK):
            m = jnp.max(vals[0], axis=0, keepdims=True)
            pop_heads(vals, None, vals[0] == m, k)
            b_sc[k:k + 1, :] = m
        b_lo = b_sc[0:8, :]
        b_hi = b_sc[8:16, :]
        b0 = b_sc[0:1, :]

        shift = lambda x, n: pltpu.roll(x, n, 0)
        cands = [
            a_vals[0] + b_lo,
            a_vals[0] + b_hi,
            a_vals[1] + b_lo,
            jnp.where(row8 < 5, a_vals[2] + b_lo, a_vals[4] + shift(b_lo, 5)),
            jnp.where(row8 < 4, a_vals[3] + b_lo,
                      jnp.where(row8 < 6, a_vals[5] + shift(b_lo, 4), a_vals[6] + shift(b_lo, 6))),
            jnp.where(row8 < 2, a_vals[7] + b_lo, a_sc[6:14, :] + b0),
            jnp.where(row8 < 2, shift(a_sc[8:16, :], 2) + b0, NEG_BIG),
        ]
        m0 = a_vals[0] + b0
        z = jnp.zeros_like(m0)
        tau = m0
        cands.append(jnp.full_like(cands[0], NEG_BIG))
        for lo, hi in _CAND_SORT_PAIRS:
            a, b = cands[lo], cands[hi]
            cands[lo], cands[hi] = jnp.maximum(a, b), jnp.minimum(a, b)
        for k in range(PEER_TOPK):
            tau = jnp.max(cands[0], axis=0, keepdims=True)
            z = z + jnp.exp(tau - m0)
            pop_heads(cands, None, cands[0] == tau, k)
        inv_z = 1.0 / z
        for k in range(PEER_TOPK):
            j = h * PEER_TOPK + k
            wt_sc[j:j + 1, :] = jnp.exp(a_vals[k] - a_vals[0]) * inv_z

        depth = jnp.zeros_like(st2)
        for k in range(DEPTH_DIRECT):
            depth = depth + jnp.where(a_vals[k] + st2 >= tau, 1.0, 0.0)
        for l in range(PEER_TOPK // (DEPTH_DIRECT + 1)):
            b_l = b_sc[l:l + 1, :]
            extra = jnp.zeros_like(b_l)
            for k in range(DEPTH_DIRECT, PEER_TOPK // (l + 1)):
                extra = extra + jnp.where(a_vals[k] + b_l >= tau, 1.0, 0.0)
            depth = depth + jnp.where(st2 == b_l, extra, 0.0)
        dp_sc[h] = depth.T

    wr_sc[...] = wt_sc[...].T
    ir_sc[...] = ix_sc[...].T
    rank_j = (lax.broadcasted_iota(jnp.int32, (PEER_NKEYS, PEER_NKEYS), 0) % PEER_TOPK).astype(F32).astype(BF16)
    sub_j = lax.broadcasted_iota(jnp.int32, (PEER_NKEYS, PEER_NKEYS), 0).astype(F32).astype(BF16)
    zero_b = jnp.zeros((PEER_NKEYS, PEER_NKEYS), BF16)

    def token(t):
        def head_rows(ref):
            return jnp.concatenate(
                [jnp.broadcast_to(ref[h, pl.ds(t, 1), :], (PEER_TOPK, PEER_NKEYS)).astype(BF16)
                 for h in range(PEER_HEADS)], axis=0)

        def all_rows(ref):
            return jnp.broadcast_to(ref[pl.ds(t, 1), :], (PEER_NKEYS, PEER_NKEYS)).astype(BF16)

        r = jnp.where(rank_j < head_rows(dp_sc), head_rows(e2_sc), zero_b)
        pt = jnp.where(sub_j == all_rows(ir_sc), all_rows(wr_sc), zero_b)
        return jnp.dot(pt, r, preferred_element_type=F32)

    def token_group(g, carry):
        for pair in range(ROUTE_PAIRS_PER_TRIP):
            t0 = pl.multiple_of((g * ROUTE_PAIRS_PER_TRIP + pair) * 16, 16)
            halves = []
            for half in range(2):
                tiles = jnp.stack([token(t0 + 8 * half + u) for u in range(8)])
                halves.append(jnp.swapaxes(tiles, 0, 1))
            w_ref[:, pl.ds(t0, 16), :] = jnp.concatenate(halves, axis=1).astype(BF16)
        return carry

    lax.fori_loop(0, tn // (16 * ROUTE_PAIRS_PER_TRIP), token_group, 0)


def _route(h2, wq_b, keys_b):
    ntok = h2.shape[0]
    tn = ROUTE_TILE
    nj = PEER_HEADS * PEER_TOPK
    return pl.pallas_call(
        _route_kernel,
        grid=(ntok // tn,),
        in_specs=[pl.BlockSpec((tn, D_MODEL), lambda i: (i, 0)),
                  _const_spec((D_MODEL, PEER_NQ)),
                  _const_spec((2 * PEER_HEADS, PEER_NKEYS, PEER_HALF))],
        out_specs=pl.BlockSpec((None, PEER_NKEYS, tn, PEER_NKEYS), lambda i: (i, 0, 0, 0)),
        out_shape=jax.ShapeDtypeStruct((ntok // tn, PEER_NKEYS, tn, PEER_NKEYS), BF16),
        scratch_shapes=[pltpu.VMEM((PEER_TOPK, tn), F32), pltpu.VMEM((PEER_TOPK, tn), F32),
                        pltpu.VMEM((nj, tn), F32), pltpu.VMEM((nj, tn), F32),
                        pltpu.VMEM((tn, nj), F32), pltpu.VMEM((tn, nj), F32),
                        pltpu.VMEM((PEER_HEADS, tn, PEER_NKEYS), F32),
                        pltpu.VMEM((PEER_HEADS, tn, PEER_NKEYS), F32)],
        compiler_params=_cparams(("parallel",)),
        name="route",
    )(h2, wq_b, keys_b)


def _expert_kernel(h_ref, ut_ref, v_ref, w_ref, x1_ref, mod_ref, g_ref, b_ref, o_ref, acc_ref, gate_sc):
    e = pl.program_id(2)

    @pl.when(e == 0)
    def _():
        acc_ref[...] = jnp.zeros_like(acc_ref)

    act = jnp.dot(h_ref[...], ut_ref[...], preferred_element_type=F32)
    rt = w_ref.shape[2]
    for j in range(w_ref.shape[0]):
        for k in range(w_ref.shape[1]):
            a = act[j * rt:(j + 1) * rt, k * PEER_NKEYS:(k + 1) * PEER_NKEYS].astype(BF16)
            gate_sc[j * rt:(j + 1) * rt, k * PEER_NKEYS:(k + 1) * PEER_NKEYS] = (
                _gelu_tanh(a) * w_ref[j, k])
    acc_ref[...] += jnp.dot(gate_sc[...], v_ref[...], preferred_element_type=F32)

    @pl.when(e == pl.num_programs(2) - 1)
    def _():
        mod = mod_ref[...]
        o_ref[...] = _ln(DN_ALPHA * x1_ref[...] + mod[5:6] * acc_ref[...]) * g_ref[...] + b_ref[...]


def _experts(h2, w4, x1, mod3, ut_b, v_b, ln2_g, ln2_b):
    bsz, seq, _ = x1.shape
    tt = min(EXPERT_TOK_TILE, seq)
    ce = EXPERT_CHUNK
    rt = w4.shape[2]
    nblk = seq // tt
    tok = lambda n: pl.BlockSpec((None, tt, n), lambda b, i, e: (b, i, 0))
    return pl.pallas_call(
        _expert_kernel,
        grid=(bsz, nblk, PEER_EXPERTS // ce),
        in_specs=[tok(D_MODEL),
                  pl.BlockSpec((None, D_MODEL, ce), lambda b, i, e: (e, 0, 0)),
                  pl.BlockSpec((ce, D_MODEL), lambda b, i, e: (e, 0)),
                  pl.BlockSpec((tt // rt, ce // PEER_NKEYS, rt, PEER_NKEYS),
                               lambda b, i, e: (b * nblk + i, e, 0, 0)),
                  tok(D_MODEL),
                  pl.BlockSpec((None, 6, D_MODEL), lambda b, i, e: (b, 0, 0)),
                  pl.BlockSpec((1, D_MODEL), lambda b, i, e: (0, 0)),
                  pl.BlockSpec((1, D_MODEL), lambda b, i, e: (0, 0))],
        out_specs=tok(D_MODEL),
        out_shape=jax.ShapeDtypeStruct((bsz, seq, D_MODEL), F32),
        scratch_shapes=[pltpu.VMEM((tt, D_MODEL), F32), pltpu.VMEM((tt, ce), BF16)],
        compiler_params=_cparams(("parallel", "parallel", "arbitrary")),
        name="expert",
    )(h2, ut_b, v_b, w4, x1, mod3, ln2_g, ln2_b)


def _rope_tables(seq):
    half = ATT_HEAD_DIM // 2
    inv_freq = ROPE_THETA ** (-jnp.arange(half, dtype=F32) * 2.0 / ATT_HEAD_DIM)
    ang = jnp.arange(seq, dtype=F32)[:, None] * inv_freq[None, :]
    cos = jnp.cos(ang)
    sin = jnp.sin(ang)
    zero = jnp.zeros_like(sin)
    reps = LANES // ATT_HEAD_DIM
    cos_t = jnp.tile(jnp.concatenate([cos, cos], axis=1), (1, reps))
    sa_t = jnp.tile(jnp.concatenate([-sin, zero], axis=1), (1, reps))
    sb_t = jnp.tile(jnp.concatenate([zero, sin], axis=1), (1, reps))
    return cos_t, sa_t, sb_t


def _prepare(layer_idx, p):
    lam_init = 0.8 - 0.6 * math.exp(-0.3 * layer_idx)
    lam = (jnp.exp(jnp.sum(p['attn_lambda_q1'].astype(F32) * p['attn_lambda_k1'].astype(F32)))
           - jnp.exp(jnp.sum(p['attn_lambda_q2'].astype(F32) * p['attn_lambda_k2'].astype(F32))) + lam_init)
    row = lambda a: a.astype(F32).reshape(1, -1)
    return dict(
        w_ada=p['w_ada'].astype(F32), b_ada=p['b_ada'].astype(F32),
        w_in=p['w_in'].astype(BF16),
        s5w=_s5_weights(p['s5_lambda_re'], p['s5_lambda_im'], p['s5_log_dt'], p['s5_b_re'],
                        p['s5_b_im'], p['s5_c_re'], p['s5_c_im']),
        s5_d=row(p['s5_d']), w_glu=p['s5_w_glu'].astype(BF16), w_s5_out=p['w_s5_out'].astype(BF16),
        lam=lam.reshape(1).astype(F32),
        subln=row(p['attn_subln_g']) * (1.0 - lam_init),
        w_attn_out=p['w_attn_out'].astype(BF16), w_o=p['w_o'].astype(BF16),
        ln1_g=row(p['ln1_g']), ln1_b=row(p['ln1_b']),
        w_q=p['peer_w_q'].astype(BF16),
        keys=p['peer_keys'].astype(BF16).reshape(2 * PEER_HEADS, PEER_NKEYS, PEER_HALF),
        ut=(p['peer_u'].astype(BF16).reshape(PEER_EXPERTS // EXPERT_CHUNK, EXPERT_CHUNK, D_MODEL)
            .transpose(0, 2, 1)),
        v=p['peer_v'].astype(BF16),
        ln2_g=row(p['ln2_g']), ln2_b=row(p['ln2_b']),
    )


def _encoder_layers(xs, cs, wts, ropes):
    projs = []
    for x, c, rope in zip(xs, cs, ropes):
        bsz = x.shape[0]
        mod3 = _ada(c.astype(F32), wts['w_ada'], wts['b_ada']).reshape(bsz, 6, D_MODEL)
        projs.append((mod3,) + tuple(_inproj(x.astype(F32), mod3, wts['w_in'], *rope)))
    y_all = _s5(jnp.concatenate([p[1] for p in projs], axis=0), wts['s5w'])
    fronts = [(mod3, u, _attention(wts['lam'], q, k, v, wts['subln']), ga, gb)
              for mod3, u, q, k, v, ga, gb in projs]
    outs, y_off = [], 0
    for x, (mod3, u, o, ga, gb) in zip(xs, fronts):
        bsz, seq, _ = x.shape
        x1, h2 = _merge(x.astype(F32), u, y_all, y_off, o, ga, gb, mod3, wts)
        w4 = _route(h2.reshape(bsz * seq, D_MODEL), wts['w_q'], wts['keys'])
        x2 = _experts(h2, w4, x1, mod3, wts['ut'], wts['v'], wts['ln2_g'], wts['ln2_b'])
        outs.append(x2.astype(x.dtype))
        y_off += bsz
    return outs


def _encoder_layer(x, c, wts, rope):
    return _encoder_layers([x], [c], wts, [rope])[0]


def kernel(x_prompt, x_sample, c_prompt, c_sample, w_ada, b_ada, w_in, s5_lambda_re, s5_lambda_im, s5_log_dt, s5_b_re, s5_b_im, s5_c_re, s5_c_im, s5_d, s5_w_glu, w_s5_out, attn_lambda_q1, attn_lambda_k1, attn_lambda_q2, attn_lambda_k2, attn_subln_g, w_attn_out, w_o, ln1_g, ln1_b, peer_w_q, peer_keys, peer_u, peer_v, ln2_g, ln2_b):
    params = dict(w_ada=w_ada, b_ada=b_ada, w_in=w_in, s5_lambda_re=s5_lambda_re,
                  s5_lambda_im=s5_lambda_im, s5_log_dt=s5_log_dt, s5_b_re=s5_b_re, s5_b_im=s5_b_im,
                  s5_c_re=s5_c_re, s5_c_im=s5_c_im, s5_d=s5_d, s5_w_glu=s5_w_glu, w_s5_out=w_s5_out,
                  attn_lambda_q1=attn_lambda_q1, attn_lambda_k1=attn_lambda_k1,
                  attn_lambda_q2=attn_lambda_q2, attn_lambda_k2=attn_lambda_k2,
                  attn_subln_g=attn_subln_g, w_attn_out=w_attn_out, w_o=w_o, ln1_g=ln1_g, ln1_b=ln1_b,
                  peer_w_q=peer_w_q, peer_keys=peer_keys, peer_u=peer_u, peer_v=peer_v,
                  ln2_g=ln2_g, ln2_b=ln2_b)
    y_prompt, y_sample = x_prompt, x_sample
    rope_p = _rope_tables(x_prompt.shape[1])
    rope_s = _rope_tables(x_sample.shape[1])
    for l in range(DEPTH):
        wts = _prepare(l, {name: a[l] for name, a in params.items()})
        if y_prompt.shape[1] == y_sample.shape[1]:
            y_prompt, y_sample = _encoder_layers([y_prompt, y_sample], [c_prompt, c_sample], wts,
                                                 [rope_p, rope_s])
        else:
            y_prompt = _encoder_layer(y_prompt, c_prompt, wts, rope_p)
            y_sample = _encoder_layer(y_sample, c_sample, wts, rope_s)
    return (y_prompt, y_sample)
```
